```python
import math
import jax, jax.numpy as jnp
from jax import lax
import numpy as np

D_MODEL = 1024
BATCH = 1
SEQ = 16384
DEPTH = 1
DEC_BATCH = 128
DEC_SEQ = 1
PAST_LEN = 8192
PAGE_SIZE = 128

A_DH = 64
A_DV = 2 * A_DH
A_WIDTH = D_MODEL // 2
A_HEADS = A_WIDTH // A_DV
ROPE_THETA = 10000.0
Q_BLOCK = 128
R_DH = 64
R_WIDTH = D_MODEL // 2
R_HEADS = R_WIDTH // R_DH
DECAY_LORA = 64
AAA_LORA = 64
GATE_LORA = 160
GN_EPS = 64e-5
RMS_EPS = 1e-6
R_OFF_R = 0
R_OFF_K = R_WIDTH
R_OFF_V = 2 * R_WIDTH
R_OFF_W = 3 * R_WIDTH
R_OFF_A = R_OFF_W + DECAY_LORA
R_OFF_G = R_OFF_A + AAA_LORA
R_IN = R_OFF_G + GATE_LORA
COL_Q = 0
COL_K = COL_Q + A_HEADS * 2 * A_DH
COL_V = COL_K + A_HEADS * 2 * A_DH
COL_GA = COL_V + A_WIDTH
COL_GR = COL_GA + D_MODEL
COL_RW = COL_GR + D_MODEL
D_IN = COL_RW + R_IN
N_GROUPS = 4
EXPERTS_PER_GROUP = 8
N_EXPERTS = N_GROUPS * EXPERTS_PER_GROUP
TOP_K = 2
D_EXPERT = 512
MOE_BLOCK = 128

kernel_name = 'hybrid_diffattn_rwkv7_hmoe_adaln_step'


def rmsnorm(x, gain):
    xf = x.astype(jnp.float32)
    y = xf * lax.rsqrt(jnp.mean(xf * xf, axis=-1, keepdims=True) + RMS_EPS)
    return (y * gain.astype(jnp.float32)).astype(x.dtype)


def rope(x, pos):
    half = A_DH // 2
    inv = ROPE_THETA ** (-jnp.arange(half, dtype=jnp.float32) / half)
    ang = pos.astype(jnp.float32)[:, None] * inv[None, :]
    cos = jnp.cos(ang)[:, None, None, :]
    sin = jnp.sin(ang)[:, None, None, :]
    xf = x.astype(jnp.float32)
    x1, x2 = xf[..., :half], xf[..., half:]
    return jnp.concatenate([x1 * cos - x2 * sin, x2 * cos + x1 * sin], axis=-1).astype(x.dtype)


def diff_softmax_mix(q, k, v, mask, lam):
    B, Q, H, _ = q.shape
    T = k.shape[1]
    qc = q.reshape(B, Q, H, 2, A_DH)
    kc = k.reshape(B, T, H, 2, A_DH)
    s = jnp.einsum('bqhcd,bkhcd->bchqk', qc, kc, preferred_element_type=jnp.float32) * (A_DH ** -0.5)
    s = jnp.where(mask, s, jnp.finfo(jnp.float32).min)
    pr = jax.nn.softmax(s, axis=-1)
    diff = pr[:, 0] - lam * pr[:, 1]
    out = jnp.einsum('bhqk,bkhd->bqhd', diff, v.astype(jnp.float32))
    return out.astype(v.dtype)


def diff_attn_prompt(q, k, v, lam):
    B, S, H, _ = q.shape
    nb = S // Q_BLOCK
    qb = jnp.moveaxis(q.reshape(B, nb, Q_BLOCK, H, 2 * A_DH), 1, 0)
    key_pos = jnp.arange(S)

    def block(args):
        q_blk, i = args
        q_pos = i * Q_BLOCK + jnp.arange(Q_BLOCK)
        mask = key_pos[None, :] <= q_pos[:, None]
        return diff_softmax_mix(q_blk, k, v, mask, lam)

    out = lax.map(block, (qb, jnp.arange(nb)))
    return jnp.moveaxis(out, 0, 1).reshape(B, S, H, A_DV)


def diff_attn_sample(q, k, v, lam, cache_k, cache_v, page_table, layer_idx):
    DB, T, H, _ = q.shape
    past = page_table.shape[1] * PAGE_SIZE
    key_pos = jnp.arange(past + T)
    q_pos = past + jnp.arange(T)
    mask = key_pos[None, :] <= q_pos[:, None]

    def one(args):
        q_s, k_s, v_s, pages = args
        k_past = cache_k[layer_idx, pages].reshape(past, H, 2 * A_DH)
        v_past = cache_v[layer_idx, pages].reshape(past, H, A_DV)
        k_all = jnp.concatenate([k_past, k_s.astype(k_past.dtype)], axis=0)
        v_all = jnp.concatenate([v_past, v_s.astype(v_past.dtype)], axis=0)
        return diff_softmax_mix(q_s[None], k_all[None], v_all[None], mask, lam)[0]

    return lax.map(one, (q, k, v, page_table)).astype(v.dtype)


def wkv_scan(state, r, w, k, v, kk, a):
    def step(S, inp):
        r_t, w_t, k_t, v_t, kk_t, a_t = inp
        sa = jnp.einsum('bhvk,bhk->bhv', S, -kk_t)
        S = (S * w_t[:, :, None, :] + sa[..., None] * (kk_t * a_t)[:, :, None, :]
             + v_t[..., None] * k_t[:, :, None, :])
        return S, jnp.einsum('bhvk,bhk->bhv', S, r_t)

    xs = tuple(jnp.moveaxis(t, 1, 0) for t in (r, w, k, v, kk, a))
    S, ys = lax.scan(step, state, xs)
    return S, jnp.moveaxis(ys, 0, 1)


def rwkv_mix(z_r, shift_prev, state, p):
    B, T, _ = z_r.shape
    f32 = jnp.float32
    z_prev = jnp.concatenate([shift_prev[:, None, :].astype(z_r.dtype), z_r[:, :-1]], axis=1)
    zs = z_r + (z_prev - z_r) * p['rw_mu']
    r = zs[..., R_OFF_R:R_OFF_K]
    k = zs[..., R_OFF_K:R_OFF_V]
    v = zs[..., R_OFF_V:R_OFF_W]
    zw = zs[..., R_OFF_W:R_OFF_A]
    za = zs[..., R_OFF_A:R_OFF_G]
    zg = zs[..., R_OFF_G:R_IN]
    w_log = -jax.nn.softplus(-(p['rw_w0'] + jnp.tanh(zw) @ p['rw_w2']).astype(f32)) - 0.5
    decay = jnp.exp(-jnp.exp(w_log))
    a = jax.nn.sigmoid((p['rw_a0'] + za @ p['rw_a2']).astype(f32))
    g = (jax.nn.sigmoid(zg) @ p['rw_g2']).astype(f32)
    heads = lambda t: t.astype(f32).reshape(B, T, R_HEADS, R_DH)
    r_h, v_h, a_h, w_h = heads(r), heads(v), heads(a), heads(decay)
    kk = heads(k * p['rw_kk'])
    kk = kk / jnp.maximum(jnp.sqrt(jnp.sum(kk * kk, axis=-1, keepdims=True)), 1e-12)
    k_h = heads(k) * (1.0 + (a_h - 1.0) * p['rw_ka'].astype(f32).reshape(R_HEADS, R_DH))
    state_new, y = wkv_scan(state.astype(f32), r_h, w_h, k_h, v_h, kk, a_h)
    mean = jnp.mean(y, axis=-1, keepdims=True)
    var = jnp.mean(jnp.square(y - mean), axis=-1, keepdims=True)
    y = ((y - mean) * lax.rsqrt(var + GN_EPS) * p['rw_ln_w'].astype(f32).reshape(R_HEADS, R_DH)
         + p['rw_ln_b'].astype(f32).reshape(R_HEADS, R_DH))
    y = y + jnp.sum(r_h * k_h * p['rw_rk'].astype(f32), axis=-1, keepdims=True) * v_h
    y = y.reshape(B, T, R_WIDTH) * g
    return y.astype(z_r.dtype), state_new


def moe_dispatch(x, expert, weight, w_gate, w_up, w_down):
    n, D = x.shape
    m = n * TOP_K
    flat_e = expert.reshape(-1)
    flat_w = weight.reshape(-1)
    flat_t = jnp.repeat(jnp.arange(n, dtype=jnp.int32), TOP_K)
    order = jnp.argsort(flat_e)
    e_s, t_s, w_s = flat_e[order], flat_t[order], flat_w[order]
    counts = jnp.zeros((N_EXPERTS,), jnp.int32).at[flat_e].add(1)
    starts = jnp.cumsum(counts) - counts
    padded = ((counts + MOE_BLOCK - 1) // MOE_BLOCK) * MOE_BLOCK
    pad_end = jnp.cumsum(padded)
    pad_start = pad_end - padded
    dest = pad_start[e_s] + (jnp.arange(m, dtype=jnp.int32) - starts[e_s])
    n_blocks = -(-m // MOE_BLOCK) + N_EXPERTS
    P = n_blocks * MOE_BLOCK
    buf_t = jnp.full((P,), n, jnp.int32).at[dest].set(t_s)
    buf_w = jnp.zeros((P,), jnp.float32).at[dest].set(w_s)
    blk_e = jnp.minimum(jnp.searchsorted(pad_end, jnp.arange(n_blocks) * MOE_BLOCK, side='right'), N_EXPERTS - 1)
    x_pad = jnp.concatenate([x, jnp.zeros((1, D), x.dtype)], axis=0)
    xb = x_pad[buf_t].reshape(n_blocks, MOE_BLOCK, D)

    def expert_block(args):
        xblk, e = args
        hdn = jax.nn.silu(xblk @ w_gate[e]) * (xblk @ w_up[e])
        return hdn @ w_down[e]

    yb = lax.map(expert_block, (xb, blk_e)).reshape(P, D)
    out = jax.ops.segment_sum(yb.astype(jnp.float32) * buf_w[:, None], buf_t, num_segments=n + 1)[:n]
    return out.astype(x.dtype)


def hier_moe(h, p):
    B, T, D = h.shape
    x = h.reshape(B * T, D)
    n = x.shape[0]
    lg = (x @ p['w_rg'] + p['b_rg']).astype(jnp.float32)
    pg = jax.nn.softmax(lg, axis=-1)
    g_idx = jnp.argmax(lg, axis=-1).astype(jnp.int32)
    pg_sel = jnp.take_along_axis(pg, g_idx[:, None], axis=-1)[:, 0]
    le = (x @ p['w_re'] + p['b_re']).astype(jnp.float32).reshape(n, N_GROUPS, EXPERTS_PER_GROUP)
    le_sel = jnp.take_along_axis(le, g_idx[:, None, None], axis=1)[:, 0]
    pe = jax.nn.softmax(le_sel, axis=-1)
    top_p, top_i = lax.top_k(pe, TOP_K)
    top_p = top_p / jnp.sum(top_p, axis=-1, keepdims=True)
    weight = pg_sel[:, None] * top_p
    expert = g_idx[:, None] * EXPERTS_PER_GROUP + top_i.astype(jnp.int32)
    y = moe_dispatch(x, expert, weight, p['w_e_gate'], p['w_e_up'], p['w_e_down'])
    return y.reshape(B, T, D)


def layer(x, c, pos, p, layer_idx, attend, wkv0, shift0):
    B, T, _ = x.shape
    f32 = jnp.float32
    mod = (jax.nn.silu(c) @ p['w_ada'] + p['b_ada'])[:, None, :]
    sh1, sc1, gt1, sh2, sc2, gt2 = jnp.split(mod, 6, axis=-1)
    h = rmsnorm(x, p['g_mix']) * (1 + sc1) + sh1
    z = h @ p['w_in']
    q = z[..., COL_Q:COL_K].reshape(B, T, A_HEADS, 2, A_DH)
    k = z[..., COL_K:COL_V].reshape(B, T, A_HEADS, 2, A_DH)
    v = z[..., COL_V:COL_GA].reshape(B, T, A_HEADS, A_DV)
    q = rope(rmsnorm(q, p['q_gain']), pos).reshape(B, T, A_HEADS, 2 * A_DH)
    k = rope(rmsnorm(k, p['k_gain']), pos).reshape(B, T, A_HEADS, 2 * A_DH)
    lam_init = 0.8 - 0.6 * math.exp(-0.3 * layer_idx)
    lam = (jnp.exp(jnp.sum(p['lam_q1'].astype(f32) * p['lam_k1'].astype(f32)))
           - jnp.exp(jnp.sum(p['lam_q2'].astype(f32) * p['lam_k2'].astype(f32))) + lam_init)
    o = attend(q, k, v, lam)
    o = (rmsnorm(o, p['subln_gain']) * (1.0 - lam_init)).reshape(B, T, A_WIDTH)
    z_r = z[..., COL_RW:]
    r_out, wkv1 = rwkv_mix(z_r, shift0, wkv0, p)
    gate_a = jax.nn.sigmoid(z[..., COL_GA:COL_GR])
    gate_r = jax.nn.sigmoid(z[..., COL_GR:COL_RW])
    merged = (gate_a * (o @ p['w_br_a']) + gate_r * (r_out @ p['w_br_r'])) @ p['w_o']
    x = x + gt1 * merged
    h2 = rmsnorm(x, p['g_ffn']) * (1 + sc2) + sh2
    x = x + gt2 * hier_moe(h2, p)
    return x, k, v, wkv1, z_r[:, -1]


def setup_inputs(seed: int = 0) -> dict:
    key = jax.random.key(seed)
    ks = iter(jax.random.split(key, 64))
    f32 = jnp.float32

    def nrm(shape, scale=1.0):
        return jax.random.normal(next(ks), shape, f32) * scale

    def uni(shape):
        return jax.random.uniform(next(ks), shape, f32)

    n_pages = PAST_LEN // PAGE_SIZE
    used = DEC_BATCH * n_pages
    n_phys = used + max(1, used // 4)
    L, D = DEPTH, D_MODEL
    inp = {}
    inp['x_prompt'] = nrm((BATCH, SEQ, D))
    inp['x_sample'] = nrm((DEC_BATCH, DEC_SEQ, D))
    inp['cache_k'] = nrm((L, n_phys, PAGE_SIZE, A_HEADS, 2 * A_DH))
    inp['cache_v'] = nrm((L, n_phys, PAGE_SIZE, A_HEADS, A_DV))
    inp['state_wkv'] = nrm((L, DEC_BATCH, R_HEADS, R_DH, R_DH), 0.3)
    inp['state_shift'] = nrm((L, DEC_BATCH, R_IN))
    inp['page_table'] = jax.random.permutation(next(ks), n_phys)[:used].reshape(DEC_BATCH, n_pages).astype(jnp.int32)
    inp['c_prompt'] = nrm((BATCH, D))
    inp['c_sample'] = nrm((DEC_BATCH, D))
    inp['w_ada'] = nrm((L, D, 6 * D), 0.3 * D ** -0.5)
    inp['b_ada'] = nrm((L, 6 * D), 0.02)
    inp['g_mix'] = 1.0 + nrm((L, D), 0.02)
    inp['g_ffn'] = 1.0 + nrm((L, D), 0.02)
    inp['w_in'] = nrm((L, D, D_IN), D ** -0.5)
    inp['q_gain'] = 1.0 + nrm((L, A_DH), 0.02)
    inp['k_gain'] = 1.0 + nrm((L, A_DH), 0.02)
    inp['lam_q1'] = nrm((L, A_DH), 0.1)
    inp['lam_k1'] = nrm((L, A_DH), 0.1)
    inp['lam_q2'] = nrm((L, A_DH), 0.1)
    inp['lam_k2'] = nrm((L, A_DH), 0.1)
    inp['subln_gain'] = 1.0 + nrm((L, A_DV), 0.02)
    inp['rw_mu'] = uni((L, R_IN))
    inp['rw_w0'] = -6.0 + 5.0 * uni((L, R_WIDTH))
    inp['rw_w2'] = nrm((L, DECAY_LORA, R_WIDTH), 0.1 * DECAY_LORA ** -0.5)
    inp['rw_a0'] = nrm((L, R_WIDTH), 0.1)
    inp['rw_a2'] = nrm((L, AAA_LORA, R_WIDTH), AAA_LORA ** -0.5)
    inp['rw_g2'] = nrm((L, GATE_LORA, R_WIDTH), GATE_LORA ** -0.5)
    inp['rw_kk'] = 0.85 + nrm((L, R_WIDTH), 0.05)
    inp['rw_ka'] = 1.0 + nrm((L, R_WIDTH), 0.05)
    inp['rw_rk'] = nrm((L, R_HEADS, R_DH), 0.1)
    inp['rw_ln_w'] = 1.0 + nrm((L, R_WIDTH), 0.02)
    inp['rw_ln_b'] = nrm((L, R_WIDTH), 0.01)
    inp['w_br_a'] = nrm((L, A_WIDTH, D), A_WIDTH ** -0.5)
    inp['w_br_r'] = nrm((L, R_WIDTH, D), R_WIDTH ** -0.5)
    inp['w_o'] = nrm((L, D, D), D ** -0.5)
    inp['w_rg'] = nrm((L, D, N_GROUPS), D ** -0.5)
    inp['b_rg'] = nrm((L, N_GROUPS), 0.01)
    inp['w_re'] = nrm((L, D, N_EXPERTS), D ** -0.5)
    inp['b_re'] = nrm((L, N_EXPERTS), 0.01)
    inp['w_e_gate'] = nrm((L, N_EXPERTS, D, D_EXPERT), D ** -0.5)
    inp['w_e_up'] = nrm((L, N_EXPERTS, D, D_EXPERT), D ** -0.5)
    inp['w_e_down'] = nrm((L, N_EXPERTS, D_EXPERT, D), D_EXPERT ** -0.5)
    return inp


def reference(x_prompt, x_sample, cache_k, cache_v, state_wkv, state_shift, page_table, c_prompt, c_sample,
              w_ada, b_ada, g_mix, g_ffn, w_in, q_gain, k_gain, lam_q1, lam_k1, lam_q2, lam_k2, subln_gain,
              rw_mu, rw_w0, rw_w2, rw_a0, rw_a2, rw_g2, rw_kk, rw_ka, rw_rk, rw_ln_w, rw_ln_b,
              w_br_a, w_br_r, w_o, w_rg, b_rg, w_re, b_re, w_e_gate, w_e_up, w_e_down):
    B, S, _ = x_prompt.shape
    DB, T, _ = x_sample.shape
    past = page_table.shape[1] * PAGE_SIZE
    pos_p = jnp.arange(S)
    pos_s = past + jnp.arange(T)
    xp, xs = x_prompt, x_sample
    kp_l, vp_l, wp_l, sp_l, ks_l, vs_l, ws_l, ss_l = [], [], [], [], [], [], [], []
    for l in range(DEPTH):
        p = dict(w_ada=w_ada[l], b_ada=b_ada[l], g_mix=g_mix[l], g_ffn=g_ffn[l], w_in=w_in[l],
                 q_gain=q_gain[l], k_gain=k_gain[l], lam_q1=lam_q1[l], lam_k1=lam_k1[l],
                 lam_q2=lam_q2[l], lam_k2=lam_k2[l], subln_gain=subln_gain[l],
                 rw_mu=rw_mu[l], rw_w0=rw_w0[l], rw_w2=rw_w2[l], rw_a0=rw_a0[l], rw_a2=rw_a2[l],
                 rw_g2=rw_g2[l], rw_kk=rw_kk[l], rw_ka=rw_ka[l], rw_rk=rw_rk[l],
                 rw_ln_w=rw_ln_w[l], rw_ln_b=rw_ln_b[l], w_br_a=w_br_a[l], w_br_r=w_br_r[l], w_o=w_o[l],
                 w_rg=w_rg[l], b_rg=b_rg[l], w_re=w_re[l], b_re=b_re[l],
                 w_e_gate=w_e_gate[l], w_e_up=w_e_up[l], w_e_down=w_e_down[l])
        wkv0 = jnp.zeros((B, R_HEADS, R_DH, R_DH), jnp.float32)
        sh0 = jnp.zeros((B, R_IN), x_prompt.dtype)
        xp, kp, vp, wp, sp = layer(xp, c_prompt, pos_p, p, l, diff_attn_prompt, wkv0, sh0)
        attend_s = lambda q, k, v, lam, l=l: diff_attn_sample(q, k, v, lam, cache_k, cache_v, page_table, l)
        xs, ks_, vs_, ws_, ss_ = layer(xs, c_sample, pos_s, p, l, attend_s, state_wkv[l], state_shift[l])
        kp_l.append(kp); vp_l.append(vp); wp_l.append(wp); sp_l.append(sp)
        ks_l.append(ks_); vs_l.append(vs_); ws_l.append(ws_.astype(state_wkv.dtype)); ss_l.append(ss_.astype(state_shift.dtype))
    k_prompt = jnp.stack(kp_l)
    v_prompt = jnp.stack(vp_l)
    wkv_prompt = jnp.stack(wp_l)
    shift_prompt = jnp.stack(sp_l)
    k_sample = jnp.stack(ks_l)
    v_sample = jnp.stack(vs_l)
    wkv_sample = jnp.stack(ws_l)
    shift_sample = jnp.stack(ss_l)
    return (xp, xs, k_prompt, v_prompt, wkv_prompt, shift_prompt, k_sample, v_sample, wkv_sample, shift_sample)
```

```python
import functools
import math

import jax
import jax.numpy as jnp
from jax import lax
from jax.experimental import pallas as pl
from jax.experimental.pallas import tpu as pltpu

F32 = jnp.float32
BF16 = jnp.bfloat16
HI = lax.Precision.HIGHEST

D_MODEL = 1024
PAGE_SIZE = 128
A_DH = 64
A_DV = 2 * A_DH
A_WIDTH = D_MODEL // 2
A_HEADS = A_WIDTH // A_DV
ROPE_THETA = 10000.0
R_DH = 64
R_WIDTH = D_MODEL // 2
R_HEADS = R_WIDTH // R_DH
DECAY_LORA = 64
AAA_LORA = 64
GATE_LORA = 160
GN_EPS = 64e-5
RMS_EPS = 1e-6
R_OFF_K = R_WIDTH
R_OFF_V = 2 * R_WIDTH
R_OFF_W = 3 * R_WIDTH
R_OFF_A = R_OFF_W + DECAY_LORA
R_OFF_G = R_OFF_A + AAA_LORA
R_IN = R_OFF_G + GATE_LORA
COL_K = A_HEADS * 2 * A_DH
COL_V = 2 * COL_K
COL_GA = COL_V + A_WIDTH
COL_GR = COL_GA + D_MODEL
COL_RW = COL_GR + D_MODEL
D_IN = COL_RW + R_IN
N_GROUPS = 4
EXPERTS_PER_GROUP = 8
N_EXPERTS = N_GROUPS * EXPERTS_PER_GROUP
TOP_K = 2
D_EXPERT = 512
LAM_INIT = 0.8 - 0.6 * math.exp(-0.3 * 0)

LANES = 128
ROUTER_PAD = LANES
NEG_BIG = -1e30
VMEM_LIMIT = 56 * 1024 * 1024

WKV_CHUNK = 64
ATTN_BLOCK = 512
PAGES_PER_STEP = 8
MOE_ROWS = 256


def _cparams(sem):
    return pltpu.CompilerParams(dimension_semantics=sem, vmem_limit_bytes=VMEM_LIMIT)


def _row_tile(n, pref):
    t = min(n, pref)
    assert n % t == 0, (n, t)
    return t


def _seg_ones(width, seg, scale=1.0):
    r = lax.broadcasted_iota(jnp.int32, (width, width), 0) // seg
    c = lax.broadcasted_iota(jnp.int32, (width, width), 1) // seg
    return jnp.where(r == c, scale, 0.0).astype(F32)


def _sigmoid(x):
    return 1.0 / (1.0 + jnp.exp(-x))


def _ada_kernel(c_ref, w_ref, b_ref, o_ref):
    c = c_ref[...]
    s = c * _sigmoid(c)
    o_ref[...] = jnp.dot(s, w_ref[...], precision=HI, preferred_element_type=F32) + b_ref[...]


def _ada(c, w_ada, b_ada):
    rows = c.shape[0]
    n_out = w_ada.shape[1]
    tn = 1536
    return pl.pallas_call(
        _ada_kernel,
        grid=(n_out // tn,),
        in_specs=[pl.BlockSpec((rows, D_MODEL), lambda j: (0, 0)),
                  pl.BlockSpec((D_MODEL, tn), lambda j: (0, j)),
                  pl.BlockSpec((1, tn), lambda j: (0, j))],
        out_specs=pl.BlockSpec((rows, tn), lambda j: (0, j)),
        out_shape=jax.ShapeDtypeStruct((rows, n_out), F32),
        compiler_params=_cparams(("arbitrary",)),
        name="ada",
    )(c, w_ada, b_ada.reshape(1, n_out))


def _mod_spec(rows, tm):
    if rows == 1:
        return pl.BlockSpec((1, D_MODEL), lambda i: (0, 0))
    return pl.BlockSpec((tm, D_MODEL), lambda i: (i, 0))


def _inproj_kernel(x_ref, sc_ref, sh_ref, g_ref, w_ref, qg_ref, kg_ref, cos_ref, sin_ref,
                   q_ref, k_ref, v_ref, ga_ref, gr_ref, zr_ref):
    x = x_ref[...]
    ms = jnp.mean(x * x, axis=-1, keepdims=True)
    h = x * lax.rsqrt(ms + RMS_EPS) * g_ref[...]
    h = h * (1.0 + sc_ref[...]) + sh_ref[...]
    hb = h.astype(BF16)

    def sec(a, b):
        return jnp.dot(hb, w_ref[:, a:b], preferred_element_type=F32)

    seg_mean = _seg_ones(LANES, A_DH, 1.0 / A_DH)
    cos = cos_ref[...]
    sin = sin_ref[...]
    lane = lax.broadcasted_iota(jnp.int32, cos.shape, 1)
    first_half = (lane % A_DH) < (A_DH // 2)

    def norm_rope(z, gain):
        m = jnp.dot(z * z, seg_mean, precision=HI, preferred_element_type=F32)
        zn = z * lax.rsqrt(m + RMS_EPS) * gain
        swapped = jnp.where(first_half, pltpu.roll(zn, LANES - A_DH // 2, 1),
                            pltpu.roll(zn, A_DH // 2, 1))
        return zn * cos + swapped * sin

    zq = sec(0, COL_K)
    zk = sec(COL_K, COL_V)
    for hd in range(A_HEADS):
        sl = slice(hd * LANES, (hd + 1) * LANES)
        q_ref[:, sl] = norm_rope(zq[:, sl], qg_ref[...])
        k_ref[:, sl] = norm_rope(zk[:, sl], kg_ref[...])
    v_ref[...] = sec(COL_V, COL_GA)
    ga_ref[...] = _sigmoid(sec(COL_GA, COL_GR))
    gr_ref[...] = _sigmoid(sec(COL_GR, COL_RW))
    zr_ref[...] = sec(COL_RW, D_IN)


def _in_proj(x, sc, sh, g_mix, w_in_bf, q_gain, k_gain, cos, sin):
    n = x.shape[0]
    tm = _row_tile(n, 256)
    row = lambda w: pl.BlockSpec((tm, w), lambda i: (i, 0))
    const = lambda r, w: pl.BlockSpec((r, w), lambda i: (0, 0))
    gain2 = lambda g: jnp.tile(g.reshape(1, A_DH), (1, 2))
    out_w = (A_WIDTH, A_WIDTH, A_WIDTH, D_MODEL, D_MODEL, R_IN)
    return pl.pallas_call(
        _inproj_kernel,
        grid=(n // tm,),
        in_specs=[row(D_MODEL), _mod_spec(sc.shape[0], tm), _mod_spec(sh.shape[0], tm),
                  const(1, D_MODEL), const(D_MODEL, D_IN), const(1, LANES), const(1, LANES),
                  row(LANES), row(LANES)],
        out_specs=[row(w) for w in out_w],
        out_shape=[jax.ShapeDtypeStruct((n, w), F32) for w in out_w],
        compiler_params=_cparams(("arbitrary",)),
        name="in_proj",
    )(x, sc, sh, g_mix.reshape(1, D_MODEL), w_in_bf, gain2(q_gain), gain2(k_gain), cos, sin)


def _rope_tables(pos):
    half = A_DH // 2
    inv = ROPE_THETA ** (-jnp.arange(half, dtype=F32) / half)
    ang = pos.astype(F32)[:, None] * inv[None, :]
    cos, sin = jnp.cos(ang), jnp.sin(ang)
    return jnp.tile(cos, (1, 4)), jnp.tile(jnp.concatenate([-sin, sin], axis=1), (1, 2))


def _lambda(lq1, lk1, lq2, lk2):
    s1 = jnp.sum(lq1 * lk1, axis=-1, keepdims=True)
    s2 = jnp.sum(lq2 * lk2, axis=-1, keepdims=True)
    return jnp.exp(s1) - jnp.exp(s2) + LAM_INIT


def _subln(o, gain):
    ms = jnp.mean(o * o, axis=-1, keepdims=True)
    return o * lax.rsqrt(ms + RMS_EPS) * gain * (1.0 - LAM_INIT)


def _attn_prompt_kernel(qi_ref, kj_ref, q_ref, k_ref, v_ref, lq1_ref, lk1_ref, lq2_ref, lk2_ref,
                        gain_ref, o_ref, qs_scr, m_scr, l_scr, acc_scr):
    p = pl.program_id(1)
    i = qi_ref[p]
    j = kj_ref[p]
    bq = q_ref.shape[0]
    bk = k_ref.shape[0]

    @pl.when(j == 0)
    def _():
        q = q_ref[...] * (A_DH ** -0.5)
        lane = lax.broadcasted_iota(jnp.int32, q.shape, 1)
        qs_scr[0:bq, :] = jnp.where(lane < A_DH, q, 0.0).astype(BF16)
        qs_scr[bq:2 * bq, :] = jnp.where(lane >= A_DH, q, 0.0).astype(BF16)
        m_scr[...] = jnp.full(m_scr.shape, NEG_BIG, F32)
        l_scr[...] = jnp.zeros(l_scr.shape, F32)
        acc_scr[...] = jnp.zeros(acc_scr.shape, F32)

    kb = k_ref[...].astype(BF16)
    s = lax.dot_general(qs_scr[...], kb, (((1,), (1,)), ((), ())), preferred_element_type=F32)

    def update(s):
        m_prev = m_scr[...]
        m_new = jnp.maximum(m_prev, jnp.max(s, axis=-1, keepdims=True))
        alpha = jnp.exp(m_prev - m_new)
        pr = jnp.exp(s - m_new)
        l_scr[...] = alpha * l_scr[...] + jnp.sum(pr, axis=-1, keepdims=True)
        acc_scr[...] = alpha * acc_scr[...] + jnp.dot(pr.astype(BF16), v_ref[...].astype(BF16),
                                                      preferred_element_type=F32)
        m_scr[...] = m_new

    @pl.when(j < i)
    def _():
        update(s)

    @pl.when(j == i)
    def _():
        row = lax.broadcasted_iota(jnp.int32, (2 * bq, bk), 0) % bq
        col = lax.broadcasted_iota(jnp.int32, (2 * bq, bk), 1)
        update(jnp.where(col <= row, s, NEG_BIG))
        d = acc_scr[...] / l_scr[...]
        lam = _lambda(lq1_ref[...], lk1_ref[...], lq2_ref[...], lk2_ref[...])
        o = d[0:bq, :] - lam * d[bq:2 * bq, :]
        o_ref[...] = _subln(o, gain_ref[...])


def _attn_prompt(q, k, v, lam_rows, subln_gain):
    n = q.shape[0]
    blk = _row_tile(n, ATTN_BLOCK)
    nq = n // blk
    qi = jnp.asarray([i for i in range(nq) for _ in range(i + 1)], jnp.int32)
    kj = jnp.asarray([j for i in range(nq) for j in range(i + 1)], jnp.int32)
    qmap = lambda h, p, qi, kj: (qi[p], h)
    kmap = lambda h, p, qi, kj: (kj[p], h)
    const = lambda w: pl.BlockSpec((1, w), lambda h, p, qi, kj: (0, 0))
    grid_spec = pltpu.PrefetchScalarGridSpec(
        num_scalar_prefetch=2,
        grid=(A_HEADS, qi.shape[0]),
        in_specs=[pl.BlockSpec((blk, LANES), qmap), pl.BlockSpec((blk, LANES), kmap),
                  pl.BlockSpec((blk, LANES), kmap),
                  const(A_DH), const(A_DH), const(A_DH), const(A_DH), const(A_DV)],
        out_specs=pl.BlockSpec((blk, LANES), qmap),
        scratch_shapes=[pltpu.VMEM((2 * blk, LANES), BF16), pltpu.VMEM((2 * blk, 1), F32),
                        pltpu.VMEM((2 * blk, 1), F32), pltpu.VMEM((2 * blk, LANES), F32)],
    )
    return pl.pallas_call(
        _attn_prompt_kernel,
        grid_spec=grid_spec,
        out_shape=jax.ShapeDtypeStruct((n, A_WIDTH), F32),
        compiler_params=_cparams(("arbitrary", "arbitrary")),
        name="attn_prompt",
    )(qi, kj, q, k, v, *lam_rows, subln_gain.reshape(1, A_DV))


def _attn_sample_kernel(*refs):
    pps = PAGES_PER_STEP
    (pt_ref, q_ref, kn_ref, vn_ref, lq1_ref, lk1_ref, lq2_ref, lk2_ref, gain_ref) = refs[:9]
    k_refs = refs[9:9 + pps]
    v_refs = refs[9 + pps:9 + 2 * pps]
    o_ref, m_scr, l_scr, acc_scr = refs[9 + 2 * pps:]
    del pt_ref
    g = pl.program_id(1)
    rows = 2 * A_HEADS
    q = q_ref[...] * (A_DH ** -0.5)
    rid = lax.broadcasted_iota(jnp.int32, (rows, A_WIDTH), 0)
    seg = lax.broadcasted_iota(jnp.int32, (rows, A_WIDTH), 1) // A_DH
    qe = jnp.where(rid == seg, q, 0.0)

    @pl.when(g == 0)
    def _():
        m_scr[...] = jnp.sum(qe * kn_ref[...], axis=-1, keepdims=True)
        l_scr[...] = jnp.ones(l_scr.shape, F32)
        acc_scr[...] = jnp.broadcast_to(vn_ref[...], acc_scr.shape)

    qb = qe.astype(BF16)
    s = jnp.concatenate(
        [lax.dot_general(qb, kr[...].astype(BF16), (((1,), (1,)), ((), ())),
                         preferred_element_type=F32) for kr in k_refs], axis=1)
    m_prev = m_scr[...]
    m_new = jnp.maximum(m_prev, jnp.max(s, axis=-1, keepdims=True))
    alpha = jnp.exp(m_prev - m_new)
    pr = jnp.exp(s - m_new)
    l_scr[...] = alpha * l_scr[...] + jnp.sum(pr, axis=-1, keepdims=True)
    prb = pr.astype(BF16)
    pv = jnp.dot(prb[:, 0:PAGE_SIZE], v_refs[0][...].astype(BF16), preferred_element_type=F32)
    for t in range(1, pps):
        pv = pv + jnp.dot(prb[:, t * PAGE_SIZE:(t + 1) * PAGE_SIZE], v_refs[t][...].astype(BF16),
                          preferred_element_type=F32)
    acc_scr[...] = alpha * acc_scr[...] + pv
    m_scr[...] = m_new

    @pl.when(g == pl.num_programs(1) - 1)
    def _():
        d = acc_scr[...] / l_scr[...]
        lam = _lambda(lq1_ref[...], lk1_ref[...], lq2_ref[...], lk2_ref[...])
        for hd in range(A_HEADS):
            sl = slice(hd * A_DV, (hd + 1) * A_DV)
            o = d[2 * hd:2 * hd + 1, sl] - lam * d[2 * hd + 1:2 * hd + 2, sl]
            o_ref[:, sl] = _subln(o, gain_ref[...])


def _attn_sample(q, k_new, v_new, cache_k, cache_v, page_table, lam_rows, subln_gain):
    nb, n_pages = page_table.shape
    pps = PAGES_PER_STEP
    assert n_pages % pps == 0
    ck = cache_k.reshape(cache_k.shape[0], PAGE_SIZE, A_WIDTH)
    cv = cache_v.reshape(cache_v.shape[0], PAGE_SIZE, A_WIDTH)
    pt = page_table.reshape(-1)
    tok = pl.BlockSpec((None, 1, A_WIDTH), lambda b, g, pt: (b, 0, 0))
    const = lambda w: pl.BlockSpec((1, w), lambda b, g, pt: (0, 0))

    def page_spec(t):
        return pl.BlockSpec((None, PAGE_SIZE, A_WIDTH),
                            lambda b, g, pt: (pt[b * n_pages + g * pps + t], 0, 0))

    pages = [page_spec(t) for t in range(pps)]
    grid_spec = pltpu.PrefetchScalarGridSpec(
        num_scalar_prefetch=1,
        grid=(nb, n_pages // pps),
        in_specs=[tok, tok, tok, const(A_DH), const(A_DH), const(A_DH), const(A_DH), const(A_DV)]
        + pages + pages,
        out_specs=tok,
        scratch_shapes=[pltpu.VMEM((2 * A_HEADS, 1), F32), pltpu.VMEM((2 * A_HEADS, 1), F32),
                        pltpu.VMEM((2 * A_HEADS, A_WIDTH), F32)],
    )
    tok3 = lambda a: a.reshape(nb, 1, A_WIDTH)
    out = pl.pallas_call(
        _attn_sample_kernel,
        grid_spec=grid_spec,
        out_shape=jax.ShapeDtypeStruct((nb, 1, A_WIDTH), F32),
        compiler_params=_cparams(("arbitrary", "arbitrary")),
        name="attn_sample",
    )(pt, tok3(q), tok3(k_new), tok3(v_new), *lam_rows, subln_gain.reshape(1, A_DV),
      *([ck] * pps), *([cv] * pps))
    return out.reshape(nb, A_WIDTH)


def _rwkv_prep_kernel(seq_mode, zr_ref, prev_ref, mu_ref, w0_ref, w2_ref, a0_ref, a2_ref, g2_ref,
                      kkp_ref, ka_ref, r_ref, k_ref, v_ref, kk_ref, a_ref, lw_ref, g_ref, *scr):
    z = zr_ref[...]
    if seq_mode:
        (carry,) = scr

        @pl.when(pl.program_id(0) == 0)
        def _():
            carry[...] = prev_ref[...]

        row = lax.broadcasted_iota(jnp.int32, z.shape, 0)
        zp = jnp.where(row == 0, carry[...], pltpu.roll(z, 1, 0))
        carry[...] = z[z.shape[0] - 1:z.shape[0], :]
    else:
        zp = prev_ref[...]
    zs = z + (zp - z) * mu_ref[...]
    r = zs[:, 0:R_OFF_K]
    k = zs[:, R_OFF_K:R_OFF_V]
    v = zs[:, R_OFF_V:R_OFF_W]
    zw = zs[:, R_OFF_W:R_OFF_A]
    za = zs[:, R_OFF_A:R_OFF_G]
    zg = zs[:, R_OFF_G:R_IN]
    w_pre = w0_ref[...] + jnp.dot(jnp.tanh(zw), w2_ref[...], precision=HI, preferred_element_type=F32)
    nx = -w_pre
    softplus = jnp.maximum(nx, 0.0) + jnp.log(1.0 + jnp.exp(-jnp.abs(nx)))
    lw_ref[...] = -jnp.exp(-softplus - 0.5)
    a = _sigmoid(a0_ref[...] + jnp.dot(za, a2_ref[...], precision=HI, preferred_element_type=F32))
    g_ref[...] = jnp.dot(_sigmoid(zg), g2_ref[...], precision=HI, preferred_element_type=F32)
    kkr = k * kkp_ref[...]
    seg_sum = _seg_ones(LANES, R_DH)
    for sb in range(R_WIDTH // LANES):
        sl = slice(sb * LANES, (sb + 1) * LANES)
        x = kkr[:, sl]
        ss = jnp.dot(x * x, seg_sum, precision=HI, preferred_element_type=F32)
        kk_ref[:, sl] = x / jnp.maximum(jnp.sqrt(ss), 1e-12)
    r_ref[...] = r
    v_ref[...] = v
    a_ref[...] = a
    k_ref[...] = k * (1.0 + (a - 1.0) * ka_ref[...])


def _rwkv_prep(zr, prev, seq_mode, p):
    n = zr.shape[0]
    tm = _row_tile(n, 256)
    row = lambda w: pl.BlockSpec((tm, w), lambda i: (i, 0))
    const = lambda r, w: pl.BlockSpec((r, w), lambda i: (0, 0))
    prev_spec = const(1, R_IN) if seq_mode else row(R_IN)
    vec = lambda a: a.reshape(1, -1)
    return pl.pallas_call(
        functools.partial(_rwkv_prep_kernel, seq_mode),
        grid=(n // tm,),
        in_specs=[row(R_IN), prev_spec, const(1, R_IN), const(1, R_WIDTH),
                  const(DECAY_LORA, R_WIDTH), const(1, R_WIDTH), const(AAA_LORA, R_WIDTH),
                  const(GATE_LORA, R_WIDTH), const(1, R_WIDTH), const(1, R_WIDTH)],
        out_specs=[row(R_WIDTH)] * 7,
        out_shape=[jax.ShapeDtypeStruct((n, R_WIDTH), F32)] * 7,
        scratch_shapes=[pltpu.VMEM((1, R_IN), F32)] if seq_mode else [],
        compiler_params=_cparams(("arbitrary",)),
        name="rwkv_prep_seq" if seq_mode else "rwkv_prep_batch",
    )(zr, prev, vec(p['rw_mu']), vec(p['rw_w0']), p['rw_w2'], vec(p['rw_a0']), p['rw_a2'],
      p['rw_g2'], vec(p['rw_kk']), vec(p['rw_ka']))


def _mm(a, b):
    return jnp.dot(a, b, precision=HI, preferred_element_type=F32)


def _mm_nt(a, b):
    return lax.dot_general(a, b, (((1,), (1,)), ((), ())), precision=HI, preferred_element_type=F32)


def _mm_tn(a, b):
    return lax.dot_general(a, b, (((0,), (0,)), ((), ())), precision=HI, preferred_element_type=F32)


def _wkv_chunk_kernel(r_ref, k_ref, v_ref, kk_ref, a_ref, lw_ref, s0_ref, y_ref, s_ref):
    c = r_ref.shape[0]

    @pl.when(pl.program_id(0) == 0)
    def _():
        s_ref[...] = s0_ref[...]

    ti = lax.broadcasted_iota(jnp.int32, (c, c), 0)
    si = lax.broadcasted_iota(jnp.int32, (c, c), 1)
    lower = si <= ti
    strict = si < ti
    lw = lw_ref[...]
    cs = _mm(jnp.where(lower, 1.0, 0.0).astype(F32), lw)
    total = cs[c - 1:c, :]
    e_pos = jnp.exp(cs)
    e_prev = jnp.exp(cs - lw)
    e_neg = jnp.exp(-cs)
    e_rem = jnp.exp(total - cs)
    e_tot = jnp.exp(total)
    kk = kk_ref[...]
    k = k_ref[...]
    b = kk * a_ref[...]
    at_all = kk * e_prev
    bt_all = b * e_neg
    kt_all = k * e_neg
    rt_all = r_ref[...] * e_pos
    bh_all = b * e_rem
    kh_all = k * e_rem
    v_all = v_ref[...]
    eye = jnp.where(si == ti, 1.0, 0.0).astype(F32)

    for h in range(R_HEADS):
        sl = slice(h * R_DH, (h + 1) * R_DH)
        at, rt, v = at_all[:, sl], rt_all[:, sl], v_all[:, sl]
        a4 = _mm_nt(jnp.concatenate([at, rt], axis=0),
                    jnp.concatenate([bt_all[:, sl], kt_all[:, sl]], axis=0))
        aab = jnp.where(strict, a4[0:c, 0:c], 0.0)
        aak = jnp.where(strict, a4[0:c, c:2 * c], 0.0)
        arb = jnp.where(lower, a4[c:2 * c, 0:c], 0.0)
        ark = jnp.where(lower, a4[c:2 * c, c:2 * c], 0.0)
        nl = -aab
        inv = eye + nl
        pw = _mm(nl, nl)
        span = 2
        while span < c:
            if 2 * span < c:
                both = _mm(jnp.concatenate([inv, pw], axis=0), pw)
                inv = inv + both[0:c, :]
                pw = both[c:2 * c, :]
            else:
                inv = inv + _mm(inv, pw)
            span *= 2
        av = _mm(jnp.concatenate([aak, ark], axis=0), v)
        tw = _mm(inv, jnp.concatenate([at, av[0:c, :]], axis=1))
        w1 = -tw[:, 0:R_DH]
        u0 = -tw[:, R_DH:2 * R_DH]
        s_old = s_ref[h]
        hs = _mm_nt(jnp.concatenate([w1, rt], axis=0), s_old)
        u = hs[0:c, :] + u0
        y_ref[:, sl] = hs[c:2 * c, :] + _mm(arb, u) + av[c:2 * c, :]
        s_ref[h] = (s_old * e_tot[:, sl] + _mm_tn(u, bh_all[:, sl]) + _mm_tn(v, kh_all[:, sl]))


def _wkv_chunk(r, k, v, kk, a, lw, s0):
    n = r.shape[0]
    c = _row_tile(n, WKV_CHUNK)
    row = pl.BlockSpec((c, R_WIDTH), lambda i: (i, 0))
    st = pl.BlockSpec((R_HEADS, R_DH, R_DH), lambda i: (0, 0, 0))
    return pl.pallas_call(
        _wkv_chunk_kernel,
        grid=(n // c,),
        in_specs=[row] * 6 + [st],
        out_specs=[row, st],
        out_shape=[jax.ShapeDtypeStruct((n, R_WIDTH), F32),
                   jax.ShapeDtypeStruct((R_HEADS, R_DH, R_DH), F32)],
        compiler_params=_cparams(("arbitrary",)),
        name="wkv_chunk",
    )(r, k, v, kk, a, lw, s0)


def _wkv_step_kernel(s_ref, r_ref, k_ref, v_ref, kk_ref, a_ref, lw_ref, y_ref, so_ref):
    s = s_ref[...]
    kk = kk_ref[...]
    sa = -jnp.sum(s * kk, axis=-1, keepdims=True)
    s2 = s * jnp.exp(lw_ref[...]) + sa * (kk * a_ref[...]) + v_ref[...] * k_ref[...]
    so_ref[...] = s2
    y_ref[...] = jnp.sum(s2 * r_ref[...], axis=-1, keepdims=True)


def _wkv_step(state, r, k, v, kk, a, lw):
    nb = state.shape[0]
    bs = _row_tile(nb, 8)
    rowv = lambda x: x.reshape(nb, R_HEADS, 1, R_DH)
    st = pl.BlockSpec((bs, R_HEADS, R_DH, R_DH), lambda i: (i, 0, 0, 0))
    rw = pl.BlockSpec((bs, R_HEADS, 1, R_DH), lambda i: (i, 0, 0, 0))
    col = pl.BlockSpec((bs, R_HEADS, R_DH, 1), lambda i: (i, 0, 0, 0))
    y, s_new = pl.pallas_call(
        _wkv_step_kernel,
        grid=(nb // bs,),
        in_specs=[st, rw, rw, col, rw, rw, rw],
        out_specs=[col, st],
        out_shape=[jax.ShapeDtypeStruct((nb, R_HEADS, R_DH, 1), F32),
                   jax.ShapeDtypeStruct(state.shape, F32)],
        compiler_params=_cparams(("arbitrary",)),
        name="wkv_step",
    )(state, rowv(r), rowv(k), v.reshape(nb, R_HEADS, R_DH, 1), rowv(kk), rowv(a), rowv(lw))
    return y.reshape(nb, R_WIDTH), s_new


def _rwkv_post_kernel(y_ref, r_ref, k_ref, v_ref, g_ref, lnw_ref, lnb_ref, rk_ref, o_ref):
    seg_mean = _seg_ones(LANES, R_DH, 1.0 / R_DH)
    seg_sum = _seg_ones(LANES, R_DH)
    for sb in range(R_WIDTH // LANES):
        sl = slice(sb * LANES, (sb + 1) * LANES)
        y = y_ref[:, sl]
        mean = jnp.dot(y, seg_mean, precision=HI, preferred_element_type=F32)
        d = y - mean
        var = jnp.dot(d * d, seg_mean, precision=HI, preferred_element_type=F32)
        yn = d * lax.rsqrt(var + GN_EPS) * lnw_ref[:, sl] + lnb_ref[:, sl]
        bonus = jnp.dot(r_ref[:, sl] * k_ref[:, sl] * rk_ref[:, sl], seg_sum, precision=HI,
                        preferred_element_type=F32)
        o_ref[:, sl] = (yn + bonus * v_ref[:, sl]) * g_ref[:, sl]


def _rwkv_post(y, r, k, v, g, p):
    n = y.shape[0]
    tm = _row_tile(n, 512)
    row = pl.BlockSpec((tm, R_WIDTH), lambda i: (i, 0))
    const = pl.BlockSpec((1, R_WIDTH), lambda i: (0, 0))
    vec = lambda a: a.reshape(1, R_WIDTH)
    return pl.pallas_call(
        _rwkv_post_kernel,
        grid=(n // tm,),
        in_specs=[row] * 5 + [const] * 3,
        out_specs=row,
        out_shape=jax.ShapeDtypeStruct((n, R_WIDTH), F32),
        compiler_params=_cparams(("arbitrary",)),
        name="rwkv_post",
    )(y, r, k, v, g, vec(p['rw_ln_w']), vec(p['rw_ln_b']), vec(p['rw_rk']))


def _merge_kernel(x_ref, o_ref, ro_ref, ga_ref, gr_ref, gt_ref, sc_ref, sh_ref, g_ref,
                  wa_ref, wr_ref, wo_ref, wrt_ref, brt_ref, x1_ref, h2_ref, lg_ref):
    ma = jnp.dot(o_ref[...].astype(BF16), wa_ref[...], preferred_element_type=F32)
    mr = jnp.dot(ro_ref[...].astype(BF16), wr_ref[...], preferred_element_type=F32)
    mg = ga_ref[...] * ma + gr_ref[...] * mr
    merged = jnp.dot(mg.astype(BF16), wo_ref[...], preferred_element_type=F32)
    x1 = x_ref[...] + gt_ref[...] * merged
    x1_ref[...] = x1
    ms = jnp.mean(x1 * x1, axis=-1, keepdims=True)
    h2 = x1 * lax.rsqrt(ms + RMS_EPS) * g_ref[...]
    h2 = h2 * (1.0 + sc_ref[...]) + sh_ref[...]
    h2_ref[...] = h2
    lg_ref[...] = jnp.dot(h2, wrt_ref[...], precision=HI, preferred_element_type=F32) + brt_ref[...]


def _merge(x, o, ro, ga, gr, gt, sc, sh, g_ffn, wa_bf, wr_bf, wo_bf, w_router, b_router):
    n = x.shape[0]
    tm = _row_tile(n, 256)
    row = lambda w: pl.BlockSpec((tm, w), lambda i: (i, 0))
    const = lambda r, w: pl.BlockSpec((r, w), lambda i: (0, 0))
    mod = lambda a: _mod_spec(a.shape[0], tm)
    return pl.pallas_call(
        _merge_kernel,
        grid=(n // tm,),
        in_specs=[row(D_MODEL), row(A_WIDTH), row(R_WIDTH), row(D_MODEL), row(D_MODEL),
                  mod(gt), mod(sc), mod(sh), const(1, D_MODEL),
                  const(A_WIDTH, D_MODEL), const(R_WIDTH, D_MODEL), const(D_MODEL, D_MODEL),
                  const(D_MODEL, ROUTER_PAD), const(1, ROUTER_PAD)],
        out_specs=[row(D_MODEL), row(D_MODEL), row(ROUTER_PAD)],
        out_shape=[jax.ShapeDtypeStruct((n, D_MODEL), F32), jax.ShapeDtypeStruct((n, D_MODEL), F32),
                   jax.ShapeDtypeStruct((n, ROUTER_PAD), F32)],
        compiler_params=_cparams(("arbitrary",)),
        name="merge",
    )(x, o, ro, ga, gr, gt, sc, sh, g_ffn.reshape(1, D_MODEL), wa_bf, wr_bf, wo_bf,
      w_router, b_router)


def _expert_kernel(be_ref, nv_ref, x_ref, bw_ref, wg_ref, wu_ref, wd_ref, y_ref):
    i = pl.program_id(0)
    del be_ref

    @pl.when(nv_ref[i] > 0)
    def _():
        xb = x_ref[...].astype(BF16)
        gate = jnp.dot(xb, wg_ref[...].astype(BF16), preferred_element_type=F32)
        up = jnp.dot(xb, wu_ref[...].astype(BF16), preferred_element_type=F32)
        hdn = gate * _sigmoid(gate) * up
        y = jnp.dot(hdn.astype(BF16), wd_ref[...].astype(BF16), preferred_element_type=F32)
        y_ref[...] = y * bw_ref[...]

    @pl.when(nv_ref[i] == 0)
    def _():
        y_ref[...] = jnp.zeros(y_ref.shape, F32)


def _experts(xb, buf_w, blk_e, blk_used, w_gate, w_up, w_down):
    rows = xb.shape[0]
    bm = MOE_ROWS
    wspec = lambda a, b: pl.BlockSpec((None, a, b), lambda i, be, nv: (be[i], 0, 0))
    grid_spec = pltpu.PrefetchScalarGridSpec(
        num_scalar_prefetch=2,
        grid=(rows // bm,),
        in_specs=[pl.BlockSpec((bm, D_MODEL), lambda i, be, nv: (i, 0)),
                  pl.BlockSpec((bm, 1), lambda i, be, nv: (i, 0)),
                  wspec(D_MODEL, D_EXPERT), wspec(D_MODEL, D_EXPERT), wspec(D_EXPERT, D_MODEL)],
        out_specs=pl.BlockSpec((bm, D_MODEL), lambda i, be, nv: (i, 0)),
    )
    return pl.pallas_call(
        _expert_kernel,
        grid_spec=grid_spec,
        out_shape=jax.ShapeDtypeStruct((rows, D_MODEL), F32),
        compiler_params=_cparams(("arbitrary",)),
        name="experts",
    )(blk_e, blk_used, xb, buf_w.reshape(rows, 1), w_gate, w_up, w_down)


def _combine_kernel(x1_ref, ya_ref, yb_ref, gt_ref, o_ref):
    o_ref[...] = x1_ref[...] + gt_ref[...] * (ya_ref[...] + yb_ref[...])


def _combine(x1, ya, yb, gt):
    n = x1.shape[0]
    tm = _row_tile(n, 512)
    row = pl.BlockSpec((tm, D_MODEL), lambda i: (i, 0))
    return pl.pallas_call(
        _combine_kernel,
        grid=(n // tm,),
        in_specs=[row, row, row, _mod_spec(gt.shape[0], tm)],
        out_specs=row,
        out_shape=jax.ShapeDtypeStruct((n, D_MODEL), F32),
        compiler_params=_cparams(("arbitrary",)),
        name="combine",
    )(x1, ya, yb, gt)


def _route(logits):
    n = logits.shape[0]
    lg = logits[:, :N_GROUPS]
    le = logits[:, N_GROUPS:N_GROUPS + N_EXPERTS].reshape(n, N_GROUPS, EXPERTS_PER_GROUP)
    pg = jax.nn.softmax(lg, axis=-1)
    g_idx = jnp.argmax(lg, axis=-1).astype(jnp.int32)
    pg_sel = jnp.take_along_axis(pg, g_idx[:, None], axis=-1)[:, 0]
    le_sel = jnp.take_along_axis(le, g_idx[:, None, None], axis=1)[:, 0]
    pe = jax.nn.softmax(le_sel, axis=-1)
    top_p, top_i = lax.top_k(pe, TOP_K)
    top_p = top_p / jnp.sum(top_p, axis=-1, keepdims=True)
    weight = pg_sel[:, None] * top_p
    expert = g_idx[:, None] * EXPERTS_PER_GROUP + top_i.astype(jnp.int32)
    return expert, weight


def _moe(h2, logits, x1, gt, w_gate, w_up, w_down):
    n = h2.shape[0]
    bm = MOE_ROWS
    expert, weight = _route(logits)
    m = n * TOP_K
    flat_e = expert.reshape(-1)
    flat_w = weight.reshape(-1)
    order = jnp.argsort(flat_e)
    e_s = flat_e[order]
    counts = jnp.zeros((N_EXPERTS,), jnp.int32).at[flat_e].add(1)
    starts = jnp.cumsum(counts) - counts
    padded = ((counts + bm - 1) // bm) * bm
    pad_end = jnp.cumsum(padded)
    pad_start = pad_end - padded
    dest = pad_start[e_s] + (jnp.arange(m, dtype=jnp.int32) - starts[e_s])
    n_blocks = -(-m // bm) + N_EXPERTS
    rows = n_blocks * bm
    slot = jnp.zeros((m,), jnp.int32).at[order].set(dest)
    buf_t = jnp.full((rows,), n, jnp.int32).at[dest].set(order // TOP_K)
    buf_w = jnp.zeros((rows,), F32).at[dest].set(flat_w[order])
    blk_start = jnp.arange(n_blocks, dtype=jnp.int32) * bm
    blk_e = jnp.minimum(jnp.searchsorted(pad_end, blk_start, side='right'),
                        N_EXPERTS - 1).astype(jnp.int32)
    blk_used = (blk_start < pad_end[-1]).astype(jnp.int32)
    x_pad = jnp.concatenate([h2, jnp.zeros((1, D_MODEL), h2.dtype)], axis=0)
    xb = x_pad[buf_t]
    yb = _experts(xb, buf_w, blk_e, blk_used, w_gate, w_up, w_down)
    slot2 = slot.reshape(n, TOP_K)
    return _combine(x1, yb[slot2[:, 0]], yb[slot2[:, 1]], gt)


def _layer(x, mod, pos, p, w, attend, rwkv):
    sh1, sc1, gt1, sh2, sc2, gt2 = [mod[:, i * D_MODEL:(i + 1) * D_MODEL] for i in range(6)]
    cos, sin = _rope_tables(pos)
    q, k, v, ga, gr, zr = _in_proj(x, sc1, sh1, p['g_mix'], w['w_in'], p['q_gain'], p['k_gain'],
                                   cos, sin)
    o = attend(q, k, v)
    ro, wkv1 = rwkv(zr)
    x1, h2, logits = _merge(x, o, ro, ga, gr, gt1, sc2, sh2, p['g_ffn'], w['w_br_a'], w['w_br_r'],
                            w['w_o'], w['w_router'], w['b_router'])
    y = _moe(h2, logits, x1, gt2, p['w_e_gate'], p['w_e_up'], p['w_e_down'])
    return y, k, v, wkv1, zr


def kernel(x_prompt, x_sample, cache_k, cache_v, state_wkv, state_shift, page_table, c_prompt, c_sample, w_ada, b_ada, g_mix, g_ffn, w_in, q_gain, k_gain, lam_q1, lam_k1, lam_q2, lam_k2, subln_gain, rw_mu, rw_w0, rw_w2, rw_a0, rw_a2, rw_g2, rw_kk, rw_ka, rw_rk, rw_ln_w, rw_ln_b, w_br_a, w_br_r, w_o, w_rg, b_rg, w_re, b_re, w_e_gate, w_e_up, w_e_down):
    assert w_ada.shape[0] == 1, "single-layer kernel"
    B, S, _ = x_prompt.shape
    DB, T, _ = x_sample.shape
    assert B == 1 and T == 1
    past = page_table.shape[1] * PAGE_SIZE
    p = dict(g_mix=g_mix[0], g_ffn=g_ffn[0], q_gain=q_gain[0], k_gain=k_gain[0],
             rw_mu=rw_mu[0], rw_w0=rw_w0[0], rw_w2=rw_w2[0], rw_a0=rw_a0[0], rw_a2=rw_a2[0],
             rw_g2=rw_g2[0], rw_kk=rw_kk[0], rw_ka=rw_ka[0], rw_rk=rw_rk[0],
             rw_ln_w=rw_ln_w[0], rw_ln_b=rw_ln_b[0],
             w_e_gate=w_e_gate[0], w_e_up=w_e_up[0], w_e_down=w_e_down[0])
    pad = ROUTER_PAD - N_GROUPS - N_EXPERTS
    w = dict(w_in=w_in[0].astype(BF16), w_br_a=w_br_a[0].astype(BF16),
             w_br_r=w_br_r[0].astype(BF16), w_o=w_o[0].astype(BF16),
             w_router=jnp.concatenate([w_rg[0], w_re[0], jnp.zeros((D_MODEL, pad), F32)], axis=1),
             b_router=jnp.concatenate([b_rg[0], b_re[0], jnp.zeros((pad,), F32)]).reshape(1, -1))
    lam_rows = [a.reshape(1, A_DH) for a in (lam_q1[0], lam_k1[0], lam_q2[0], lam_k2[0])]

    c_all = jnp.concatenate([c_prompt, jnp.zeros((7, D_MODEL), F32), c_sample], axis=0)
    mod = _ada(c_all, w_ada[0], b_ada[0])
    mod_p, mod_s = mod[0:1], mod[8:8 + DB]

    def rwkv_prompt(zr):
        r, k, v, kk, a, lw, g = _rwkv_prep(zr, jnp.zeros((1, R_IN), F32), True, p)
        y, s1 = _wkv_chunk(r, k, v, kk, a, lw, jnp.zeros((R_HEADS, R_DH, R_DH), F32))
        return _rwkv_post(y, r, k, v, g, p), s1

    def rwkv_sample(zr):
        r, k, v, kk, a, lw, g = _rwkv_prep(zr, state_shift[0], False, p)
        y, s1 = _wkv_step(state_wkv[0], r, k, v, kk, a, lw)
        return _rwkv_post(y, r, k, v, g, p), s1

    attend_p = lambda q, k, v: _attn_prompt(q, k, v, lam_rows, subln_gain[0])
    attend_s = lambda q, k, v: _attn_sample(q, k, v, cache_k[0], cache_v[0], page_table, lam_rows,
                                            subln_gain[0])

    yp, kp, vp, wp, zrp = _layer(x_prompt[0], mod_p, jnp.arange(S), p, w, attend_p, rwkv_prompt)
    ys, ks_, vs_, ws_, zrs = _layer(x_sample[:, 0], mod_s, jnp.full((DB,), past), p, w, attend_s,
                                    rwkv_sample)
    return (yp.reshape(1, S, D_MODEL), ys.reshape(DB, 1, D_MODEL),
            kp.reshape(1, 1, S, A_HEADS, 2 * A_DH), vp.reshape(1, 1, S, A_HEADS, A_DV),
            wp.reshape(1, 1, R_HEADS, R_DH, R_DH), zrp[S - 1:S].reshape(1, 1, R_IN),
            ks_.reshape(1, DB, 1, A_HEADS, 2 * A_DH), vs_.reshape(1, DB, 1, A_HEADS, A_DV),
            ws_.reshape(1, DB, R_HEADS, R_DH, R_DH), zrs.reshape(1, DB, R_IN))
```

```python
import functools
import math

import jax
import jax.numpy as jnp
from jax import lax
from jax.experimental import pallas as pl
from jax.experimental.pallas import tpu as pltpu

F32 = jnp.float32
BF16 = jnp.bfloat16
HI = lax.Precision.HIGHEST

D_MODEL = 1024
PAGE_SIZE = 128
A_DH = 64
A_DV = 2 * A_DH
A_WIDTH = D_MODEL // 2
A_HEADS = A_WIDTH // A_DV
ROPE_THETA = 10000.0
R_DH = 64
R_WIDTH = D_MODEL // 2
R_HEADS = R_WIDTH // R_DH
DECAY_LORA = 64
AAA_LORA = 64
GATE_LORA = 160
GN_EPS = 64e-5
RMS_EPS = 1e-6
R_OFF_K = R_WIDTH
R_OFF_V = 2 * R_WIDTH
R_OFF_W = 3 * R_WIDTH
R_OFF_A = R_OFF_W + DECAY_LORA
R_OFF_G = R_OFF_A + AAA_LORA
R_IN = R_OFF_G + GATE_LORA
COL_K = A_HEADS * 2 * A_DH
COL_V = 2 * COL_K
COL_GA = COL_V + A_WIDTH
COL_GR = COL_GA + D_MODEL
COL_RW = COL_GR + D_MODEL
D_IN = COL_RW + R_IN
N_GROUPS = 4
EXPERTS_PER_GROUP = 8
N_EXPERTS = N_GROUPS * EXPERTS_PER_GROUP
TOP_K = 2
D_EXPERT = 512
LAM_INIT = 0.8 - 0.6 * math.exp(-0.3 * 0)

LANES = 128
ROUTER_PAD = LANES
NEG_BIG = -1e30
VMEM_LIMIT = 56 * 1024 * 1024

WKV_CHUNK = 64
ATTN_Q_BLOCK = 1024
ATTN_K_BLOCK = 512
ATTN_ROW_CHUNK = 256

_NN = (((1,), (0,)), ((), ()))
_NT = (((1,), (1,)), ((), ()))
_TN = (((0,), (0,)), ((), ()))
PAGES_PER_STEP = 8
MOE_ROWS = 256


def _cparams(sem):
    return pltpu.CompilerParams(dimension_semantics=sem, vmem_limit_bytes=VMEM_LIMIT)


def _row_tile(n, pref):
    t = min(n, pref)
    assert n % t == 0, (n, t)
    return t


def _seg_ones(width, seg, scale=1.0):
    r = lax.broadcasted_iota(jnp.int32, (width, width), 0) // seg
    c = lax.broadcasted_iota(jnp.int32, (width, width), 1) // seg
    return jnp.where(r == c, scale, 0.0).astype(F32)


def _sigmoid(x):
    return 1.0 / (1.0 + jnp.exp(-x))


def _ada_kernel(c_ref, w_ref, b_ref, o_ref):
    c = c_ref[...]
    s = c * _sigmoid(c)
    o_ref[...] = jnp.dot(s, w_ref[...], precision=HI, preferred_element_type=F32) + b_ref[...]


def _ada(c, w_ada, b_ada):
    rows = c.shape[0]
    n_out = w_ada.shape[1]
    tn = 1536
    return pl.pallas_call(
        _ada_kernel,
        grid=(n_out // tn,),
        in_specs=[pl.BlockSpec((rows, D_MODEL), lambda j: (0, 0)),
                  pl.BlockSpec((D_MODEL, tn), lambda j: (0, j)),
                  pl.BlockSpec((1, tn), lambda j: (0, j))],
        out_specs=pl.BlockSpec((rows, tn), lambda j: (0, j)),
        out_shape=jax.ShapeDtypeStruct((rows, n_out), F32),
        compiler_params=_cparams(("arbitrary",)),
        name="ada",
    )(c, w_ada, b_ada.reshape(1, n_out))


def _mod_spec(rows, tm):
    if rows == 1:
        return pl.BlockSpec((1, D_MODEL), lambda i: (0, 0))
    return pl.BlockSpec((tm, D_MODEL), lambda i: (i, 0))


def _inproj_kernel(x_ref, sc_ref, sh_ref, g_ref, w_ref, qg_ref, kg_ref, cos_ref, sin_ref,
                   q_ref, k_ref, v_ref, ga_ref, gr_ref, zr_ref, kb_ref, vb_ref):
    x = x_ref[...]
    ms = jnp.mean(x * x, axis=-1, keepdims=True)
    h = x * lax.rsqrt(ms + RMS_EPS) * g_ref[...]
    h = h * (1.0 + sc_ref[...]) + sh_ref[...]
    hb = h.astype(BF16)

    def sec(a, b):
        return jnp.dot(hb, w_ref[:, a:b], preferred_element_type=F32)

    seg_mean = _seg_ones(LANES, A_DH, 1.0 / A_DH)
    cos = cos_ref[...]
    sin = sin_ref[...]
    lane = lax.broadcasted_iota(jnp.int32, cos.shape, 1)
    first_half = (lane % A_DH) < (A_DH // 2)

    def norm_rope(z, gain):
        m = jnp.dot(z * z, seg_mean, precision=HI, preferred_element_type=F32)
        zn = z * lax.rsqrt(m + RMS_EPS) * gain
        swapped = jnp.where(first_half, pltpu.roll(zn, LANES - A_DH // 2, 1),
                            pltpu.roll(zn, A_DH // 2, 1))
        return zn * cos + swapped * sin

    zq = sec(0, COL_K)
    zk = sec(COL_K, COL_V)
    for hd in range(A_HEADS):
        sl = slice(hd * LANES, (hd + 1) * LANES)
        q_ref[:, sl] = norm_rope(zq[:, sl], qg_ref[...])
        kh = norm_rope(zk[:, sl], kg_ref[...])
        k_ref[:, sl] = kh
        kb_ref[:, sl] = kh.astype(BF16)
    v = sec(COL_V, COL_GA)
    v_ref[...] = v
    vb_ref[...] = v.astype(BF16)
    ga_ref[...] = _sigmoid(sec(COL_GA, COL_GR))
    gr_ref[...] = _sigmoid(sec(COL_GR, COL_RW))
    zr_ref[...] = sec(COL_RW, D_IN)


def _in_proj(x, sc, sh, g_mix, w_in_bf, q_gain, k_gain, cos, sin):
    n = x.shape[0]
    tm = _row_tile(n, 256)
    row = lambda w: pl.BlockSpec((tm, w), lambda i: (i, 0))
    const = lambda r, w: pl.BlockSpec((r, w), lambda i: (0, 0))
    gain2 = lambda g: jnp.tile(g.reshape(1, A_DH), (1, 2))
    out_w = (A_WIDTH, A_WIDTH, A_WIDTH, D_MODEL, D_MODEL, R_IN)
    return pl.pallas_call(
        _inproj_kernel,
        grid=(n // tm,),
        in_specs=[row(D_MODEL), _mod_spec(sc.shape[0], tm), _mod_spec(sh.shape[0], tm),
                  const(1, D_MODEL), const(D_MODEL, D_IN), const(1, LANES), const(1, LANES),
                  row(LANES), row(LANES)],
        out_specs=[row(w) for w in out_w] + [row(A_WIDTH)] * 2,
        out_shape=[jax.ShapeDtypeStruct((n, w), F32) for w in out_w]
        + [jax.ShapeDtypeStruct((n, A_WIDTH), BF16)] * 2,
        compiler_params=_cparams(("arbitrary",)),
        name="in_proj",
    )(x, sc, sh, g_mix.reshape(1, D_MODEL), w_in_bf, gain2(q_gain), gain2(k_gain), cos, sin)


def _rope_tables(pos):
    half = A_DH // 2
    inv = ROPE_THETA ** (-jnp.arange(half, dtype=F32) / half)
    ang = pos.astype(F32)[:, None] * inv[None, :]
    cos, sin = jnp.cos(ang), jnp.sin(ang)
    return jnp.tile(cos, (1, 4)), jnp.tile(jnp.concatenate([-sin, sin], axis=1), (1, 2))


def _lambda(lq1, lk1, lq2, lk2):
    s1 = jnp.sum(lq1 * lk1, axis=-1, keepdims=True)
    s2 = jnp.sum(lq2 * lk2, axis=-1, keepdims=True)
    return jnp.exp(s1) - jnp.exp(s2) + LAM_INIT


def _subln(o, gain):
    ms = jnp.mean(o * o, axis=-1, keepdims=True)
    return o * lax.rsqrt(ms + RMS_EPS) * gain * (1.0 - LAM_INIT)


def _attn_prompt_kernel(bk, q_ref, k_ref, v_ref, lq1_ref, lk1_ref, lq2_ref, lk2_ref, gain_ref,
                        o_ref, qs_scr, m_scr, acc_scr):
    i = pl.program_id(1)
    bq = q_ref.shape[0]
    rc = min(bq, ATTN_ROW_CHUNK)
    q = q_ref[...] * (A_DH ** -0.5 * math.log2(math.e))
    lane = lax.broadcasted_iota(jnp.int32, q.shape, 1)
    qs_scr[0:bq, :] = jnp.where(lane < A_DH, q, 0.0).astype(BF16)
    qs_scr[bq:2 * bq, :] = jnp.where(lane >= A_DH, q, 0.0).astype(BF16)
    m_scr[...] = jnp.full(m_scr.shape, NEG_BIG, F32)
    acc_scr[...] = jnp.zeros(acc_scr.shape, F32)
    ones = jnp.ones((bk, LANES), BF16)

    def update(start, mask_offset):
        kb = k_ref[pl.ds(start, bk), :]
        vx = jnp.concatenate([v_ref[pl.ds(start, bk), :], ones], axis=1)
        for c in range(2 * bq // rc):
            rows = slice(c * rc, (c + 1) * rc)
            s = lax.dot_general(qs_scr[rows, :], kb, _NT, preferred_element_type=F32)
            if mask_offset is not None:
                row = lax.broadcasted_iota(jnp.int32, (rc, bk), 0) + (c * rc) % bq
                col = lax.broadcasted_iota(jnp.int32, (rc, bk), 1) + mask_offset
                s = jnp.where(col <= row, s, NEG_BIG)
            m_prev = m_scr[rows, :]
            m_new = jnp.maximum(m_prev, jnp.max(s, axis=-1, keepdims=True))
            pr = jnp.exp2((s - jnp.tile(m_new, (1, bk // LANES))).astype(BF16))
            alpha = jnp.exp2(m_prev - m_new)
            acc_scr[rows, :] = jnp.tile(alpha, (1, 2)) * acc_scr[rows, :] + jnp.dot(
                pr, vx, preferred_element_type=F32)
            m_scr[rows, :] = m_new

    def below_diagonal(j, carry):
        update(pl.multiple_of(j * bk, bk), None)
        return carry

    lax.fori_loop(0, i * (bq // bk), below_diagonal, 0)
    for jj in range(bq // bk):
        update(pl.multiple_of(i * bq + jj * bk, bk), jj * bk)
    acc = acc_scr[...]
    d = acc[:, 0:LANES] / acc[:, LANES:2 * LANES]
    lam = _lambda(lq1_ref[...], lk1_ref[...], lq2_ref[...], lk2_ref[...])
    o_ref[...] = _subln(d[0:bq, :] - lam * d[bq:2 * bq, :], gain_ref[...])


def _attn_prompt(q, kb, vb, lam_rows, subln_gain):
    n = q.shape[0]
    bq = _row_tile(n, ATTN_Q_BLOCK)
    bk = _row_tile(bq, ATTN_K_BLOCK)
    const = lambda w: pl.BlockSpec((1, w), lambda h, i: (0, 0))
    head = pl.BlockSpec((n, LANES), lambda h, i: (0, h))
    return pl.pallas_call(
        functools.partial(_attn_prompt_kernel, bk),
        grid=(A_HEADS, n // bq),
        in_specs=[pl.BlockSpec((bq, LANES), lambda h, i: (i, h)), head, head,
                  const(A_DH), const(A_DH), const(A_DH), const(A_DH), const(A_DV)],
        out_specs=pl.BlockSpec((bq, LANES), lambda h, i: (i, h)),
        out_shape=jax.ShapeDtypeStruct((n, A_WIDTH), F32),
        scratch_shapes=[pltpu.VMEM((2 * bq, LANES), BF16), pltpu.VMEM((2 * bq, LANES), F32),
                        pltpu.VMEM((2 * bq, 2 * LANES), F32)],
        compiler_params=_cparams(("arbitrary", "arbitrary")),
        name="attn_prompt",
    )(q, kb, vb, *lam_rows, subln_gain.reshape(1, A_DV))


def _attn_sample_kernel(*refs):
    pps = PAGES_PER_STEP
    (pt_ref, q_ref, kn_ref, vn_ref, lq1_ref, lk1_ref, lq2_ref, lk2_ref, gain_ref) = refs[:9]
    k_refs = refs[9:9 + pps]
    v_refs = refs[9 + pps:9 + 2 * pps]
    o_ref, m_scr, l_scr, acc_scr = refs[9 + 2 * pps:]
    del pt_ref
    g = pl.program_id(1)
    rows = 2 * A_HEADS
    page_rows = PAGE_SIZE * A_HEADS
    q = q_ref[...] * (A_DH ** -0.5)
    by_head = lambda x: jnp.concatenate(
        [jnp.broadcast_to(x[:, hd * A_DV:(hd + 1) * A_DV], (2, A_DV)) for hd in range(A_HEADS)],
        axis=0)
    rid = lax.broadcasted_iota(jnp.int32, (rows, A_DV), 0)
    comp = lax.broadcasted_iota(jnp.int32, (rows, A_DV), 1) // A_DH
    qm = jnp.where(rid % 2 == comp, by_head(q), 0.0)

    @pl.when(g == 0)
    def _():
        m_scr[...] = jnp.sum(qm * by_head(kn_ref[...]), axis=-1, keepdims=True)
        l_scr[...] = jnp.ones(l_scr.shape, F32)
        acc_scr[...] = by_head(vn_ref[...])

    qb = qm.astype(BF16)
    s = jnp.concatenate(
        [lax.dot_general(qb, kr[...].astype(BF16), (((1,), (1,)), ((), ())),
                         preferred_element_type=F32) for kr in k_refs], axis=1)
    srow = lax.broadcasted_iota(jnp.int32, s.shape, 0) // 2
    scol = lax.broadcasted_iota(jnp.int32, s.shape, 1) % A_HEADS
    s = jnp.where(srow == scol, s, NEG_BIG)
    m_prev = m_scr[...]
    m_new = jnp.maximum(m_prev, jnp.max(s, axis=-1, keepdims=True))
    alpha = jnp.exp(m_prev - m_new)
    pr = jnp.exp(s - m_new)
    l_scr[...] = alpha * l_scr[...] + jnp.sum(pr, axis=-1, keepdims=True)
    prb = pr.astype(BF16)
    pv = jnp.dot(prb[:, 0:page_rows], v_refs[0][...].astype(BF16), preferred_element_type=F32)
    for t in range(1, pps):
        pv = pv + jnp.dot(prb[:, t * page_rows:(t + 1) * page_rows], v_refs[t][...].astype(BF16),
                          preferred_element_type=F32)
    acc_scr[...] = alpha * acc_scr[...] + pv
    m_scr[...] = m_new

    @pl.when(g == pl.num_programs(1) - 1)
    def _():
        d = acc_scr[...] / l_scr[...]
        lam = _lambda(lq1_ref[...], lk1_ref[...], lq2_ref[...], lk2_ref[...])
        for hd in range(A_HEADS):
            o = d[2 * hd:2 * hd + 1, :] - lam * d[2 * hd + 1:2 * hd + 2, :]
            o_ref[:, hd * A_DV:(hd + 1) * A_DV] = _subln(o, gain_ref[...])


def _attn_sample(q, k_new, v_new, cache_k, cache_v, page_table, lam_rows, subln_gain):
    nb, n_pages = page_table.shape
    pps = PAGES_PER_STEP
    assert n_pages % pps == 0
    page_rows = PAGE_SIZE * A_HEADS
    ck = cache_k.reshape(cache_k.shape[0], page_rows, A_DV)
    cv = cache_v.reshape(cache_v.shape[0], page_rows, A_DV)
    pt = page_table.reshape(-1)
    tok = pl.BlockSpec((None, 1, A_WIDTH), lambda b, g, pt: (b, 0, 0))
    const = lambda w: pl.BlockSpec((1, w), lambda b, g, pt: (0, 0))

    def page_spec(t):
        return pl.BlockSpec((None, page_rows, A_DV),
                            lambda b, g, pt: (pt[b * n_pages + g * pps + t], 0, 0))

    pages = [page_spec(t) for t in range(pps)]
    grid_spec = pltpu.PrefetchScalarGridSpec(
        num_scalar_prefetch=1,
        grid=(nb, n_pages // pps),
        in_specs=[tok, tok, tok, const(A_DH), const(A_DH), const(A_DH), const(A_DH), const(A_DV)]
        + pages + pages,
        out_specs=tok,
        scratch_shapes=[pltpu.VMEM((2 * A_HEADS, 1), F32), pltpu.VMEM((2 * A_HEADS, 1), F32),
                        pltpu.VMEM((2 * A_HEADS, A_DV), F32)],
    )
    tok3 = lambda a: a.reshape(nb, 1, A_WIDTH)
    out = pl.pallas_call(
        _attn_sample_kernel,
        grid_spec=grid_spec,
        out_shape=jax.ShapeDtypeStruct((nb, 1, A_WIDTH), F32),
        compiler_params=_cparams(("arbitrary", "arbitrary")),
        name="attn_sample",
    )(pt, tok3(q), tok3(k_new), tok3(v_new), *lam_rows, subln_gain.reshape(1, A_DV),
      *([ck] * pps), *([cv] * pps))
    return out.reshape(nb, A_WIDTH)


def _rwkv_prep_kernel(seq_mode, zr_ref, prev_ref, mu_ref, w0_ref, w2_ref, a0_ref, a2_ref, g2_ref,
                      kkp_ref, ka_ref, r_ref, k_ref, v_ref, kk_ref, a_ref, lw_ref, g_ref, *scr):
    z = zr_ref[...]
    if seq_mode:
        (carry,) = scr

        @pl.when(pl.program_id(0) == 0)
        def _():
            carry[...] = prev_ref[...]

        row = lax.broadcasted_iota(jnp.int32, z.shape, 0)
        zp = jnp.where(row == 0, carry[...], pltpu.roll(z, 1, 0))
        carry[...] = z[z.shape[0] - 1:z.shape[0], :]
    else:
        zp = prev_ref[...]
    zs = z + (zp - z) * mu_ref[...]
    r = zs[:, 0:R_OFF_K]
    k = zs[:, R_OFF_K:R_OFF_V]
    v = zs[:, R_OFF_V:R_OFF_W]
    zw = zs[:, R_OFF_W:R_OFF_A]
    za = zs[:, R_OFF_A:R_OFF_G]
    zg = zs[:, R_OFF_G:R_IN]
    w_pre = w0_ref[...] + jnp.dot(jnp.tanh(zw), w2_ref[...], precision=HI, preferred_element_type=F32)
    nx = -w_pre
    softplus = jnp.maximum(nx, 0.0) + jnp.log(1.0 + jnp.exp(-jnp.abs(nx)))
    lw_ref[...] = -jnp.exp(-softplus - 0.5)
    a = _sigmoid(a0_ref[...] + jnp.dot(za, a2_ref[...], precision=HI, preferred_element_type=F32))
    g_ref[...] = jnp.dot(_sigmoid(zg), g2_ref[...], precision=HI, preferred_element_type=F32)
    kkr = k * kkp_ref[...]
    seg_sum = _seg_ones(LANES, R_DH)
    for sb in range(R_WIDTH // LANES):
        sl = slice(sb * LANES, (sb + 1) * LANES)
        x = kkr[:, sl]
        ss = jnp.dot(x * x, seg_sum, precision=HI, preferred_element_type=F32)
        kk_ref[:, sl] = x / jnp.maximum(jnp.sqrt(ss), 1e-12)
    r_ref[...] = r
    v_ref[...] = v
    a_ref[...] = a
    k_ref[...] = k * (1.0 + (a - 1.0) * ka_ref[...])


def _rwkv_prep(zr, prev, seq_mode, p):
    n = zr.shape[0]
    tm = _row_tile(n, 256)
    row = lambda w: pl.BlockSpec((tm, w), lambda i: (i, 0))
    const = lambda r, w: pl.BlockSpec((r, w), lambda i: (0, 0))
    prev_spec = const(1, R_IN) if seq_mode else row(R_IN)
    vec = lambda a: a.reshape(1, -1)
    return pl.pallas_call(
        functools.partial(_rwkv_prep_kernel, seq_mode),
        grid=(n // tm,),
        in_specs=[row(R_IN), prev_spec, const(1, R_IN), const(1, R_WIDTH),
                  const(DECAY_LORA, R_WIDTH), const(1, R_WIDTH), const(AAA_LORA, R_WIDTH),
                  const(GATE_LORA, R_WIDTH), const(1, R_WIDTH), const(1, R_WIDTH)],
        out_specs=[row(R_WIDTH)] * 7,
        out_shape=[jax.ShapeDtypeStruct((n, R_WIDTH), F32)] * 7,
        scratch_shapes=[pltpu.VMEM((1, R_IN), F32)] if seq_mode else [],
        compiler_params=_cparams(("arbitrary",)),
        name="rwkv_prep_seq" if seq_mode else "rwkv_prep_batch",
    )(zr, prev, vec(p['rw_mu']), vec(p['rw_w0']), p['rw_w2'], vec(p['rw_a0']), p['rw_a2'],
      p['rw_g2'], vec(p['rw_kk']), vec(p['rw_ka']))


def _split(x):
    hi = x.astype(BF16)
    return hi, (x - hi.astype(F32)).astype(BF16)


def _dot3(a, b, dims):
    ah, al = _split(a)
    bh, bl = _split(b)
    d = lambda x, y: lax.dot_general(x, y, dims, preferred_element_type=F32)
    return d(ah, bh) + (d(ah, bl) + d(al, bh))


def _mm(a, b):
    return _dot3(a, b, _NN)


def _mm_nt(a, b):
    return _dot3(a, b, _NT)


def _mm_tn(a, b):
    return _dot3(a, b, _TN)


def _wkv_chunk_kernel(r_ref, k_ref, v_ref, kk_ref, a_ref, lw_ref, s0_ref, y_ref, s_ref):
    c = r_ref.shape[0]

    @pl.when(pl.program_id(0) == 0)
    def _():
        s_ref[...] = s0_ref[...]

    ti = lax.broadcasted_iota(jnp.int32, (c, c), 0)
    si = lax.broadcasted_iota(jnp.int32, (c, c), 1)
    lower = si <= ti
    strict = si < ti
    lw = lw_ref[...]
    cs = jnp.dot(jnp.where(lower, 1.0, 0.0).astype(F32), lw, precision=HI,
                 preferred_element_type=F32)
    total = cs[c - 1:c, :]
    e_pos = jnp.exp(cs)
    e_prev = jnp.exp(cs - lw)
    e_neg = jnp.exp(-cs)
    e_rem = jnp.exp(total - cs)
    e_tot = jnp.exp(total)
    kk = kk_ref[...]
    k = k_ref[...]
    b = kk * a_ref[...]
    at_all = kk * e_prev
    bt_all = b * e_neg
    kt_all = k * e_neg
    rt_all = r_ref[...] * e_pos
    bh_all = b * e_rem
    kh_all = k * e_rem
    v_all = v_ref[...]
    eye = jnp.where(si == ti, 1.0, 0.0).astype(F32)

    heads = range(R_HEADS)
    sls = [slice(h * R_DH, (h + 1) * R_DH) for h in heads]
    at = [at_all[:, sl] for sl in sls]
    rt = [rt_all[:, sl] for sl in sls]
    v = [v_all[:, sl] for sl in sls]
    a4 = [_mm_nt(jnp.concatenate([at[h], rt[h]], axis=0),
                 jnp.concatenate([bt_all[:, sls[h]], kt_all[:, sls[h]]], axis=0)) for h in heads]
    aak = [jnp.where(strict, x[0:c, c:2 * c], 0.0) for x in a4]
    arb = [jnp.where(lower, x[c:2 * c, 0:c], 0.0) for x in a4]
    ark = [jnp.where(lower, x[c:2 * c, c:2 * c], 0.0) for x in a4]
    nl = [jnp.where(strict, -x[0:c, 0:c], 0.0) for x in a4]
    inv = [eye + x for x in nl]
    pw = [_mm(x, x) for x in nl]
    span = 2
    while span < c:
        if 2 * span < c:
            both = [_mm(jnp.concatenate([inv[h], pw[h]], axis=0), pw[h]) for h in heads]
            inv = [inv[h] + both[h][0:c, :] for h in heads]
            pw = [x[c:2 * c, :] for x in both]
        else:
            inv = [inv[h] + _mm(inv[h], pw[h]) for h in heads]
        span *= 2
    av = [_mm(jnp.concatenate([aak[h], ark[h]], axis=0), v[h]) for h in heads]
    tw = [_mm(inv[h], jnp.concatenate([at[h], av[h][0:c, :]], axis=1)) for h in heads]
    kv = [_mm_tn(v[h], kh_all[:, sls[h]]) for h in heads]
    s_old = [s_ref[h] for h in heads]
    hs = [_mm_nt(jnp.concatenate([-tw[h][:, 0:R_DH], rt[h]], axis=0), s_old[h]) for h in heads]
    u = [hs[h][0:c, :] - tw[h][:, R_DH:2 * R_DH] for h in heads]
    au = [_mm(arb[h], u[h]) for h in heads]
    ub = [_mm_tn(u[h], bh_all[:, sls[h]]) for h in heads]
    for h in heads:
        y_ref[:, sls[h]] = hs[h][c:2 * c, :] + au[h] + av[h][c:2 * c, :]
        s_ref[h] = s_old[h] * e_tot[:, sls[h]] + ub[h] + kv[h]


def _wkv_chunk(r, k, v, kk, a, lw, s0):
    n = r.shape[0]
    c = _row_tile(n, WKV_CHUNK)
    row = pl.BlockSpec((c, R_WIDTH), lambda i: (i, 0))
    st = pl.BlockSpec((R_HEADS, R_DH, R_DH), lambda i: (0, 0, 0))
    return pl.pallas_call(
        _wkv_chunk_kernel,
        grid=(n // c,),
        in_specs=[row] * 6 + [st],
        out_specs=[row, st],
        out_shape=[jax.ShapeDtypeStruct((n, R_WIDTH), F32),
                   jax.ShapeDtypeStruct((R_HEADS, R_DH, R_DH), F32)],
        compiler_params=_cparams(("arbitrary",)),
        name="wkv_chunk",
    )(r, k, v, kk, a, lw, s0)


def _wkv_step_kernel(s_ref, r_ref, k_ref, v_ref, kk_ref, a_ref, lw_ref, y_ref, so_ref):
    s = s_ref[...]
    kk = kk_ref[...]
    sa = -jnp.sum(s * kk, axis=-1, keepdims=True)
    s2 = s * jnp.exp(lw_ref[...]) + sa * (kk * a_ref[...]) + v_ref[...] * k_ref[...]
    so_ref[...] = s2
    y_ref[...] = jnp.sum(s2 * r_ref[...], axis=-1, keepdims=True)


def _wkv_step(state, r, k, v, kk, a, lw):
    nb = state.shape[0]
    bs = _row_tile(nb, 8)
    rowv = lambda x: x.reshape(nb, R_HEADS, 1, R_DH)
    st = pl.BlockSpec((bs, R_HEADS, R_DH, R_DH), lambda i: (i, 0, 0, 0))
    rw = pl.BlockSpec((bs, R_HEADS, 1, R_DH), lambda i: (i, 0, 0, 0))
    col = pl.BlockSpec((bs, R_HEADS, R_DH, 1), lambda i: (i, 0, 0, 0))
    y, s_new = pl.pallas_call(
        _wkv_step_kernel,
        grid=(nb // bs,),
        in_specs=[st, rw, rw, col, rw, rw, rw],
        out_specs=[col, st],
        out_shape=[jax.ShapeDtypeStruct((nb, R_HEADS, R_DH, 1), F32),
                   jax.ShapeDtypeStruct(state.shape, F32)],
        compiler_params=_cparams(("arbitrary",)),
        name="wkv_step",
    )(state, rowv(r), rowv(k), v.reshape(nb, R_HEADS, R_DH, 1), rowv(kk), rowv(a), rowv(lw))
    return y.reshape(nb, R_WIDTH), s_new


def _rwkv_post_kernel(y_ref, r_ref, k_ref, v_ref, g_ref, lnw_ref, lnb_ref, rk_ref, o_ref):
    seg_mean = _seg_ones(LANES, R_DH, 1.0 / R_DH)
    seg_sum = _seg_ones(LANES, R_DH)
    for sb in range(R_WIDTH // LANES):
        sl = slice(sb * LANES, (sb + 1) * LANES)
        y = y_ref[:, sl]
        mean = jnp.dot(y, seg_mean, precision=HI, preferred_element_type=F32)
        d = y - mean
        var = jnp.dot(d * d, seg_mean, precision=HI, preferred_element_type=F32)
        yn = d * lax.rsqrt(var + GN_EPS) * lnw_ref[:, sl] + lnb_ref[:, sl]
        bonus = jnp.dot(r_ref[:, sl] * k_ref[:, sl] * rk_ref[:, sl], seg_sum, precision=HI,
                        preferred_element_type=F32)
        o_ref[:, sl] = (yn + bonus * v_ref[:, sl]) * g_ref[:, sl]


def _rwkv_post(y, r, k, v, g, p):
    n = y.shape[0]
    tm = _row_tile(n, 512)
    row = pl.BlockSpec((tm, R_WIDTH), lambda i: (i, 0))
    const = pl.BlockSpec((1, R_WIDTH), lambda i: (0, 0))
    vec = lambda a: a.reshape(1, R_WIDTH)
    return pl.pallas_call(
        _rwkv_post_kernel,
        grid=(n // tm,),
        in_specs=[row] * 5 + [const] * 3,
        out_specs=row,
        out_shape=jax.ShapeDtypeStruct((n, R_WIDTH), F32),
        compiler_params=_cparams(("arbitrary",)),
        name="rwkv_post",
    )(y, r, k, v, g, vec(p['rw_ln_w']), vec(p['rw_ln_b']), vec(p['rw_rk']))


def _merge_kernel(x_ref, o_ref, ro_ref, ga_ref, gr_ref, gt_ref, sc_ref, sh_ref, g_ref,
                  wa_ref, wr_ref, wo_ref, wrt_ref, brt_ref, x1_ref, h2_ref, lg_ref):
    ma = jnp.dot(o_ref[...].astype(BF16), wa_ref[...], preferred_element_type=F32)
    mr = jnp.dot(ro_ref[...].astype(BF16), wr_ref[...], preferred_element_type=F32)
    mg = ga_ref[...] * ma + gr_ref[...] * mr
    merged = jnp.dot(mg.astype(BF16), wo_ref[...], preferred_element_type=F32)
    x1 = x_ref[...] + gt_ref[...] * merged
    x1_ref[...] = x1
    ms = jnp.mean(x1 * x1, axis=-1, keepdims=True)
    h2 = x1 * lax.rsqrt(ms + RMS_EPS) * g_ref[...]
    h2 = h2 * (1.0 + sc_ref[...]) + sh_ref[...]
    h2_ref[...] = h2
    lg_ref[...] = jnp.dot(h2, wrt_ref[...], precision=HI, preferred_element_type=F32) + brt_ref[...]


def _merge(x, o, ro, ga, gr, gt, sc, sh, g_ffn, wa_bf, wr_bf, wo_bf, w_router, b_router):
    n = x.shape[0]
    tm = _row_tile(n, 256)
    row = lambda w: pl.BlockSpec((tm, w), lambda i: (i, 0))
    const = lambda r, w: pl.BlockSpec((r, w), lambda i: (0, 0))
    mod = lambda a: _mod_spec(a.shape[0], tm)
    return pl.pallas_call(
        _merge_kernel,
        grid=(n // tm,),
        in_specs=[row(D_MODEL), row(A_WIDTH), row(R_WIDTH), row(D_MODEL), row(D_MODEL),
                  mod(gt), mod(sc), mod(sh), const(1, D_MODEL),
                  const(A_WIDTH, D_MODEL), const(R_WIDTH, D_MODEL), const(D_MODEL, D_MODEL),
                  const(D_MODEL, ROUTER_PAD), const(1, ROUTER_PAD)],
        out_specs=[row(D_MODEL), row(D_MODEL), row(ROUTER_PAD)],
        out_shape=[jax.ShapeDtypeStruct((n, D_MODEL), F32), jax.ShapeDtypeStruct((n, D_MODEL), F32),
                   jax.ShapeDtypeStruct((n, ROUTER_PAD), F32)],
        compiler_params=_cparams(("arbitrary",)),
        name="merge",
    )(x, o, ro, ga, gr, gt, sc, sh, g_ffn.reshape(1, D_MODEL), wa_bf, wr_bf, wo_bf,
      w_router, b_router)


def _expert_kernel(be_ref, nv_ref, x_ref, bw_ref, wg_ref, wu_ref, wd_ref, y_ref):
    i = pl.program_id(0)
    del be_ref

    @pl.when(nv_ref[i] > 0)
    def _():
        xb = x_ref[...].astype(BF16)
        gate = jnp.dot(xb, wg_ref[...].astype(BF16), preferred_element_type=F32)
        up = jnp.dot(xb, wu_ref[...].astype(BF16), preferred_element_type=F32)
        hdn = gate * _sigmoid(gate) * up
        y = jnp.dot(hdn.astype(BF16), wd_ref[...].astype(BF16), preferred_element_type=F32)
        y_ref[...] = y * bw_ref[...]

    @pl.when(nv_ref[i] == 0)
    def _():
        y_ref[...] = jnp.zeros(y_ref.shape, F32)


def _experts(xb, buf_w, blk_e, blk_used, w_gate, w_up, w_down):
    rows = xb.shape[0]
    bm = MOE_ROWS
    wspec = lambda a, b: pl.BlockSpec((None, a, b), lambda i, be, nv: (be[i], 0, 0))
    grid_spec = pltpu.PrefetchScalarGridSpec(
        num_scalar_prefetch=2,
        grid=(rows // bm,),
        in_specs=[pl.BlockSpec((bm, D_MODEL), lambda i, be, nv: (i, 0)),
                  pl.BlockSpec((bm, 1), lambda i, be, nv: (i, 0)),
                  wspec(D_MODEL, D_EXPERT), wspec(D_MODEL, D_EXPERT), wspec(D_EXPERT, D_MODEL)],
        out_specs=pl.BlockSpec((bm, D_MODEL), lambda i, be, nv: (i, 0)),
    )
    return pl.pallas_call(
        _expert_kernel,
        grid_spec=grid_spec,
        out_shape=jax.ShapeDtypeStruct((rows, D_MODEL), F32),
        compiler_params=_cparams(("arbitrary",)),
        name="experts",
    )(blk_e, blk_used, xb, buf_w.reshape(rows, 1), w_gate, w_up, w_down)


def _combine_kernel(x1_ref, ya_ref, yb_ref, gt_ref, o_ref):
    o_ref[...] = x1_ref[...] + gt_ref[...] * (ya_ref[...] + yb_ref[...])


def _combine(x1, ya, yb, gt):
    n = x1.shape[0]
    tm = _row_tile(n, 512)
    row = pl.BlockSpec((tm, D_MODEL), lambda i: (i, 0))
    return pl.pallas_call(
        _combine_kernel,
        grid=(n // tm,),
        in_specs=[row, row, row, _mod_spec(gt.shape[0], tm)],
        out_specs=row,
        out_shape=jax.ShapeDtypeStruct((n, D_MODEL), F32),
        compiler_params=_cparams(("arbitrary",)),
        name="combine",
    )(x1, ya, yb, gt)


def _route(logits):
    n = logits.shape[0]
    lg = logits[:, :N_GROUPS]
    le = logits[:, N_GROUPS:N_GROUPS + N_EXPERTS].reshape(n, N_GROUPS, EXPERTS_PER_GROUP)
    pg = jax.nn.softmax(lg, axis=-1)
    g_idx = jnp.argmax(lg, axis=-1).astype(jnp.int32)
    pg_sel = jnp.take_along_axis(pg, g_idx[:, None], axis=-1)[:, 0]
    le_sel = jnp.take_along_axis(le, g_idx[:, None, None], axis=1)[:, 0]
    pe = jax.nn.softmax(le_sel, axis=-1)
    top_p, top_i = lax.top_k(pe, TOP_K)
    top_p = top_p / jnp.sum(top_p, axis=-1, keepdims=True)
    weight = pg_sel[:, None] * top_p
    expert = g_idx[:, None] * EXPERTS_PER_GROUP + top_i.astype(jnp.int32)
    return expert, weight


def _moe(h2, logits, x1, gt, w_gate, w_up, w_down):
    n = h2.shape[0]
    bm = MOE_ROWS
    expert, weight = _route(logits)
    m = n * TOP_K
    flat_e = expert.reshape(-1)
    flat_w = weight.reshape(-1)
    order = jnp.argsort(flat_e)
    e_s = flat_e[order]
    counts = jnp.zeros((N_EXPERTS,), jnp.int32).at[flat_e].add(1)
    starts = jnp.cumsum(counts) - counts
    padded = ((counts + bm - 1) // bm) * bm
    pad_end = jnp.cumsum(padded)
    pad_start = pad_end - padded
    dest = pad_start[e_s] + (jnp.arange(m, dtype=jnp.int32) - starts[e_s])
    n_blocks = -(-m // bm) + N_EXPERTS
    rows = n_blocks * bm
    slot = jnp.zeros((m,), jnp.int32).at[order].set(dest)
    buf_t = jnp.full((rows,), n, jnp.int32).at[dest].set(order // TOP_K)
    buf_w = jnp.zeros((rows,), F32).at[dest].set(flat_w[order])
    blk_start = jnp.arange(n_blocks, dtype=jnp.int32) * bm
    blk_e = jnp.minimum(jnp.searchsorted(pad_end, blk_start, side='right'),
                        N_EXPERTS - 1).astype(jnp.int32)
    blk_used = (blk_start < pad_end[-1]).astype(jnp.int32)
    x_pad = jnp.concatenate([h2, jnp.zeros((1, D_MODEL), h2.dtype)], axis=0)
    xb = x_pad[buf_t]
    yb = _experts(xb, buf_w, blk_e, blk_used, w_gate, w_up, w_down)
    slot2 = slot.reshape(n, TOP_K)
    return _combine(x1, yb[slot2[:, 0]], yb[slot2[:, 1]], gt)


def _layer(x, mod, pos, p, w, attend, rwkv):
    sh1, sc1, gt1, sh2, sc2, gt2 = [mod[:, i * D_MODEL:(i + 1) * D_MODEL] for i in range(6)]
    cos, sin = _rope_tables(pos)
    q, k, v, ga, gr, zr, kb, vb = _in_proj(x, sc1, sh1, p['g_mix'], w['w_in'], p['q_gain'],
                                           p['k_gain'], cos, sin)
    o = attend(q, k, v, kb, vb)
    ro, wkv1 = rwkv(zr)
    x1, h2, logits = _merge(x, o, ro, ga, gr, gt1, sc2, sh2, p['g_ffn'], w['w_br_a'], w['w_br_r'],
                            w['w_o'], w['w_router'], w['b_router'])
    y = _moe(h2, logits, x1, gt2, p['w_e_gate'], p['w_e_up'], p['w_e_down'])
    return y, k, v, wkv1, zr


def kernel(x_prompt, x_sample, cache_k, cache_v, state_wkv, state_shift, page_table, c_prompt, c_sample, w_ada, b_ada, g_mix, g_ffn, w_in, q_gain, k_gain, lam_q1, lam_k1, lam_q2, lam_k2, subln_gain, rw_mu, rw_w0, rw_w2, rw_a0, rw_a2, rw_g2, rw_kk, rw_ka, rw_rk, rw_ln_w, rw_ln_b, w_br_a, w_br_r, w_o, w_rg, b_rg, w_re, b_re, w_e_gate, w_e_up, w_e_down):
    assert w_ada.shape[0] == 1, "single-layer kernel"
    B, S, _ = x_prompt.shape
    DB, T, _ = x_sample.shape
    assert B == 1 and T == 1
    past = page_table.shape[1] * PAGE_SIZE
    p = dict(g_mix=g_mix[0], g_ffn=g_ffn[0], q_gain=q_gain[0], k_gain=k_gain[0],
             rw_mu=rw_mu[0], rw_w0=rw_w0[0], rw_w2=rw_w2[0], rw_a0=rw_a0[0], rw_a2=rw_a2[0],
             rw_g2=rw_g2[0], rw_kk=rw_kk[0], rw_ka=rw_ka[0], rw_rk=rw_rk[0],
             rw_ln_w=rw_ln_w[0], rw_ln_b=rw_ln_b[0],
             w_e_gate=w_e_gate[0], w_e_up=w_e_up[0], w_e_down=w_e_down[0])
    pad = ROUTER_PAD - N_GROUPS - N_EXPERTS
    w = dict(w_in=w_in[0].astype(BF16), w_br_a=w_br_a[0].astype(BF16),
             w_br_r=w_br_r[0].astype(BF16), w_o=w_o[0].astype(BF16),
             w_router=jnp.concatenate([w_rg[0], w_re[0], jnp.zeros((D_MODEL, pad), F32)], axis=1),
             b_router=jnp.concatenate([b_rg[0], b_re[0], jnp.zeros((pad,), F32)]).reshape(1, -1))
    lam_rows = [a.reshape(1, A_DH) for a in (lam_q1[0], lam_k1[0], lam_q2[0], lam_k2[0])]

    c_all = jnp.concatenate([c_prompt, jnp.zeros((7, D_MODEL), F32), c_sample], axis=0)
    mod = _ada(c_all, w_ada[0], b_ada[0])
    mod_p, mod_s = mod[0:1], mod[8:8 + DB]

    def rwkv_prompt(zr):
        r, k, v, kk, a, lw, g = _rwkv_prep(zr, jnp.zeros((1, R_IN), F32), True, p)
        y, s1 = _wkv_chunk(r, k, v, kk, a, lw, jnp.zeros((R_HEADS, R_DH, R_DH), F32))
        return _rwkv_post(y, r, k, v, g, p), s1

    def rwkv_sample(zr):
        r, k, v, kk, a, lw, g = _rwkv_prep(zr, state_shift[0], False, p)
        y, s1 = _wkv_step(state_wkv[0], r, k, v, kk, a, lw)
        return _rwkv_post(y, r, k, v, g, p), s1

    attend_p = lambda q, k, v, kb, vb: _attn_prompt(q, kb, vb, lam_rows, subln_gain[0])
    attend_s = lambda q, k, v, kb, vb: _attn_sample(q, k, v, cache_k[0], cache_v[0], page_table,
                                                    lam_rows, subln_gain[0])

    yp, kp, vp, wp, zrp = _layer(x_prompt[0], mod_p, jnp.arange(S), p, w, attend_p, rwkv_prompt)
    ys, ks_, vs_, ws_, zrs = _layer(x_sample[:, 0], mod_s, jnp.full((DB,), past), p, w, attend_s,
                                    rwkv_sample)
    return (yp.reshape(1, S, D_MODEL), ys.reshape(DB, 1, D_MODEL),
            kp.reshape(1, 1, S, A_HEADS, 2 * A_DH), vp.reshape(1, 1, S, A_HEADS, A_DV),
            wp.reshape(1, 1, R_HEADS, R_DH, R_DH), zrp[S - 1:S].reshape(1, 1, R_IN),
            ks_.reshape(1, DB, 1, A_HEADS, 2 * A_DH), vs_.reshape(1, DB, 1, A_HEADS, A_DV),
            ws_.reshape(1, DB, R_HEADS, R_DH, R_DH), zrs.reshape(1, DB, R_IN))
```

```python
import functools
import math

import jax
import jax.numpy as jnp
from jax import lax
from jax.experimental import pallas as pl
from jax.experimental.pallas import tpu as pltpu

F32 = jnp.float32
BF16 = jnp.bfloat16
HI = lax.Precision.HIGHEST

D_MODEL = 1024
PAGE_SIZE = 128
A_DH = 64
A_DV = 2 * A_DH
A_WIDTH = D_MODEL // 2
A_HEADS = A_WIDTH // A_DV
ROPE_THETA = 10000.0
R_DH = 64
R_WIDTH = D_MODEL // 2
R_HEADS = R_WIDTH // R_DH
DECAY_LORA = 64
AAA_LORA = 64
GATE_LORA = 160
GN_EPS = 64e-5
RMS_EPS = 1e-6
R_OFF_K = R_WIDTH
R_OFF_V = 2 * R_WIDTH
R_OFF_W = 3 * R_WIDTH
R_OFF_A = R_OFF_W + DECAY_LORA
R_OFF_G = R_OFF_A + AAA_LORA
R_IN = R_OFF_G + GATE_LORA
COL_K = A_HEADS * 2 * A_DH
COL_V = 2 * COL_K
COL_GA = COL_V + A_WIDTH
COL_GR = COL_GA + D_MODEL
COL_RW = COL_GR + D_MODEL
D_IN = COL_RW + R_IN
N_GROUPS = 4
EXPERTS_PER_GROUP = 8
N_EXPERTS = N_GROUPS * EXPERTS_PER_GROUP
TOP_K = 2
D_EXPERT = 512
LAM_INIT = 0.8 - 0.6 * math.exp(-0.3 * 0)

LANES = 128
ROUTER_PAD = LANES
NEG_BIG = -1e30
VMEM_LIMIT = 56 * 1024 * 1024

WKV_CHUNK = 64
ATTN_Q_BLOCK = 1024
ATTN_K_BLOCK = 1024
ATTN_ROW_CHUNK = 256

_NN = (((1,), (0,)), ((), ()))
_NT = (((1,), (1,)), ((), ()))
_TN = (((0,), (0,)), ((), ()))
SEQS_PER_STEP = 4
PAGES_PER_STEP = 4
MOE_ROWS = 256


def _cparams(sem):
    return pltpu.CompilerParams(dimension_semantics=sem, vmem_limit_bytes=VMEM_LIMIT)


def _row_tile(n, pref):
    t = min(n, pref)
    assert n % t == 0, (n, t)
    return t


def _seg_ones(width, seg, scale=1.0):
    r = lax.broadcasted_iota(jnp.int32, (width, width), 0) // seg
    c = lax.broadcasted_iota(jnp.int32, (width, width), 1) // seg
    return jnp.where(r == c, scale, 0.0).astype(F32)


def _sigmoid(x):
    return 1.0 / (1.0 + jnp.exp(-x))


def _ada_kernel(c_ref, w_ref, b_ref, o_ref):
    c = c_ref[...]
    s = c * _sigmoid(c)
    o_ref[...] = jnp.dot(s, w_ref[...], precision=HI, preferred_element_type=F32) + b_ref[...]


def _ada(c, w_ada, b_ada):
    rows = c.shape[0]
    n_out = w_ada.shape[1]
    tn = 1536
    return pl.pallas_call(
        _ada_kernel,
        grid=(n_out // tn,),
        in_specs=[pl.BlockSpec((rows, D_MODEL), lambda j: (0, 0)),
                  pl.BlockSpec((D_MODEL, tn), lambda j: (0, j)),
                  pl.BlockSpec((1, tn), lambda j: (0, j))],
        out_specs=pl.BlockSpec((rows, tn), lambda j: (0, j)),
        out_shape=jax.ShapeDtypeStruct((rows, n_out), F32),
        compiler_params=_cparams(("arbitrary",)),
        name="ada",
    )(c, w_ada, b_ada.reshape(1, n_out))


def _mod_spec(rows, tm):
    if rows == 1:
        return pl.BlockSpec((1, D_MODEL), lambda i: (0, 0))
    return pl.BlockSpec((tm, D_MODEL), lambda i: (i, 0))


def _inproj_kernel(x_ref, sc_ref, sh_ref, g_ref, w_ref, qg_ref, kg_ref, cos_ref, sin_ref,
                   q_ref, k_ref, v_ref, ga_ref, gr_ref, zr_ref, kb_ref, vb_ref):
    x = x_ref[...]
    ms = jnp.mean(x * x, axis=-1, keepdims=True)
    h = x * lax.rsqrt(ms + RMS_EPS) * g_ref[...]
    h = h * (1.0 + sc_ref[...]) + sh_ref[...]
    hb = h.astype(BF16)

    def sec(a, b):
        return jnp.dot(hb, w_ref[:, a:b], preferred_element_type=F32)

    seg_mean = _seg_ones(LANES, A_DH, 1.0 / A_DH)
    cos = cos_ref[...]
    sin = sin_ref[...]
    lane = lax.broadcasted_iota(jnp.int32, cos.shape, 1)
    first_half = (lane % A_DH) < (A_DH // 2)

    def norm_rope(z, gain):
        m = jnp.dot(z * z, seg_mean, precision=HI, preferred_element_type=F32)
        zn = z * lax.rsqrt(m + RMS_EPS) * gain
        swapped = jnp.where(first_half, pltpu.roll(zn, LANES - A_DH // 2, 1),
                            pltpu.roll(zn, A_DH // 2, 1))
        return zn * cos + swapped * sin

    zq = sec(0, COL_K)
    zk = sec(COL_K, COL_V)
    for hd in range(A_HEADS):
        sl = slice(hd * LANES, (hd + 1) * LANES)
        q_ref[:, sl] = norm_rope(zq[:, sl], qg_ref[...])
        kh = norm_rope(zk[:, sl], kg_ref[...])
        k_ref[:, sl] = kh
        kb_ref[:, sl] = kh.astype(BF16)
    v = sec(COL_V, COL_GA)
    v_ref[...] = v
    vb_ref[...] = v.astype(BF16)
    ga_ref[...] = _sigmoid(sec(COL_GA, COL_GR))
    gr_ref[...] = _sigmoid(sec(COL_GR, COL_RW))
    zr_ref[...] = sec(COL_RW, D_IN)


def _in_proj(x, sc, sh, g_mix, w_in_bf, q_gain, k_gain, cos, sin):
    n = x.shape[0]
    tm = _row_tile(n, 256)
    row = lambda w: pl.BlockSpec((tm, w), lambda i: (i, 0))
    const = lambda r, w: pl.BlockSpec((r, w), lambda i: (0, 0))
    gain2 = lambda g: jnp.tile(g.reshape(1, A_DH), (1, 2))
    out_w = (A_WIDTH, A_WIDTH, A_WIDTH, D_MODEL, D_MODEL, R_IN)
    return pl.pallas_call(
        _inproj_kernel,
        grid=(n // tm,),
        in_specs=[row(D_MODEL), _mod_spec(sc.shape[0], tm), _mod_spec(sh.shape[0], tm),
                  const(1, D_MODEL), const(D_MODEL, D_IN), const(1, LANES), const(1, LANES),
                  row(LANES), row(LANES)],
        out_specs=[row(w) for w in out_w] + [row(A_WIDTH)] * 2,
        out_shape=[jax.ShapeDtypeStruct((n, w), F32) for w in out_w]
        + [jax.ShapeDtypeStruct((n, A_WIDTH), BF16)] * 2,
        compiler_params=_cparams(("arbitrary",)),
        name="in_proj",
    )(x, sc, sh, g_mix.reshape(1, D_MODEL), w_in_bf, gain2(q_gain), gain2(k_gain), cos, sin)


def _rope_tables(pos):
    half = A_DH // 2
    inv = ROPE_THETA ** (-jnp.arange(half, dtype=F32) / half)
    ang = pos.astype(F32)[:, None] * inv[None, :]
    cos, sin = jnp.cos(ang), jnp.sin(ang)
    return jnp.tile(cos, (1, 4)), jnp.tile(jnp.concatenate([-sin, sin], axis=1), (1, 2))


def _lambda(lq1, lk1, lq2, lk2):
    s1 = jnp.sum(lq1 * lk1, axis=-1, keepdims=True)
    s2 = jnp.sum(lq2 * lk2, axis=-1, keepdims=True)
    return jnp.exp(s1) - jnp.exp(s2) + LAM_INIT


def _subln(o, gain):
    ms = jnp.mean(o * o, axis=-1, keepdims=True)
    return o * lax.rsqrt(ms + RMS_EPS) * gain * (1.0 - LAM_INIT)


def _attn_prompt_kernel(bk, q_ref, k_ref, v_ref, lq1_ref, lk1_ref, lq2_ref, lk2_ref, gain_ref,
                        o_ref, qs_scr, m_scr, acc_scr):
    i = pl.program_id(1)
    bq = q_ref.shape[0]
    rc = min(bq, ATTN_ROW_CHUNK)
    q = q_ref[...] * (A_DH ** -0.5 * math.log2(math.e))
    lane = lax.broadcasted_iota(jnp.int32, q.shape, 1)
    qs_scr[0:bq, :] = jnp.where(lane < A_DH, q, 0.0).astype(BF16)
    qs_scr[bq:2 * bq, :] = jnp.where(lane >= A_DH, q, 0.0).astype(BF16)
    m_scr[...] = jnp.full(m_scr.shape, NEG_BIG, F32)
    acc_scr[...] = jnp.zeros(acc_scr.shape, F32)
    ones = jnp.ones((bk, LANES), BF16)

    def update(start, mask_offset):
        kb = k_ref[pl.ds(start, bk), :]
        vx = jnp.concatenate([v_ref[pl.ds(start, bk), :], ones], axis=1)
        for c in range(2 * bq // rc):
            rows = slice(c * rc, (c + 1) * rc)
            s = lax.dot_general(qs_scr[rows, :], kb, _NT, preferred_element_type=F32)
            if mask_offset is not None:
                row = lax.broadcasted_iota(jnp.int32, (rc, bk), 0) + (c * rc) % bq
                col = lax.broadcasted_iota(jnp.int32, (rc, bk), 1) + mask_offset
                s = jnp.where(col <= row, s, NEG_BIG)
            m_prev = m_scr[rows, :]
            m_new = jnp.maximum(m_prev, jnp.max(s, axis=-1, keepdims=True))
            pr = jnp.exp2((s - jnp.tile(m_new, (1, bk // LANES))).astype(BF16))
            alpha = jnp.exp2(m_prev - m_new)
            acc_scr[rows, :] = jnp.tile(alpha, (1, 2)) * acc_scr[rows, :] + jnp.dot(
                pr, vx, preferred_element_type=F32)
            m_scr[rows, :] = m_new

    def below_diagonal(j, carry):
        update(pl.multiple_of(j * bk, bk), None)
        return carry

    lax.fori_loop(0, i * (bq // bk), below_diagonal, 0)
    for jj in range(bq // bk):
        update(pl.multiple_of(i * bq + jj * bk, bk), jj * bk)
    acc = acc_scr[...]
    d = acc[:, 0:LANES] / acc[:, LANES:2 * LANES]
    lam = _lambda(lq1_ref[...], lk1_ref[...], lq2_ref[...], lk2_ref[...])
    o_ref[...] = _subln(d[0:bq, :] - lam * d[bq:2 * bq, :], gain_ref[...])


def _attn_prompt(q, kb, vb, lam_rows, subln_gain):
    n = q.shape[0]
    bq = _row_tile(n, ATTN_Q_BLOCK)
    bk = _row_tile(bq, ATTN_K_BLOCK)
    const = lambda w: pl.BlockSpec((1, w), lambda h, i: (0, 0))
    head = pl.BlockSpec((n, LANES), lambda h, i: (0, h))
    return pl.pallas_call(
        functools.partial(_attn_prompt_kernel, bk),
        grid=(A_HEADS, n // bq),
        in_specs=[pl.BlockSpec((bq, LANES), lambda h, i: (i, h)), head, head,
                  const(A_DH), const(A_DH), const(A_DH), const(A_DH), const(A_DV)],
        out_specs=pl.BlockSpec((bq, LANES), lambda h, i: (i, h)),
        out_shape=jax.ShapeDtypeStruct((n, A_WIDTH), F32),
        scratch_shapes=[pltpu.VMEM((2 * bq, LANES), BF16), pltpu.VMEM((2 * bq, LANES), F32),
                        pltpu.VMEM((2 * bq, 2 * LANES), F32)],
        compiler_params=_cparams(("arbitrary", "arbitrary")),
        name="attn_prompt",
    )(q, kb, vb, *lam_rows, subln_gain.reshape(1, A_DV))


def _attn_sample_kernel(*refs):
    ns, pps = SEQS_PER_STEP, PAGES_PER_STEP
    (pt_ref, q_ref, kn_ref, vn_ref, lq1_ref, lk1_ref, lq2_ref, lk2_ref, gain_ref) = refs[:9]
    k_refs = refs[9:9 + ns * pps]
    v_refs = refs[9 + ns * pps:9 + 2 * ns * pps]
    o_ref, m_scr, l_scr, acc_scr = refs[9 + 2 * ns * pps:]
    del pt_ref
    g = pl.program_id(1)
    rows = 2 * A_HEADS
    page_rows = PAGE_SIZE * A_HEADS
    by_head = lambda x: jnp.concatenate(
        [jnp.broadcast_to(x[:, hd * A_DV:(hd + 1) * A_DV], (2, A_DV)) for hd in range(A_HEADS)],
        axis=0)
    rid = lax.broadcasted_iota(jnp.int32, (rows, A_DV), 0)
    comp = lax.broadcasted_iota(jnp.int32, (rows, A_DV), 1) // A_DH
    srow = lax.broadcasted_iota(jnp.int32, (rows, pps * page_rows), 0) // 2
    scol = lax.broadcasted_iota(jnp.int32, (rows, pps * page_rows), 1) % A_HEADS
    lam = _lambda(lq1_ref[...], lk1_ref[...], lq2_ref[...], lk2_ref[...])

    qms = [jnp.where(rid % 2 == comp, by_head(q_ref[u] * (A_DH ** -0.5)), 0.0) for u in range(ns)]

    @pl.when(g == 0)
    def _():
        for u in range(ns):
            m_scr[u] = jnp.sum(qms[u] * by_head(kn_ref[u]), axis=-1, keepdims=True)
            l_scr[u] = jnp.ones((rows, 1), F32)
            acc_scr[u] = by_head(vn_ref[u])

    for u in range(ns):
        qb = qms[u].astype(BF16)
        s = jnp.concatenate(
            [lax.dot_general(qb, k_refs[u * pps + t][...].astype(BF16), _NT,
                             preferred_element_type=F32) for t in range(pps)], axis=1)
        s = jnp.where(srow == scol, s, NEG_BIG)
        m_prev = m_scr[u]
        m_new = jnp.maximum(m_prev, jnp.max(s, axis=-1, keepdims=True))
        alpha = jnp.exp(m_prev - m_new)
        pr = jnp.exp(s - m_new)
        l_scr[u] = alpha * l_scr[u] + jnp.sum(pr, axis=-1, keepdims=True)
        prb = pr.astype(BF16)
        pv = jnp.dot(prb[:, 0:page_rows], v_refs[u * pps][...].astype(BF16),
                     preferred_element_type=F32)
        for t in range(1, pps):
            pv = pv + jnp.dot(prb[:, t * page_rows:(t + 1) * page_rows],
                              v_refs[u * pps + t][...].astype(BF16), preferred_element_type=F32)
        acc_scr[u] = alpha * acc_scr[u] + pv
        m_scr[u] = m_new

    @pl.when(g == pl.num_programs(1) - 1)
    def _():
        for u in range(ns):
            d = acc_scr[u] / l_scr[u]
            for hd in range(A_HEADS):
                o = d[2 * hd:2 * hd + 1, :] - lam * d[2 * hd + 1:2 * hd + 2, :]
                o_ref[u, :, hd * A_DV:(hd + 1) * A_DV] = _subln(o, gain_ref[...])


def _attn_sample(q, k_new, v_new, cache_k, cache_v, page_table, lam_rows, subln_gain):
    nb, n_pages = page_table.shape
    ns, pps = SEQS_PER_STEP, PAGES_PER_STEP
    assert n_pages % pps == 0 and nb % ns == 0
    page_rows = PAGE_SIZE * A_HEADS
    ck = cache_k.reshape(cache_k.shape[0], page_rows, A_DV)
    cv = cache_v.reshape(cache_v.shape[0], page_rows, A_DV)
    pt = page_table.reshape(-1)
    tok = pl.BlockSpec((ns, 1, A_WIDTH), lambda b, g, pt: (b, 0, 0))
    const = lambda w: pl.BlockSpec((1, w), lambda b, g, pt: (0, 0))

    def page_spec(u, t):
        return pl.BlockSpec((None, page_rows, A_DV),
                            lambda b, g, pt: (pt[(b * ns + u) * n_pages + g * pps + t], 0, 0))

    pages = [page_spec(u, t) for u in range(ns) for t in range(pps)]
    grid_spec = pltpu.PrefetchScalarGridSpec(
        num_scalar_prefetch=1,
        grid=(nb // ns, n_pages // pps),
        in_specs=[tok, tok, tok, const(A_DH), const(A_DH), const(A_DH), const(A_DH), const(A_DV)]
        + pages + pages,
        out_specs=tok,
        scratch_shapes=[pltpu.VMEM((ns, 2 * A_HEADS, 1), F32), pltpu.VMEM((ns, 2 * A_HEADS, 1), F32),
                        pltpu.VMEM((ns, 2 * A_HEADS, A_DV), F32)],
    )
    tok3 = lambda a: a.reshape(nb, 1, A_WIDTH)
    out = pl.pallas_call(
        _attn_sample_kernel,
        grid_spec=grid_spec,
        out_shape=jax.ShapeDtypeStruct((nb, 1, A_WIDTH), F32),
        compiler_params=_cparams(("arbitrary", "arbitrary")),
        name="attn_sample",
    )(pt, tok3(q), tok3(k_new), tok3(v_new), *lam_rows, subln_gain.reshape(1, A_DV),
      *([ck] * (ns * pps)), *([cv] * (ns * pps)))
    return out.reshape(nb, A_WIDTH)


def _rwkv_prep_kernel(seq_mode, zr_ref, prev_ref, mu_ref, w0_ref, w2_ref, a0_ref, a2_ref, g2_ref,
                      kkp_ref, ka_ref, r_ref, k_ref, v_ref, kk_ref, a_ref, lw_ref, g_ref, *scr):
    z = zr_ref[...]
    if seq_mode:
        (carry,) = scr

        @pl.when(pl.program_id(0) == 0)
        def _():
            carry[...] = prev_ref[...]

        row = lax.broadcasted_iota(jnp.int32, z.shape, 0)
        zp = jnp.where(row == 0, carry[...], pltpu.roll(z, 1, 0))
        carry[...] = z[z.shape[0] - 1:z.shape[0], :]
    else:
        zp = prev_ref[...]
    zs = z + (zp - z) * mu_ref[...]
    r = zs[:, 0:R_OFF_K]
    k = zs[:, R_OFF_K:R_OFF_V]
    v = zs[:, R_OFF_V:R_OFF_W]
    zw = zs[:, R_OFF_W:R_OFF_A]
    za = zs[:, R_OFF_A:R_OFF_G]
    zg = zs[:, R_OFF_G:R_IN]
    w_pre = w0_ref[...] + jnp.dot(jnp.tanh(zw), w2_ref[...], precision=HI, preferred_element_type=F32)
    nx = -w_pre
    softplus = jnp.maximum(nx, 0.0) + jnp.log(1.0 + jnp.exp(-jnp.abs(nx)))
    lw_ref[...] = -jnp.exp(-softplus - 0.5)
    a = _sigmoid(a0_ref[...] + jnp.dot(za, a2_ref[...], precision=HI, preferred_element_type=F32))
    g_ref[...] = jnp.dot(_sigmoid(zg), g2_ref[...], precision=HI, preferred_element_type=F32)
    kkr = k * kkp_ref[...]
    seg_sum = _seg_ones(LANES, R_DH)
    for sb in range(R_WIDTH // LANES):
        sl = slice(sb * LANES, (sb + 1) * LANES)
        x = kkr[:, sl]
        ss = jnp.dot(x * x, seg_sum, precision=HI, preferred_element_type=F32)
        kk_ref[:, sl] = x / jnp.maximum(jnp.sqrt(ss), 1e-12)
    r_ref[...] = r
    v_ref[...] = v
    a_ref[...] = a
    k_ref[...] = k * (1.0 + (a - 1.0) * ka_ref[...])


def _rwkv_prep(zr, prev, seq_mode, p):
    n = zr.shape[0]
    tm = _row_tile(n, 256)
    row = lambda w: pl.BlockSpec((tm, w), lambda i: (i, 0))
    const = lambda r, w: pl.BlockSpec((r, w), lambda i: (0, 0))
    prev_spec = const(1, R_IN) if seq_mode else row(R_IN)
    vec = lambda a: a.reshape(1, -1)
    return pl.pallas_call(
        functools.partial(_rwkv_prep_kernel, seq_mode),
        grid=(n // tm,),
        in_specs=[row(R_IN), prev_spec, const(1, R_IN), const(1, R_WIDTH),
                  const(DECAY_LORA, R_WIDTH), const(1, R_WIDTH), const(AAA_LORA, R_WIDTH),
                  const(GATE_LORA, R_WIDTH), const(1, R_WIDTH), const(1, R_WIDTH)],
        out_specs=[row(R_WIDTH)] * 7,
        out_shape=[jax.ShapeDtypeStruct((n, R_WIDTH), F32)] * 7,
        scratch_shapes=[pltpu.VMEM((1, R_IN), F32)] if seq_mode else [],
        compiler_params=_cparams(("arbitrary",)),
        name="rwkv_prep_seq" if seq_mode else "rwkv_prep_batch",
    )(zr, prev, vec(p['rw_mu']), vec(p['rw_w0']), p['rw_w2'], vec(p['rw_a0']), p['rw_a2'],
      p['rw_g2'], vec(p['rw_kk']), vec(p['rw_ka']))


def _split(x):
    hi = x.astype(BF16)
    return hi, (x - hi.astype(F32)).astype(BF16)


def _dot3(a, b, dims):
    ah, al = _split(a)
    bh, bl = _split(b)
    d = lambda x, y: lax.dot_general(x, y, dims, preferred_element_type=F32)
    return d(ah, bh) + (d(ah, bl) + d(al, bh))


def _mm(a, b):
    return _dot3(a, b, _NN)


def _mm_nt(a, b):
    return _dot3(a, b, _NT)


def _mm_tn(a, b):
    return _dot3(a, b, _TN)


def _wkv_chunk_kernel(r_ref, k_ref, v_ref, kk_ref, a_ref, lw_ref, s0_ref, y_ref, s_ref):
    c = r_ref.shape[0]

    @pl.when(pl.program_id(0) == 0)
    def _():
        s_ref[...] = s0_ref[...]

    ti = lax.broadcasted_iota(jnp.int32, (c, c), 0)
    si = lax.broadcasted_iota(jnp.int32, (c, c), 1)
    lower = si <= ti
    strict = si < ti
    lw = lw_ref[...]
    cs = jnp.dot(jnp.where(lower, 1.0, 0.0).astype(F32), lw, precision=HI,
                 preferred_element_type=F32)
    total = cs[c - 1:c, :]
    e_pos = jnp.exp(cs)
    e_prev = jnp.exp(cs - lw)
    e_neg = jnp.exp(-cs)
    e_rem = jnp.exp(total - cs)
    e_tot = jnp.exp(total)
    kk = kk_ref[...]
    k = k_ref[...]
    b = kk * a_ref[...]
    at_all = kk * e_prev
    bt_all = b * e_neg
    kt_all = k * e_neg
    rt_all = r_ref[...] * e_pos
    bh_all = b * e_rem
    kh_all = k * e_rem
    v_all = v_ref[...]
    eye = jnp.where(si == ti, 1.0, 0.0).astype(F32)

    heads = range(R_HEADS)
    sls = [slice(h * R_DH, (h + 1) * R_DH) for h in heads]
    at = [at_all[:, sl] for sl in sls]
    rt = [rt_all[:, sl] for sl in sls]
    v = [v_all[:, sl] for sl in sls]
    a4 = [_mm_nt(jnp.concatenate([at[h], rt[h]], axis=0),
                 jnp.concatenate([bt_all[:, sls[h]], kt_all[:, sls[h]]], axis=0)) for h in heads]
    aak = [jnp.where(strict, x[0:c, c:2 * c], 0.0) for x in a4]
    arb = [jnp.where(lower, x[c:2 * c, 0:c], 0.0) for x in a4]
    ark = [jnp.where(lower, x[c:2 * c, c:2 * c], 0.0) for x in a4]
    nl = [jnp.where(strict, -x[0:c, 0:c], 0.0) for x in a4]
    inv = [eye + x for x in nl]
    pw = [_mm(x, x) for x in nl]
    span = 2
    while span < c:
        if 2 * span < c:
            both = [_mm(jnp.concatenate([inv[h], pw[h]], axis=0), pw[h]) for h in heads]
            inv = [inv[h] + both[h][0:c, :] for h in heads]
            pw = [x[c:2 * c, :] for x in both]
        else:
            inv = [inv[h] + _mm(inv[h], pw[h]) for h in heads]
        span *= 2
    av = [_mm(jnp.concatenate([aak[h], ark[h]], axis=0), v[h]) for h in heads]
    tw = [_mm(inv[h], jnp.concatenate([at[h], av[h][0:c, :]], axis=1)) for h in heads]
    kv = [_mm_tn(v[h], kh_all[:, sls[h]]) for h in heads]
    s_old = [s_ref[h] for h in heads]
    hs = [_mm_nt(jnp.concatenate([-tw[h][:, 0:R_DH], rt[h]], axis=0), s_old[h]) for h in heads]
    u = [hs[h][0:c, :] - tw[h][:, R_DH:2 * R_DH] for h in heads]
    au = [_mm(arb[h], u[h]) for h in heads]
    ub = [_mm_tn(u[h], bh_all[:, sls[h]]) for h in heads]
    for h in heads:
        y_ref[:, sls[h]] = hs[h][c:2 * c, :] + au[h] + av[h][c:2 * c, :]
        s_ref[h] = s_old[h] * e_tot[:, sls[h]] + ub[h] + kv[h]


def _wkv_chunk(r, k, v, kk, a, lw, s0):
    n = r.shape[0]
    c = _row_tile(n, WKV_CHUNK)
    row = pl.BlockSpec((c, R_WIDTH), lambda i: (i, 0))
    st = pl.BlockSpec((R_HEADS, R_DH, R_DH), lambda i: (0, 0, 0))
    return pl.pallas_call(
        _wkv_chunk_kernel,
        grid=(n // c,),
        in_specs=[row] * 6 + [st],
        out_specs=[row, st],
        out_shape=[jax.ShapeDtypeStruct((n, R_WIDTH), F32),
                   jax.ShapeDtypeStruct((R_HEADS, R_DH, R_DH), F32)],
        compiler_params=_cparams(("arbitrary",)),
        name="wkv_chunk",
    )(r, k, v, kk, a, lw, s0)


def _wkv_step_kernel(s_ref, r_ref, k_ref, v_ref, kk_ref, a_ref, lw_ref, y_ref, so_ref):
    s = s_ref[...]
    kk = kk_ref[...]
    sa = -jnp.sum(s * kk, axis=-1, keepdims=True)
    s2 = s * jnp.exp(lw_ref[...]) + sa * (kk * a_ref[...]) + v_ref[...] * k_ref[...]
    so_ref[...] = s2
    y_ref[...] = jnp.sum(s2 * r_ref[...], axis=-1, keepdims=True)


def _wkv_step(state, r, k, v, kk, a, lw):
    nb = state.shape[0]
    bs = _row_tile(nb, 8)
    rowv = lambda x: x.reshape(nb, R_HEADS, 1, R_DH)
    st = pl.BlockSpec((bs, R_HEADS, R_DH, R_DH), lambda i: (i, 0, 0, 0))
    rw = pl.BlockSpec((bs, R_HEADS, 1, R_DH), lambda i: (i, 0, 0, 0))
    col = pl.BlockSpec((bs, R_HEADS, R_DH, 1), lambda i: (i, 0, 0, 0))
    y, s_new = pl.pallas_call(
        _wkv_step_kernel,
        grid=(nb // bs,),
        in_specs=[st, rw, rw, col, rw, rw, rw],
        out_specs=[col, st],
        out_shape=[jax.ShapeDtypeStruct((nb, R_HEADS, R_DH, 1), F32),
                   jax.ShapeDtypeStruct(state.shape, F32)],
        compiler_params=_cparams(("arbitrary",)),
        name="wkv_step",
    )(state, rowv(r), rowv(k), v.reshape(nb, R_HEADS, R_DH, 1), rowv(kk), rowv(a), rowv(lw))
    return y.reshape(nb, R_WIDTH), s_new


def _rwkv_post_kernel(y_ref, r_ref, k_ref, v_ref, g_ref, lnw_ref, lnb_ref, rk_ref, o_ref):
    seg_mean = _seg_ones(LANES, R_DH, 1.0 / R_DH)
    seg_sum = _seg_ones(LANES, R_DH)
    for sb in range(R_WIDTH // LANES):
        sl = slice(sb * LANES, (sb + 1) * LANES)
        y = y_ref[:, sl]
        mean = jnp.dot(y, seg_mean, precision=HI, preferred_element_type=F32)
        d = y - mean
        var = jnp.dot(d * d, seg_mean, precision=HI, preferred_element_type=F32)
        yn = d * lax.rsqrt(var + GN_EPS) * lnw_ref[:, sl] + lnb_ref[:, sl]
        bonus = jnp.dot(r_ref[:, sl] * k_ref[:, sl] * rk_ref[:, sl], seg_sum, precision=HI,
                        preferred_element_type=F32)
        o_ref[:, sl] = (yn + bonus * v_ref[:, sl]) * g_ref[:, sl]


def _rwkv_post(y, r, k, v, g, p):
    n = y.shape[0]
    tm = _row_tile(n, 512)
    row = pl.BlockSpec((tm, R_WIDTH), lambda i: (i, 0))
    const = pl.BlockSpec((1, R_WIDTH), lambda i: (0, 0))
    vec = lambda a: a.reshape(1, R_WIDTH)
    return pl.pallas_call(
        _rwkv_post_kernel,
        grid=(n // tm,),
        in_specs=[row] * 5 + [const] * 3,
        out_specs=row,
        out_shape=jax.ShapeDtypeStruct((n, R_WIDTH), F32),
        compiler_params=_cparams(("arbitrary",)),
        name="rwkv_post",
    )(y, r, k, v, g, vec(p['rw_ln_w']), vec(p['rw_ln_b']), vec(p['rw_rk']))


def _merge_kernel(x_ref, o_ref, ro_ref, ga_ref, gr_ref, gt_ref, sc_ref, sh_ref, g_ref,
                  wa_ref, wr_ref, wo_ref, wrt_ref, brt_ref, x1_ref, h2_ref, rt_ref):
    ma = jnp.dot(o_ref[...].astype(BF16), wa_ref[...], preferred_element_type=F32)
    mr = jnp.dot(ro_ref[...].astype(BF16), wr_ref[...], preferred_element_type=F32)
    mg = ga_ref[...] * ma + gr_ref[...] * mr
    merged = jnp.dot(mg.astype(BF16), wo_ref[...], preferred_element_type=F32)
    x1 = x_ref[...] + gt_ref[...] * merged
    x1_ref[...] = x1
    ms = jnp.mean(x1 * x1, axis=-1, keepdims=True)
    h2 = x1 * lax.rsqrt(ms + RMS_EPS) * g_ref[...]
    h2 = h2 * (1.0 + sc_ref[...]) + sh_ref[...]
    h2_ref[...] = h2
    logits = jnp.dot(h2, wrt_ref[...], precision=HI, preferred_element_type=F32) + brt_ref[...]
    rt_ref[...] = _route(logits)


def _route(logits):
    lane = lax.broadcasted_iota(jnp.int32, logits.shape, 1)
    lane_f = lane.astype(F32)
    first_max = lambda x, m: jnp.min(jnp.where(x == m, lane_f, float(LANES)), axis=-1, keepdims=True)
    is_g = lane < N_GROUPS
    lg = jnp.where(is_g, logits, NEG_BIG)
    gmax = jnp.max(lg, axis=-1, keepdims=True)
    g_idx = first_max(lg, gmax)
    sum_g = jnp.sum(jnp.where(is_g, jnp.exp(lg - gmax), 0.0), axis=-1, keepdims=True)
    group_of_lane = ((lane - N_GROUPS) // EXPERTS_PER_GROUP).astype(F32)
    in_group = jnp.where(lane >= N_GROUPS, group_of_lane, -1.0) == g_idx
    le = jnp.where(in_group, logits, NEG_BIG)
    m1 = jnp.max(le, axis=-1, keepdims=True)
    i1 = first_max(le, m1)
    le2 = jnp.where(lane_f == i1, NEG_BIG, le)
    m2 = jnp.max(le2, axis=-1, keepdims=True)
    i2 = first_max(le2, m2)
    t = jnp.exp(m2 - m1)
    w1 = 1.0 / (sum_g * (1.0 + t))
    out = jnp.where(lane == 0, i1 - N_GROUPS, 0.0)
    out = jnp.where(lane == 1, i2 - N_GROUPS, out)
    out = jnp.where(lane == 2, w1, out)
    return jnp.where(lane == 3, w1 * t, out)


def _merge(x, o, ro, ga, gr, gt, sc, sh, g_ffn, wa_bf, wr_bf, wo_bf, w_router, b_router):
    n = x.shape[0]
    tm = _row_tile(n, 256)
    row = lambda w: pl.BlockSpec((tm, w), lambda i: (i, 0))
    const = lambda r, w: pl.BlockSpec((r, w), lambda i: (0, 0))
    mod = lambda a: _mod_spec(a.shape[0], tm)
    return pl.pallas_call(
        _merge_kernel,
        grid=(n // tm,),
        in_specs=[row(D_MODEL), row(A_WIDTH), row(R_WIDTH), row(D_MODEL), row(D_MODEL),
                  mod(gt), mod(sc), mod(sh), const(1, D_MODEL),
                  const(A_WIDTH, D_MODEL), const(R_WIDTH, D_MODEL), const(D_MODEL, D_MODEL),
                  const(D_MODEL, ROUTER_PAD), const(1, ROUTER_PAD)],
        out_specs=[row(D_MODEL), row(D_MODEL), row(ROUTER_PAD)],
        out_shape=[jax.ShapeDtypeStruct((n, D_MODEL), F32), jax.ShapeDtypeStruct((n, D_MODEL), F32),
                   jax.ShapeDtypeStruct((n, ROUTER_PAD), F32)],
        compiler_params=_cparams(("arbitrary",)),
        name="merge",
    )(x, o, ro, ga, gr, gt, sc, sh, g_ffn.reshape(1, D_MODEL), wa_bf, wr_bf, wo_bf,
      w_router, b_router)


def _rank_kernel(rt_ref, pos_ref, cnt_ref, carry):
    tm = rt_ref.shape[0]

    @pl.when(pl.program_id(0) == 0)
    def _():
        carry[...] = jnp.zeros(carry.shape, F32)

    rt = rt_ref[...]
    lane = lax.broadcasted_iota(jnp.int32, rt.shape, 1)
    lane_f = lane.astype(F32)
    oh0 = jnp.where(lane_f == rt[:, 0:1], 1.0, 0.0)
    oh1 = jnp.where(lane_f == rt[:, 1:2], 1.0, 0.0)
    ti = lax.broadcasted_iota(jnp.int32, (tm, tm), 0)
    si = lax.broadcasted_iota(jnp.int32, (tm, tm), 1)
    earlier = jnp.where(si < ti, 1.0, 0.0).astype(BF16)
    pre = jnp.dot(earlier, jnp.concatenate([oh0, oh1], axis=1).astype(BF16),
                  preferred_element_type=F32)
    c = carry[...]
    rank0 = jnp.sum(oh0 * (pre[:, 0:LANES] + c[0:1, :]), axis=-1, keepdims=True)
    rank1 = jnp.sum(oh1 * (pre[:, LANES:2 * LANES] + c[1:2, :]), axis=-1, keepdims=True)
    pos_ref[...] = jnp.where(lane == 0, rank0, jnp.where(lane == 1, rank1, 0.0))
    row = lax.broadcasted_iota(jnp.int32, c.shape, 0)
    c = c + jnp.where(row == 0, jnp.sum(oh0, axis=0, keepdims=True), 0.0) \
          + jnp.where(row == 1, jnp.sum(oh1, axis=0, keepdims=True), 0.0)
    carry[...] = c
    cnt_ref[...] = c


def _rank(route):
    n = route.shape[0]
    tm = _row_tile(n, 256)
    return pl.pallas_call(
        _rank_kernel,
        grid=(n // tm,),
        in_specs=[pl.BlockSpec((tm, LANES), lambda i: (i, 0))],
        out_specs=[pl.BlockSpec((tm, LANES), lambda i: (i, 0)),
                   pl.BlockSpec((8, LANES), lambda i: (0, 0))],
        out_shape=[jax.ShapeDtypeStruct((n, LANES), F32), jax.ShapeDtypeStruct((8, LANES), F32)],
        scratch_shapes=[pltpu.VMEM((8, LANES), F32)],
        compiler_params=_cparams(("arbitrary",)),
        name="moe_rank",
    )(route)


def _row_copy(src, src_row, dst, dst_row, sem):
    return pltpu.make_async_copy(src.at[pl.ds(src_row, 1), :], dst.at[pl.ds(dst_row, 1), :], sem)


def _dispatch_kernel(d0_ref, d1_ref, x_ref, xb_in, xb_ref, sem):
    del xb_in
    tm = x_ref.shape[0]
    base = pl.program_id(0) * tm

    def issue(t, carry):
        _row_copy(x_ref, t, xb_ref, d0_ref[base + t], sem).start()
        _row_copy(x_ref, t, xb_ref, d1_ref[base + t], sem).start()
        return carry

    lax.fori_loop(0, tm, issue, 0, unroll=8)
    for _ in range(TOP_K):
        pltpu.make_async_copy(x_ref, xb_ref.at[pl.ds(0, tm), :], sem).wait()


def _dispatch(h2, dest0, dest1, rows):
    n = h2.shape[0]
    tm = _row_tile(n, 256)
    grid_spec = pltpu.PrefetchScalarGridSpec(
        num_scalar_prefetch=2,
        grid=(n // tm,),
        in_specs=[pl.BlockSpec((tm, D_MODEL), lambda i, d0, d1: (i, 0)),
                  pl.BlockSpec(memory_space=pl.ANY)],
        out_specs=pl.BlockSpec(memory_space=pl.ANY),
        scratch_shapes=[pltpu.SemaphoreType.DMA(())],
    )
    return pl.pallas_call(
        _dispatch_kernel,
        grid_spec=grid_spec,
        out_shape=jax.ShapeDtypeStruct((rows, D_MODEL), F32),
        input_output_aliases={3: 0},
        compiler_params=_cparams(("arbitrary",)),
        name="moe_dispatch",
    )(dest0, dest1, h2, jnp.zeros((rows, D_MODEL), F32))


def _expert_kernel(be_ref, nv_ref, x_ref, wg_ref, wu_ref, wd_ref, y_ref):
    i = pl.program_id(0)
    del be_ref

    @pl.when(nv_ref[i] > 0)
    def _():
        xb = x_ref[...].astype(BF16)
        gate = jnp.dot(xb, wg_ref[...].astype(BF16), preferred_element_type=F32)
        up = jnp.dot(xb, wu_ref[...].astype(BF16), preferred_element_type=F32)
        hdn = gate * _sigmoid(gate) * up
        y_ref[...] = jnp.dot(hdn.astype(BF16), wd_ref[...].astype(BF16),
                             preferred_element_type=F32)

    @pl.when(nv_ref[i] == 0)
    def _():
        y_ref[...] = jnp.zeros(y_ref.shape, F32)


def _experts(xb, blk_e, blk_used, w_gate, w_up, w_down):
    rows = xb.shape[0]
    bm = MOE_ROWS
    wspec = lambda a, b: pl.BlockSpec((None, a, b), lambda i, be, nv: (be[i], 0, 0))
    grid_spec = pltpu.PrefetchScalarGridSpec(
        num_scalar_prefetch=2,
        grid=(rows // bm,),
        in_specs=[pl.BlockSpec((bm, D_MODEL), lambda i, be, nv: (i, 0)),
                  wspec(D_MODEL, D_EXPERT), wspec(D_MODEL, D_EXPERT), wspec(D_EXPERT, D_MODEL)],
        out_specs=pl.BlockSpec((bm, D_MODEL), lambda i, be, nv: (i, 0)),
    )
    return pl.pallas_call(
        _expert_kernel,
        grid_spec=grid_spec,
        out_shape=jax.ShapeDtypeStruct((rows, D_MODEL), F32),
        compiler_params=_cparams(("arbitrary",)),
        name="experts",
    )(blk_e, blk_used, xb, w_gate, w_up, w_down)


def _combine_kernel(d0_ref, d1_ref, x1_ref, rt_ref, gt_ref, yb_ref, o_ref, ya_scr, yb_scr, sem):
    tm = x1_ref.shape[0]
    base = pl.program_id(0) * tm

    def issue(t, carry):
        _row_copy(yb_ref, d0_ref[base + t], ya_scr, t, sem).start()
        _row_copy(yb_ref, d1_ref[base + t], yb_scr, t, sem).start()
        return carry

    lax.fori_loop(0, tm, issue, 0, unroll=8)
    pltpu.make_async_copy(yb_ref.at[pl.ds(0, tm), :], ya_scr, sem).wait()
    pltpu.make_async_copy(yb_ref.at[pl.ds(0, tm), :], yb_scr, sem).wait()
    rt = rt_ref[...]
    moe = rt[:, 2:3] * ya_scr[...] + rt[:, 3:4] * yb_scr[...]
    o_ref[...] = x1_ref[...] + gt_ref[...] * moe


def _combine(x1, route, gt, yb, dest0, dest1):
    n = x1.shape[0]
    tm = _row_tile(n, 256)
    gt_spec = (pl.BlockSpec((1, D_MODEL), lambda i, d0, d1: (0, 0)) if gt.shape[0] == 1
               else pl.BlockSpec((tm, D_MODEL), lambda i, d0, d1: (i, 0)))
    grid_spec = pltpu.PrefetchScalarGridSpec(
        num_scalar_prefetch=2,
        grid=(n // tm,),
        in_specs=[pl.BlockSpec((tm, D_MODEL), lambda i, d0, d1: (i, 0)),
                  pl.BlockSpec((tm, LANES), lambda i, d0, d1: (i, 0)),
                  gt_spec, pl.BlockSpec(memory_space=pl.ANY)],
        out_specs=pl.BlockSpec((tm, D_MODEL), lambda i, d0, d1: (i, 0)),
        scratch_shapes=[pltpu.VMEM((tm, D_MODEL), F32), pltpu.VMEM((tm, D_MODEL), F32),
                        pltpu.SemaphoreType.DMA(())],
    )
    return pl.pallas_call(
        _combine_kernel,
        grid_spec=grid_spec,
        out_shape=jax.ShapeDtypeStruct((n, D_MODEL), F32),
        compiler_params=_cparams(("arbitrary",)),
        name="moe_combine",
    )(dest0, dest1, x1, route, gt, yb)


def _moe(h2, route, x1, gt, w_gate, w_up, w_down):
    n = h2.shape[0]
    bm = MOE_ROWS
    pos, cnt = _rank(route)
    as_int = lambda x: x.astype(jnp.int32)
    e0, e1, rank0, rank1 = as_int(route[:, 0]), as_int(route[:, 1]), as_int(pos[:, 0]), as_int(pos[:, 1])
    c0, c1 = as_int(cnt[0, :N_EXPERTS]), as_int(cnt[1, :N_EXPERTS])
    padded = ((c0 + c1 + bm - 1) // bm) * bm
    pad_end = jnp.cumsum(padded)
    pad_start = pad_end - padded
    dest0 = pad_start[e0] + rank0
    dest1 = (pad_start + c0)[e1] + rank1
    n_blocks = -(-(n * TOP_K) // bm) + N_EXPERTS
    blk_start = jnp.arange(n_blocks, dtype=jnp.int32) * bm
    blk_e = jnp.minimum(jnp.searchsorted(pad_end, blk_start, side='right'),
                        N_EXPERTS - 1).astype(jnp.int32)
    blk_used = (blk_start < pad_end[-1]).astype(jnp.int32)
    xb = _dispatch(h2, dest0, dest1, n_blocks * bm)
    yb = _experts(xb, blk_e, blk_used, w_gate, w_up, w_down)
    return _combine(x1, route, gt, yb, dest0, dest1)


def _layer(x, mod, pos, p, w, attend, rwkv):
    sh1, sc1, gt1, sh2, sc2, gt2 = [mod[:, i * D_MODEL:(i + 1) * D_MODEL] for i in range(6)]
    cos, sin = _rope_tables(pos)
    q, k, v, ga, gr, zr, kb, vb = _in_proj(x, sc1, sh1, p['g_mix'], w['w_in'], p['q_gain'],
                                           p['k_gain'], cos, sin)
    o = attend(q, k, v, kb, vb)
    ro, wkv1 = rwkv(zr)
    x1, h2, logits = _merge(x, o, ro, ga, gr, gt1, sc2, sh2, p['g_ffn'], w['w_br_a'], w['w_br_r'],
                            w['w_o'], w['w_router'], w['b_router'])
    y = _moe(h2, logits, x1, gt2, p['w_e_gate'], p['w_e_up'], p['w_e_down'])
    return y, k, v, wkv1, zr


def kernel(x_prompt, x_sample, cache_k, cache_v, state_wkv, state_shift, page_table, c_prompt, c_sample, w_ada, b_ada, g_mix, g_ffn, w_in, q_gain, k_gain, lam_q1, lam_k1, lam_q2, lam_k2, subln_gain, rw_mu, rw_w0, rw_w2, rw_a0, rw_a2, rw_g2, rw_kk, rw_ka, rw_rk, rw_ln_w, rw_ln_b, w_br_a, w_br_r, w_o, w_rg, b_rg, w_re, b_re, w_e_gate, w_e_up, w_e_down):
    assert w_ada.shape[0] == 1, "single-layer kernel"
    B, S, _ = x_prompt.shape
    DB, T, _ = x_sample.shape
    assert B == 1 and T == 1
    past = page_table.shape[1] * PAGE_SIZE
    p = dict(g_mix=g_mix[0], g_ffn=g_ffn[0], q_gain=q_gain[0], k_gain=k_gain[0],
             rw_mu=rw_mu[0], rw_w0=rw_w0[0], rw_w2=rw_w2[0], rw_a0=rw_a0[0], rw_a2=rw_a2[0],
             rw_g2=rw_g2[0], rw_kk=rw_kk[0], rw_ka=rw_ka[0], rw_rk=rw_rk[0],
             rw_ln_w=rw_ln_w[0], rw_ln_b=rw_ln_b[0],
             w_e_gate=w_e_gate[0], w_e_up=w_e_up[0], w_e_down=w_e_down[0])
    pad = ROUTER_PAD - N_GROUPS - N_EXPERTS
    w = dict(w_in=w_in[0].astype(BF16), w_br_a=w_br_a[0].astype(BF16),
             w_br_r=w_br_r[0].astype(BF16), w_o=w_o[0].astype(BF16),
             w_router=jnp.concatenate([w_rg[0], w_re[0], jnp.zeros((D_MODEL, pad), F32)], axis=1),
             b_router=jnp.concatenate([b_rg[0], b_re[0], jnp.zeros((pad,), F32)]).reshape(1, -1))
    lam_rows = [a.reshape(1, A_DH) for a in (lam_q1[0], lam_k1[0], lam_q2[0], lam_k2[0])]

    c_all = jnp.concatenate([c_prompt, jnp.zeros((7, D_MODEL), F32), c_sample], axis=0)
    mod = _ada(c_all, w_ada[0], b_ada[0])
    mod_p, mod_s = mod[0:1], mod[8:8 + DB]

    def rwkv_prompt(zr):
        r, k, v, kk, a, lw, g = _rwkv_prep(zr, jnp.zeros((1, R_IN), F32), True, p)
        y, s1 = _wkv_chunk(r, k, v, kk, a, lw, jnp.zeros((R_HEADS, R_DH, R_DH), F32))
        return _rwkv_post(y, r, k, v, g, p), s1

    def rwkv_sample(zr):
        r, k, v, kk, a, lw, g = _rwkv_prep(zr, state_shift[0], False, p)
        y, s1 = _wkv_step(state_wkv[0], r, k, v, kk, a, lw)
        return _rwkv_post(y, r, k, v, g, p), s1

    attend_p = lambda q, k, v, kb, vb: _attn_prompt(q, kb, vb, lam_rows, subln_gain[0])
    attend_s = lambda q, k, v, kb, vb: _attn_sample(q, k, v, cache_k[0], cache_v[0], page_table,
                                                    lam_rows, subln_gain[0])

    yp, kp, vp, wp, zrp = _layer(x_prompt[0], mod_p, jnp.arange(S), p, w, attend_p, rwkv_prompt)
    ys, ks_, vs_, ws_, zrs = _layer(x_sample[:, 0], mod_s, jnp.full((DB,), past), p, w, attend_s,
                                    rwkv_sample)
    return (yp.reshape(1, S, D_MODEL), ys.reshape(DB, 1, D_MODEL),
            kp.reshape(1, 1, S, A_HEADS, 2 * A_DH), vp.reshape(1, 1, S, A_HEADS, A_DV),
            wp.reshape(1, 1, R_HEADS, R_DH, R_DH), zrp[S - 1:S].reshape(1, 1, R_IN),
            ks_.reshape(1, DB, 1, A_HEADS, 2 * A_DH), vs_.reshape(1, DB, 1, A_HEADS, A_DV),
            ws_.reshape(1, DB, R_HEADS, R_DH, R_DH), zrs.reshape(1, DB, R_IN))
```

```python
import functools
import math

import jax
import jax.numpy as jnp
from jax import lax
from jax.experimental import pallas as pl
from jax.experimental.pallas import tpu as pltpu

F32 = jnp.float32
BF16 = jnp.bfloat16
HI = lax.Precision.HIGHEST

D_MODEL = 1024
PAGE_SIZE = 128
A_DH = 64
A_DV = 2 * A_DH
A_WIDTH = D_MODEL // 2
A_HEADS = A_WIDTH // A_DV
ROPE_THETA = 10000.0
R_DH = 64
R_WIDTH = D_MODEL // 2
R_HEADS = R_WIDTH // R_DH
DECAY_LORA = 64
AAA_LORA = 64
GATE_LORA = 160
GN_EPS = 64e-5
RMS_EPS = 1e-6
R_OFF_K = R_WIDTH
R_OFF_V = 2 * R_WIDTH
R_OFF_W = 3 * R_WIDTH
R_OFF_A = R_OFF_W + DECAY_LORA
R_OFF_G = R_OFF_A + AAA_LORA
R_IN = R_OFF_G + GATE_LORA
COL_K = A_HEADS * 2 * A_DH
COL_V = 2 * COL_K
COL_GA = COL_V + A_WIDTH
COL_GR = COL_GA + D_MODEL
COL_RW = COL_GR + D_MODEL
D_IN = COL_RW + R_IN
N_GROUPS = 4
EXPERTS_PER_GROUP = 8
N_EXPERTS = N_GROUPS * EXPERTS_PER_GROUP
TOP_K = 2
D_EXPERT = 512
LAM_INIT = 0.8 - 0.6 * math.exp(-0.3 * 0)

LANES = 128
ROUTER_PAD = LANES
NEG_BIG = -1e30
VMEM_LIMIT = 56 * 1024 * 1024

WKV_CHUNK = 64
ATTN_Q_BLOCK = 1024
ATTN_K_BLOCK = 1024
ATTN_ROW_CHUNK = 256

_NN = (((1,), (0,)), ((), ()))
_NT = (((1,), (1,)), ((), ()))
_TN = (((0,), (0,)), ((), ()))
SEQS_PER_STEP = 4
PAGES_PER_STEP = 4
PAGE_BUFFERS = 3
MOE_ROWS = 256


def _cparams(sem):
    return pltpu.CompilerParams(dimension_semantics=sem, vmem_limit_bytes=VMEM_LIMIT)


def _row_tile(n, pref):
    t = min(n, pref)
    assert n % t == 0, (n, t)
    return t


def _seg_ones(width, seg, scale=1.0):
    r = lax.broadcasted_iota(jnp.int32, (width, width), 0) // seg
    c = lax.broadcasted_iota(jnp.int32, (width, width), 1) // seg
    return jnp.where(r == c, scale, 0.0).astype(F32)


def _sigmoid(x):
    return 1.0 / (1.0 + jnp.exp(-x))


def _ada_kernel(c_ref, w_ref, b_ref, o_ref):
    c = c_ref[...]
    s = c * _sigmoid(c)
    o_ref[...] = jnp.dot(s, w_ref[...], precision=HI, preferred_element_type=F32) + b_ref[...]


def _ada(c, w_ada, b_ada):
    rows = c.shape[0]
    n_out = w_ada.shape[1]
    tn = 1536
    return pl.pallas_call(
        _ada_kernel,
        grid=(n_out // tn,),
        in_specs=[pl.BlockSpec((rows, D_MODEL), lambda j: (0, 0)),
                  pl.BlockSpec((D_MODEL, tn), lambda j: (0, j)),
                  pl.BlockSpec((1, tn), lambda j: (0, j))],
        out_specs=pl.BlockSpec((rows, tn), lambda j: (0, j)),
        out_shape=jax.ShapeDtypeStruct((rows, n_out), F32),
        compiler_params=_cparams(("arbitrary",)),
        name="ada",
    )(c, w_ada, b_ada.reshape(1, n_out))


def _mod_spec(rows, tm):
    if rows == 1:
        return pl.BlockSpec((1, D_MODEL), lambda i: (0, 0))
    return pl.BlockSpec((tm, D_MODEL), lambda i: (i, 0))


def _inproj_kernel(x_ref, sc_ref, sh_ref, g_ref, w_ref, qg_ref, kg_ref, cos_ref, sin_ref,
                   q_ref, k_ref, v_ref, ga_ref, gr_ref, zr_ref, kb_ref, vb_ref):
    x = x_ref[...]
    ms = jnp.mean(x * x, axis=-1, keepdims=True)
    h = x * lax.rsqrt(ms + RMS_EPS) * g_ref[...]
    h = h * (1.0 + sc_ref[...]) + sh_ref[...]
    hb = h.astype(BF16)

    def sec(a, b):
        return jnp.dot(hb, w_ref[:, a:b], preferred_element_type=F32)

    seg_mean = _seg_ones(LANES, A_DH, 1.0 / A_DH)
    cos = cos_ref[...]
    sin = sin_ref[...]
    lane = lax.broadcasted_iota(jnp.int32, cos.shape, 1)
    first_half = (lane % A_DH) < (A_DH // 2)

    def norm_rope(z, gain):
        m = jnp.dot(z * z, seg_mean, precision=HI, preferred_element_type=F32)
        zn = z * lax.rsqrt(m + RMS_EPS) * gain
        swapped = jnp.where(first_half, pltpu.roll(zn, LANES - A_DH // 2, 1),
                            pltpu.roll(zn, A_DH // 2, 1))
        return zn * cos + swapped * sin

    zq = sec(0, COL_K)
    zk = sec(COL_K, COL_V)
    for hd in range(A_HEADS):
        sl = slice(hd * LANES, (hd + 1) * LANES)
        q_ref[:, sl] = norm_rope(zq[:, sl], qg_ref[...])
        kh = norm_rope(zk[:, sl], kg_ref[...])
        k_ref[:, sl] = kh
        kb_ref[:, sl] = kh.astype(BF16)
    v = sec(COL_V, COL_GA)
    v_ref[...] = v
    vb_ref[...] = v.astype(BF16)
    ga_ref[...] = _sigmoid(sec(COL_GA, COL_GR))
    gr_ref[...] = _sigmoid(sec(COL_GR, COL_RW))
    zr_ref[...] = sec(COL_RW, D_IN)


def _in_proj(x, sc, sh, g_mix, w_in_bf, q_gain, k_gain, cos, sin):
    n = x.shape[0]
    tm = _row_tile(n, 256)
    row = lambda w: pl.BlockSpec((tm, w), lambda i: (i, 0))
    const = lambda r, w: pl.BlockSpec((r, w), lambda i: (0, 0))
    gain2 = lambda g: jnp.tile(g.reshape(1, A_DH), (1, 2))
    out_w = (A_WIDTH, A_WIDTH, A_WIDTH, D_MODEL, D_MODEL, R_IN)
    return pl.pallas_call(
        _inproj_kernel,
        grid=(n // tm,),
        in_specs=[row(D_MODEL), _mod_spec(sc.shape[0], tm), _mod_spec(sh.shape[0], tm),
                  const(1, D_MODEL), const(D_MODEL, D_IN), const(1, LANES), const(1, LANES),
                  row(LANES), row(LANES)],
        out_specs=[row(w) for w in out_w] + [row(A_WIDTH)] * 2,
        out_shape=[jax.ShapeDtypeStruct((n, w), F32) for w in out_w]
        + [jax.ShapeDtypeStruct((n, A_WIDTH), BF16)] * 2,
        compiler_params=_cparams(("arbitrary",)),
        name="in_proj",
    )(x, sc, sh, g_mix.reshape(1, D_MODEL), w_in_bf, gain2(q_gain), gain2(k_gain), cos, sin)


def _rope_tables(pos):
    half = A_DH // 2
    inv = ROPE_THETA ** (-jnp.arange(half, dtype=F32) / half)
    ang = pos.astype(F32)[:, None] * inv[None, :]
    cos, sin = jnp.cos(ang), jnp.sin(ang)
    return jnp.tile(cos, (1, 4)), jnp.tile(jnp.concatenate([-sin, sin], axis=1), (1, 2))


def _lambda(lq1, lk1, lq2, lk2):
    s1 = jnp.sum(lq1 * lk1, axis=-1, keepdims=True)
    s2 = jnp.sum(lq2 * lk2, axis=-1, keepdims=True)
    return jnp.exp(s1) - jnp.exp(s2) + LAM_INIT


def _subln(o, gain):
    ms = jnp.mean(o * o, axis=-1, keepdims=True)
    return o * lax.rsqrt(ms + RMS_EPS) * gain * (1.0 - LAM_INIT)


def _attn_prompt_kernel(bk, q_ref, k_ref, v_ref, lq1_ref, lk1_ref, lq2_ref, lk2_ref, gain_ref,
                        o_ref, qs_scr, m_scr, acc_scr):
    i = pl.program_id(1)
    bq = q_ref.shape[0]
    rc = min(bq, ATTN_ROW_CHUNK)
    q = q_ref[...] * (A_DH ** -0.5 * math.log2(math.e))
    lane = lax.broadcasted_iota(jnp.int32, q.shape, 1)
    qs_scr[0:bq, :] = jnp.where(lane < A_DH, q, 0.0).astype(BF16)
    qs_scr[bq:2 * bq, :] = jnp.where(lane >= A_DH, q, 0.0).astype(BF16)
    m_scr[...] = jnp.full(m_scr.shape, NEG_BIG, F32)
    acc_scr[...] = jnp.zeros(acc_scr.shape, F32)
    ones = jnp.ones((bk, LANES), BF16)

    def update(start, mask_offset):
        kb = k_ref[pl.ds(start, bk), :]
        vx = jnp.concatenate([v_ref[pl.ds(start, bk), :], ones], axis=1)
        for c in range(2 * bq // rc):
            rows = slice(c * rc, (c + 1) * rc)
            s = lax.dot_general(qs_scr[rows, :], kb, _NT, preferred_element_type=F32)
            if mask_offset is not None:
                row = lax.broadcasted_iota(jnp.int32, (rc, bk), 0) + (c * rc) % bq
                col = lax.broadcasted_iota(jnp.int32, (rc, bk), 1) + mask_offset
                s = jnp.where(col <= row, s, NEG_BIG)
            m_prev = m_scr[rows, :]
            m_new = jnp.maximum(m_prev, jnp.max(s, axis=-1, keepdims=True))
            pr = jnp.exp2((s - jnp.tile(m_new, (1, bk // LANES))).astype(BF16))
            alpha = jnp.exp2(m_prev - m_new)
            acc_scr[rows, :] = jnp.tile(alpha, (1, 2)) * acc_scr[rows, :] + jnp.dot(
                pr, vx, preferred_element_type=F32)
            m_scr[rows, :] = m_new

    def below_diagonal(j, carry):
        update(pl.multiple_of(j * bk, bk), None)
        return carry

    lax.fori_loop(0, i * (bq // bk), below_diagonal, 0)
    for jj in range(bq // bk):
        update(pl.multiple_of(i * bq + jj * bk, bk), jj * bk)
    acc = acc_scr[...]
    d = acc[:, 0:LANES] / acc[:, LANES:2 * LANES]
    lam = _lambda(lq1_ref[...], lk1_ref[...], lq2_ref[...], lk2_ref[...])
    o_ref[...] = _subln(d[0:bq, :] - lam * d[bq:2 * bq, :], gain_ref[...])


def _attn_prompt(q, kb, vb, lam_rows, subln_gain):
    n = q.shape[0]
    bq = _row_tile(n, ATTN_Q_BLOCK)
    bk = _row_tile(bq, ATTN_K_BLOCK)
    const = lambda w: pl.BlockSpec((1, w), lambda h, i: (0, 0))
    head = pl.BlockSpec((n, LANES), lambda h, i: (0, h))
    return pl.pallas_call(
        functools.partial(_attn_prompt_kernel, bk),
        grid=(A_HEADS, n // bq),
        in_specs=[pl.BlockSpec((bq, LANES), lambda h, i: (i, h)), head, head,
                  const(A_DH), const(A_DH), const(A_DH), const(A_DH), const(A_DV)],
        out_specs=pl.BlockSpec((bq, LANES), lambda h, i: (i, h)),
        out_shape=jax.ShapeDtypeStruct((n, A_WIDTH), F32),
        scratch_shapes=[pltpu.VMEM((2 * bq, LANES), BF16), pltpu.VMEM((2 * bq, LANES), F32),
                        pltpu.VMEM((2 * bq, 2 * LANES), F32)],
        compiler_params=_cparams(("arbitrary", "arbitrary")),
        name="attn_prompt",
    )(q, kb, vb, *lam_rows, subln_gain.reshape(1, A_DV))


def _attn_sample_kernel(*refs):
    ns, pps, nbuf = SEQS_PER_STEP, PAGES_PER_STEP, PAGE_BUFFERS
    (pt_ref, q_ref, kn_ref, vn_ref, lq1_ref, lk1_ref, lq2_ref, lk2_ref, gain_ref, ck_ref, cv_ref,
     o_ref, kbuf, vbuf, sems, m_scr, l_scr, acc_scr) = refs
    g = pl.program_id(1)
    ng = pl.num_programs(1)
    step = pl.program_id(0) * ng + g
    total = pl.num_programs(0) * ng

    def page_copies(step_idx, slot):
        bb = step_idx // ng
        gg = step_idx - bb * ng
        copies = []
        for u in range(ns):
            for t in range(pps):
                page = pt_ref[(bb * ns + u) * (ng * pps) + gg * pps + t]
                j = u * pps + t
                copies.append(pltpu.make_async_copy(ck_ref.at[page], kbuf.at[slot, j], sems.at[0, slot]))
                copies.append(pltpu.make_async_copy(cv_ref.at[page], vbuf.at[slot, j], sems.at[1, slot]))
        return copies

    @pl.when(step == 0)
    def _():
        for d in range(nbuf - 1):
            for c in page_copies(d, d):
                c.start()

    ahead = step + (nbuf - 1)

    @pl.when(ahead < total)
    def _():
        for c in page_copies(ahead, ahead % nbuf):
            c.start()

    slot = step % nbuf
    for c in page_copies(step, slot):
        c.wait()
    k_refs = [kbuf.at[slot, j] for j in range(ns * pps)]
    v_refs = [vbuf.at[slot, j] for j in range(ns * pps)]
    rows = 2 * A_HEADS
    page_rows = PAGE_SIZE * A_HEADS
    by_head = lambda x: jnp.concatenate(
        [jnp.broadcast_to(x[:, hd * A_DV:(hd + 1) * A_DV], (2, A_DV)) for hd in range(A_HEADS)],
        axis=0)
    rid = lax.broadcasted_iota(jnp.int32, (rows, A_DV), 0)
    comp = lax.broadcasted_iota(jnp.int32, (rows, A_DV), 1) // A_DH
    srow = lax.broadcasted_iota(jnp.int32, (rows, pps * page_rows), 0) // 2
    scol = lax.broadcasted_iota(jnp.int32, (rows, pps * page_rows), 1) % A_HEADS
    lam = _lambda(lq1_ref[...], lk1_ref[...], lq2_ref[...], lk2_ref[...])

    qms = [jnp.where(rid % 2 == comp, by_head(q_ref[u] * (A_DH ** -0.5)), 0.0) for u in range(ns)]

    @pl.when(g == 0)
    def _():
        for u in range(ns):
            m_scr[u] = jnp.sum(qms[u] * by_head(kn_ref[u]), axis=-1, keepdims=True)
            l_scr[u] = jnp.ones((rows, 1), F32)
            acc_scr[u] = by_head(vn_ref[u])

    for u in range(ns):
        qb = qms[u].astype(BF16)
        s = jnp.concatenate(
            [lax.dot_general(qb, k_refs[u * pps + t][...].astype(BF16), _NT,
                             preferred_element_type=F32) for t in range(pps)], axis=1)
        s = jnp.where(srow == scol, s, NEG_BIG)
        m_prev = m_scr[u]
        m_new = jnp.maximum(m_prev, jnp.max(s, axis=-1, keepdims=True))
        alpha = jnp.exp(m_prev - m_new)
        pr = jnp.exp(s - m_new)
        l_scr[u] = alpha * l_scr[u] + jnp.sum(pr, axis=-1, keepdims=True)
        prb = pr.astype(BF16)
        pv = jnp.dot(prb[:, 0:page_rows], v_refs[u * pps][...].astype(BF16),
                     preferred_element_type=F32)
        for t in range(1, pps):
            pv = pv + jnp.dot(prb[:, t * page_rows:(t + 1) * page_rows],
                              v_refs[u * pps + t][...].astype(BF16), preferred_element_type=F32)
        acc_scr[u] = alpha * acc_scr[u] + pv
        m_scr[u] = m_new

    @pl.when(g == pl.num_programs(1) - 1)
    def _():
        for u in range(ns):
            d = acc_scr[u] / l_scr[u]
            for hd in range(A_HEADS):
                o = d[2 * hd:2 * hd + 1, :] - lam * d[2 * hd + 1:2 * hd + 2, :]
                o_ref[u, :, hd * A_DV:(hd + 1) * A_DV] = _subln(o, gain_ref[...])


def _attn_sample(q, k_new, v_new, cache_k, cache_v, page_table, lam_rows, subln_gain):
    nb, n_pages = page_table.shape
    ns, pps, nbuf = SEQS_PER_STEP, PAGES_PER_STEP, PAGE_BUFFERS
    assert n_pages % pps == 0 and nb % ns == 0
    assert (nb // ns) * (n_pages // pps) >= nbuf - 1
    page_rows = PAGE_SIZE * A_HEADS
    ck = cache_k.reshape(cache_k.shape[0], page_rows, A_DV)
    cv = cache_v.reshape(cache_v.shape[0], page_rows, A_DV)
    pt = page_table.reshape(-1)
    tok = pl.BlockSpec((ns, 1, A_WIDTH), lambda b, g, pt: (b, 0, 0))
    const = lambda w: pl.BlockSpec((1, w), lambda b, g, pt: (0, 0))
    hbm = pl.BlockSpec(memory_space=pl.ANY)
    grid_spec = pltpu.PrefetchScalarGridSpec(
        num_scalar_prefetch=1,
        grid=(nb // ns, n_pages // pps),
        in_specs=[tok, tok, tok, const(A_DH), const(A_DH), const(A_DH), const(A_DH), const(A_DV),
                  hbm, hbm],
        out_specs=tok,
        scratch_shapes=[pltpu.VMEM((nbuf, ns * pps, page_rows, A_DV), F32),
                        pltpu.VMEM((nbuf, ns * pps, page_rows, A_DV), F32),
                        pltpu.SemaphoreType.DMA((2, nbuf)),
                        pltpu.VMEM((ns, 2 * A_HEADS, 1), F32), pltpu.VMEM((ns, 2 * A_HEADS, 1), F32),
                        pltpu.VMEM((ns, 2 * A_HEADS, A_DV), F32)],
    )
    tok3 = lambda a: a.reshape(nb, 1, A_WIDTH)
    out = pl.pallas_call(
        _attn_sample_kernel,
        grid_spec=grid_spec,
        out_shape=jax.ShapeDtypeStruct((nb, 1, A_WIDTH), F32),
        compiler_params=_cparams(("arbitrary", "arbitrary")),
        name="attn_sample",
    )(pt, tok3(q), tok3(k_new), tok3(v_new), *lam_rows, subln_gain.reshape(1, A_DV), ck, cv)
    return out.reshape(nb, A_WIDTH)


def _rwkv_prep_kernel(seq_mode, zr_ref, prev_ref, mu_ref, w0_ref, w2_ref, a0_ref, a2_ref, g2_ref,
                      kkp_ref, ka_ref, r_ref, k_ref, v_ref, kk_ref, a_ref, lw_ref, g_ref, *scr):
    z = zr_ref[...]
    if seq_mode:
        (carry,) = scr

        @pl.when(pl.program_id(0) == 0)
        def _():
            carry[...] = prev_ref[...]

        row = lax.broadcasted_iota(jnp.int32, z.shape, 0)
        zp = jnp.where(row == 0, carry[...], pltpu.roll(z, 1, 0))
        carry[...] = z[z.shape[0] - 1:z.shape[0], :]
    else:
        zp = prev_ref[...]
    zs = z + (zp - z) * mu_ref[...]
    r = zs[:, 0:R_OFF_K]
    k = zs[:, R_OFF_K:R_OFF_V]
    v = zs[:, R_OFF_V:R_OFF_W]
    zw = zs[:, R_OFF_W:R_OFF_A]
    za = zs[:, R_OFF_A:R_OFF_G]
    zg = zs[:, R_OFF_G:R_IN]
    w_pre = w0_ref[...] + jnp.dot(jnp.tanh(zw), w2_ref[...], precision=HI, preferred_element_type=F32)
    nx = -w_pre
    softplus = jnp.maximum(nx, 0.0) + jnp.log(1.0 + jnp.exp(-jnp.abs(nx)))
    lw_ref[...] = -jnp.exp(-softplus - 0.5)
    a = _sigmoid(a0_ref[...] + jnp.dot(za, a2_ref[...], precision=HI, preferred_element_type=F32))
    g_ref[...] = jnp.dot(_sigmoid(zg), g2_ref[...], precision=HI, preferred_element_type=F32)
    kkr = k * kkp_ref[...]
    seg_sum = _seg_ones(LANES, R_DH)
    for sb in range(R_WIDTH // LANES):
        sl = slice(sb * LANES, (sb + 1) * LANES)
        x = kkr[:, sl]
        ss = jnp.dot(x * x, seg_sum, precision=HI, preferred_element_type=F32)
        kk_ref[:, sl] = x / jnp.maximum(jnp.sqrt(ss), 1e-12)
    r_ref[...] = r
    v_ref[...] = v
    a_ref[...] = a
    k_ref[...] = k * (1.0 + (a - 1.0) * ka_ref[...])


def _rwkv_prep(zr, prev, seq_mode, p):
    n = zr.shape[0]
    tm = _row_tile(n, 256)
    row = lambda w: pl.BlockSpec((tm, w), lambda i: (i, 0))
    const = lambda r, w: pl.BlockSpec((r, w), lambda i: (0, 0))
    prev_spec = const(1, R_IN) if seq_mode else row(R_IN)
    vec = lambda a: a.reshape(1, -1)
    return pl.pallas_call(
        functools.partial(_rwkv_prep_kernel, seq_mode),
        grid=(n // tm,),
        in_specs=[row(R_IN), prev_spec, const(1, R_IN), const(1, R_WIDTH),
                  const(DECAY_LORA, R_WIDTH), const(1, R_WIDTH), const(AAA_LORA, R_WIDTH),
                  const(GATE_LORA, R_WIDTH), const(1, R_WIDTH), const(1, R_WIDTH)],
        out_specs=[row(R_WIDTH)] * 7,
        out_shape=[jax.ShapeDtypeStruct((n, R_WIDTH), F32)] * 7,
        scratch_shapes=[pltpu.VMEM((1, R_IN), F32)] if seq_mode else [],
        compiler_params=_cparams(("arbitrary",)),
        name="rwkv_prep_seq" if seq_mode else "rwkv_prep_batch",
    )(zr, prev, vec(p['rw_mu']), vec(p['rw_w0']), p['rw_w2'], vec(p['rw_a0']), p['rw_a2'],
      p['rw_g2'], vec(p['rw_kk']), vec(p['rw_ka']))


def _split(x):
    hi = x.astype(BF16)
    return hi, (x - hi.astype(F32)).astype(BF16)


def _dot3(a, b, dims):
    ah, al = _split(a)
    bh, bl = _split(b)
    d = lambda x, y: lax.dot_general(x, y, dims, preferred_element_type=F32)
    return d(ah, bh) + (d(ah, bl) + d(al, bh))


def _mm(a, b):
    return _dot3(a, b, _NN)


def _mm_nt(a, b):
    return _dot3(a, b, _NT)


def _mm_tn(a, b):
    return _dot3(a, b, _TN)


def _wkv_chunk_kernel(r_ref, k_ref, v_ref, kk_ref, a_ref, lw_ref, s0_ref, y_ref, s_ref):
    c = r_ref.shape[0]

    @pl.when(pl.program_id(0) == 0)
    def _():
        s_ref[...] = s0_ref[...]

    ti = lax.broadcasted_iota(jnp.int32, (c, c), 0)
    si = lax.broadcasted_iota(jnp.int32, (c, c), 1)
    lower = si <= ti
    strict = si < ti
    lw = lw_ref[...]
    cs = jnp.dot(jnp.where(lower, 1.0, 0.0).astype(F32), lw, precision=HI,
                 preferred_element_type=F32)
    total = cs[c - 1:c, :]
    e_pos = jnp.exp(cs)
    e_prev = jnp.exp(cs - lw)
    e_neg = jnp.exp(-cs)
    e_rem = jnp.exp(total - cs)
    e_tot = jnp.exp(total)
    kk = kk_ref[...]
    k = k_ref[...]
    b = kk * a_ref[...]
    at_all = kk * e_prev
    bt_all = b * e_neg
    kt_all = k * e_neg
    rt_all = r_ref[...] * e_pos
    bh_all = b * e_rem
    kh_all = k * e_rem
    v_all = v_ref[...]
    eye = jnp.where(si == ti, 1.0, 0.0).astype(F32)

    heads = range(R_HEADS)
    sls = [slice(h * R_DH, (h + 1) * R_DH) for h in heads]
    at = [at_all[:, sl] for sl in sls]
    rt = [rt_all[:, sl] for sl in sls]
    v = [v_all[:, sl] for sl in sls]
    a4 = [_mm_nt(jnp.concatenate([at[h], rt[h]], axis=0),
                 jnp.concatenate([bt_all[:, sls[h]], kt_all[:, sls[h]]], axis=0)) for h in heads]
    aak = [jnp.where(strict, x[0:c, c:2 * c], 0.0) for x in a4]
    arb = [jnp.where(lower, x[c:2 * c, 0:c], 0.0) for x in a4]
    ark = [jnp.where(lower, x[c:2 * c, c:2 * c], 0.0) for x in a4]
    nl = [jnp.where(strict, -x[0:c, 0:c], 0.0) for x in a4]
    inv = [eye + x for x in nl]
    pw = [_mm(x, x) for x in nl]
    span = 2
    while span < c:
        if 2 * span < c:
            both = [_mm(jnp.concatenate([inv[h], pw[h]], axis=0), pw[h]) for h in heads]
            inv = [inv[h] + both[h][0:c, :] for h in heads]
            pw = [x[c:2 * c, :] for x in both]
        else:
            inv = [inv[h] + _mm(inv[h], pw[h]) for h in heads]
        span *= 2
    av = [_mm(jnp.concatenate([aak[h], ark[h]], axis=0), v[h]) for h in heads]
    tw = [_mm(inv[h], jnp.concatenate([at[h], av[h][0:c, :]], axis=1)) for h in heads]
    kv = [_mm_tn(v[h], kh_all[:, sls[h]]) for h in heads]
    s_old = [s_ref[h] for h in heads]
    hs = [_mm_nt(jnp.concatenate([-tw[h][:, 0:R_DH], rt[h]], axis=0), s_old[h]) for h in heads]
    u = [hs[h][0:c, :] - tw[h][:, R_DH:2 * R_DH] for h in heads]
    au = [_mm(arb[h], u[h]) for h in heads]
    ub = [_mm_tn(u[h], bh_all[:, sls[h]]) for h in heads]
    for h in heads:
        y_ref[:, sls[h]] = hs[h][c:2 * c, :] + au[h] + av[h][c:2 * c, :]
        s_ref[h] = s_old[h] * e_tot[:, sls[h]] + ub[h] + kv[h]


def _wkv_chunk(r, k, v, kk, a, lw, s0):
    n = r.shape[0]
    c = _row_tile(n, WKV_CHUNK)
    row = pl.BlockSpec((c, R_WIDTH), lambda i: (i, 0))
    st = pl.BlockSpec((R_HEADS, R_DH, R_DH), lambda i: (0, 0, 0))
    return pl.pallas_call(
        _wkv_chunk_kernel,
        grid=(n // c,),
        in_specs=[row] * 6 + [st],
        out_specs=[row, st],
        out_shape=[jax.ShapeDtypeStruct((n, R_WIDTH), F32),
                   jax.ShapeDtypeStruct((R_HEADS, R_DH, R_DH), F32)],
        compiler_params=_cparams(("arbitrary",)),
        name="wkv_chunk",
    )(r, k, v, kk, a, lw, s0)


def _wkv_step_kernel(s_ref, r_ref, k_ref, v_ref, kk_ref, a_ref, lw_ref, y_ref, so_ref):
    s = s_ref[...]
    kk = kk_ref[...]
    sa = -jnp.sum(s * kk, axis=-1, keepdims=True)
    s2 = s * jnp.exp(lw_ref[...]) + sa * (kk * a_ref[...]) + v_ref[...] * k_ref[...]
    so_ref[...] = s2
    y_ref[...] = jnp.sum(s2 * r_ref[...], axis=-1, keepdims=True)


def _wkv_step(state, r, k, v, kk, a, lw):
    nb = state.shape[0]
    bs = _row_tile(nb, 8)
    rowv = lambda x: x.reshape(nb, R_HEADS, 1, R_DH)
    st = pl.BlockSpec((bs, R_HEADS, R_DH, R_DH), lambda i: (i, 0, 0, 0))
    rw = pl.BlockSpec((bs, R_HEADS, 1, R_DH), lambda i: (i, 0, 0, 0))
    col = pl.BlockSpec((bs, R_HEADS, R_DH, 1), lambda i: (i, 0, 0, 0))
    y, s_new = pl.pallas_call(
        _wkv_step_kernel,
        grid=(nb // bs,),
        in_specs=[st, rw, rw, col, rw, rw, rw],
        out_specs=[col, st],
        out_shape=[jax.ShapeDtypeStruct((nb, R_HEADS, R_DH, 1), F32),
                   jax.ShapeDtypeStruct(state.shape, F32)],
        compiler_params=_cparams(("arbitrary",)),
        name="wkv_step",
    )(state, rowv(r), rowv(k), v.reshape(nb, R_HEADS, R_DH, 1), rowv(kk), rowv(a), rowv(lw))
    return y.reshape(nb, R_WIDTH), s_new


def _rwkv_post_kernel(y_ref, r_ref, k_ref, v_ref, g_ref, lnw_ref, lnb_ref, rk_ref, o_ref):
    seg_mean = _seg_ones(LANES, R_DH, 1.0 / R_DH)
    seg_sum = _seg_ones(LANES, R_DH)
    for sb in range(R_WIDTH // LANES):
        sl = slice(sb * LANES, (sb + 1) * LANES)
        y = y_ref[:, sl]
        mean = jnp.dot(y, seg_mean, precision=HI, preferred_element_type=F32)
        d = y - mean
        var = jnp.dot(d * d, seg_mean, precision=HI, preferred_element_type=F32)
        yn = d * lax.rsqrt(var + GN_EPS) * lnw_ref[:, sl] + lnb_ref[:, sl]
        bonus = jnp.dot(r_ref[:, sl] * k_ref[:, sl] * rk_ref[:, sl], seg_sum, precision=HI,
                        preferred_element_type=F32)
        o_ref[:, sl] = (yn + bonus * v_ref[:, sl]) * g_ref[:, sl]


def _rwkv_post(y, r, k, v, g, p):
    n = y.shape[0]
    tm = _row_tile(n, 512)
    row = pl.BlockSpec((tm, R_WIDTH), lambda i: (i, 0))
    const = pl.BlockSpec((1, R_WIDTH), lambda i: (0, 0))
    vec = lambda a: a.reshape(1, R_WIDTH)
    return pl.pallas_call(
        _rwkv_post_kernel,
        grid=(n // tm,),
        in_specs=[row] * 5 + [const] * 3,
        out_specs=row,
        out_shape=jax.ShapeDtypeStruct((n, R_WIDTH), F32),
        compiler_params=_cparams(("arbitrary",)),
        name="rwkv_post",
    )(y, r, k, v, g, vec(p['rw_ln_w']), vec(p['rw_ln_b']), vec(p['rw_rk']))


def _merge_kernel(x_ref, o_ref, ro_ref, ga_ref, gr_ref, gt_ref, sc_ref, sh_ref, g_ref,
                  wa_ref, wr_ref, wo_ref, wrt_ref, brt_ref, x1_ref, h2_ref, rt_ref):
    ma = jnp.dot(o_ref[...].astype(BF16), wa_ref[...], preferred_element_type=F32)
    mr = jnp.dot(ro_ref[...].astype(BF16), wr_ref[...], preferred_element_type=F32)
    mg = ga_ref[...] * ma + gr_ref[...] * mr
    merged = jnp.dot(mg.astype(BF16), wo_ref[...], preferred_element_type=F32)
    x1 = x_ref[...] + gt_ref[...] * merged
    x1_ref[...] = x1
    ms = jnp.mean(x1 * x1, axis=-1, keepdims=True)
    h2 = x1 * lax.rsqrt(ms + RMS_EPS) * g_ref[...]
    h2 = h2 * (1.0 + sc_ref[...]) + sh_ref[...]
    h2_ref[...] = h2
    logits = jnp.dot(h2, wrt_ref[...], precision=HI, preferred_element_type=F32) + brt_ref[...]
    rt_ref[...] = _route(logits)


def _route(logits):
    lane = lax.broadcasted_iota(jnp.int32, logits.shape, 1)
    lane_f = lane.astype(F32)
    first_max = lambda x, m: jnp.min(jnp.where(x == m, lane_f, float(LANES)), axis=-1, keepdims=True)
    is_g = lane < N_GROUPS
    lg = jnp.where(is_g, logits, NEG_BIG)
    gmax = jnp.max(lg, axis=-1, keepdims=True)
    g_idx = first_max(lg, gmax)
    sum_g = jnp.sum(jnp.where(is_g, jnp.exp(lg - gmax), 0.0), axis=-1, keepdims=True)
    group_of_lane = ((lane - N_GROUPS) // EXPERTS_PER_GROUP).astype(F32)
    in_group = jnp.where(lane >= N_GROUPS, group_of_lane, -1.0) == g_idx
    le = jnp.where(in_group, logits, NEG_BIG)
    m1 = jnp.max(le, axis=-1, keepdims=True)
    i1 = first_max(le, m1)
    le2 = jnp.where(lane_f == i1, NEG_BIG, le)
    m2 = jnp.max(le2, axis=-1, keepdims=True)
    i2 = first_max(le2, m2)
    t = jnp.exp(m2 - m1)
    w1 = 1.0 / (sum_g * (1.0 + t))
    out = jnp.where(lane == 0, i1 - N_GROUPS, 0.0)
    out = jnp.where(lane == 1, i2 - N_GROUPS, out)
    out = jnp.where(lane == 2, w1, out)
    return jnp.where(lane == 3, w1 * t, out)


def _merge(x, o, ro, ga, gr, gt, sc, sh, g_ffn, wa_bf, wr_bf, wo_bf, w_router, b_router):
    n = x.shape[0]
    tm = _row_tile(n, 256)
    row = lambda w: pl.BlockSpec((tm, w), lambda i: (i, 0))
    const = lambda r, w: pl.BlockSpec((r, w), lambda i: (0, 0))
    mod = lambda a: _mod_spec(a.shape[0], tm)
    return pl.pallas_call(
        _merge_kernel,
        grid=(n // tm,),
        in_specs=[row(D_MODEL), row(A_WIDTH), row(R_WIDTH), row(D_MODEL), row(D_MODEL),
                  mod(gt), mod(sc), mod(sh), const(1, D_MODEL),
                  const(A_WIDTH, D_MODEL), const(R_WIDTH, D_MODEL), const(D_MODEL, D_MODEL),
                  const(D_MODEL, ROUTER_PAD), const(1, ROUTER_PAD)],
        out_specs=[row(D_MODEL), row(D_MODEL), row(ROUTER_PAD)],
        out_shape=[jax.ShapeDtypeStruct((n, D_MODEL), F32), jax.ShapeDtypeStruct((n, D_MODEL), F32),
                   jax.ShapeDtypeStruct((n, ROUTER_PAD), F32)],
        compiler_params=_cparams(("arbitrary",)),
        name="merge",
    )(x, o, ro, ga, gr, gt, sc, sh, g_ffn.reshape(1, D_MODEL), wa_bf, wr_bf, wo_bf,
      w_router, b_router)


def _rank_kernel(rt_ref, pos_ref, cnt_ref, carry):
    tm = rt_ref.shape[0]

    @pl.when(pl.program_id(0) == 0)
    def _():
        carry[...] = jnp.zeros(carry.shape, F32)

    rt = rt_ref[...]
    lane = lax.broadcasted_iota(jnp.int32, rt.shape, 1)
    lane_f = lane.astype(F32)
    oh0 = jnp.where(lane_f == rt[:, 0:1], 1.0, 0.0)
    oh1 = jnp.where(lane_f == rt[:, 1:2], 1.0, 0.0)
    ti = lax.broadcasted_iota(jnp.int32, (tm, tm), 0)
    si = lax.broadcasted_iota(jnp.int32, (tm, tm), 1)
    earlier = jnp.where(si < ti, 1.0, 0.0).astype(BF16)
    pre = jnp.dot(earlier, jnp.concatenate([oh0, oh1], axis=1).astype(BF16),
                  preferred_element_type=F32)
    c = carry[...]
    rank0 = jnp.sum(oh0 * (pre[:, 0:LANES] + c[0:1, :]), axis=-1, keepdims=True)
    rank1 = jnp.sum(oh1 * (pre[:, LANES:2 * LANES] + c[1:2, :]), axis=-1, keepdims=True)
    pos_ref[...] = jnp.where(lane == 0, rank0, jnp.where(lane == 1, rank1, 0.0))
    row = lax.broadcasted_iota(jnp.int32, c.shape, 0)
    c = c + jnp.where(row == 0, jnp.sum(oh0, axis=0, keepdims=True), 0.0) \
          + jnp.where(row == 1, jnp.sum(oh1, axis=0, keepdims=True), 0.0)
    carry[...] = c
    cnt_ref[...] = c


def _rank(route):
    n = route.shape[0]
    tm = _row_tile(n, 256)
    return pl.pallas_call(
        _rank_kernel,
        grid=(n // tm,),
        in_specs=[pl.BlockSpec((tm, LANES), lambda i: (i, 0))],
        out_specs=[pl.BlockSpec((tm, LANES), lambda i: (i, 0)),
                   pl.BlockSpec((8, LANES), lambda i: (0, 0))],
        out_shape=[jax.ShapeDtypeStruct((n, LANES), F32), jax.ShapeDtypeStruct((8, LANES), F32)],
        scratch_shapes=[pltpu.VMEM((8, LANES), F32)],
        compiler_params=_cparams(("arbitrary",)),
        name="moe_rank",
    )(route)


def _slots_kernel(bm, rt_ref, pos_ref, cnt_ref, dest_ref, blk_ref):
    cnt = cnt_ref[...]
    lane = lax.broadcasted_iota(jnp.int32, cnt.shape, 1)
    is_expert = lane < N_EXPERTS
    c0 = jnp.broadcast_to(cnt[0:1, :], cnt.shape)
    padded = jnp.floor((c0 + cnt[1:2, :] + (bm - 1)) * (1.0 / bm)) * bm
    src = lax.broadcasted_iota(jnp.int32, (LANES, LANES), 0)
    dst = lax.broadcasted_iota(jnp.int32, (LANES, LANES), 1)
    pad_end = jnp.dot(padded, jnp.where(src <= dst, 1.0, 0.0).astype(F32), precision=HI,
                      preferred_element_type=F32)
    pad_start = pad_end - padded
    rt = rt_ref[...]
    pos = pos_ref[...]
    tlane = lax.broadcasted_iota(jnp.int32, rt.shape, 1)
    tlane_f = tlane.astype(F32)
    pick = lambda e, table: jnp.sum(jnp.where(tlane_f == e, table[0:1, :], 0.0), axis=-1, keepdims=True)
    d0 = pick(rt[:, 0:1], pad_start) + pos[:, 0:1]
    d1 = pick(rt[:, 1:2], pad_start + c0) + pos[:, 1:2]
    dest_ref[...] = jnp.where(tlane == 0, d0, jnp.where(tlane == 1, d1, 0.0)).astype(jnp.int32)
    nb = blk_ref.shape[0]
    blane = lax.broadcasted_iota(jnp.int32, (nb, LANES), 1)
    start = (lax.broadcasted_iota(jnp.int32, (nb, LANES), 0) * bm).astype(F32)
    ends = jnp.where(blane < N_EXPERTS, pad_end[0:1, :], 3e38)
    expert = jnp.minimum(jnp.sum(jnp.where(ends <= start, 1.0, 0.0), axis=-1, keepdims=True),
                         N_EXPERTS - 1.0)
    total = jnp.max(jnp.where(is_expert, pad_end, 0.0), axis=-1, keepdims=True)[0:1, :]
    used = jnp.where(start < total, 1.0, 0.0)
    blk_ref[...] = jnp.where(blane == 0, expert, jnp.where(blane == 1, used, 0.0)).astype(jnp.int32)


def _slots(route, pos, cnt, n_blocks):
    n = route.shape[0]
    tm = _row_tile(n, 512)
    nbp = -(-n_blocks // 8) * 8
    row = pl.BlockSpec((tm, LANES), lambda i: (i, 0))
    return pl.pallas_call(
        functools.partial(_slots_kernel, MOE_ROWS),
        grid=(n // tm,),
        in_specs=[row, row, pl.BlockSpec((8, LANES), lambda i: (0, 0))],
        out_specs=[row, pl.BlockSpec((nbp, LANES), lambda i: (0, 0))],
        out_shape=[jax.ShapeDtypeStruct((n, LANES), jnp.int32),
                   jax.ShapeDtypeStruct((nbp, LANES), jnp.int32)],
        compiler_params=_cparams(("arbitrary",)),
        name="moe_slots",
    )(route, pos, cnt)


def _row_copy(src, src_row, dst, dst_row, sem):
    return pltpu.make_async_copy(src.at[pl.ds(src_row, 1), :], dst.at[pl.ds(dst_row, 1), :], sem)


def _dispatch_kernel(d0_ref, d1_ref, x_ref, xb_in, xb_ref, sem):
    del xb_in
    tm = x_ref.shape[0]
    base = pl.program_id(0) * tm

    def issue(t, carry):
        _row_copy(x_ref, t, xb_ref, d0_ref[base + t], sem).start()
        _row_copy(x_ref, t, xb_ref, d1_ref[base + t], sem).start()
        return carry

    lax.fori_loop(0, tm, issue, 0, unroll=8)
    for _ in range(TOP_K):
        pltpu.make_async_copy(x_ref, xb_ref.at[pl.ds(0, tm), :], sem).wait()


def _dispatch(h2, dest0, dest1, rows):
    n = h2.shape[0]
    tm = _row_tile(n, 256)
    grid_spec = pltpu.PrefetchScalarGridSpec(
        num_scalar_prefetch=2,
        grid=(n // tm,),
        in_specs=[pl.BlockSpec((tm, D_MODEL), lambda i, d0, d1: (i, 0)),
                  pl.BlockSpec(memory_space=pl.ANY)],
        out_specs=pl.BlockSpec(memory_space=pl.ANY),
        scratch_shapes=[pltpu.SemaphoreType.DMA(())],
    )
    return pl.pallas_call(
        _dispatch_kernel,
        grid_spec=grid_spec,
        out_shape=jax.ShapeDtypeStruct((rows, D_MODEL), F32),
        input_output_aliases={3: 0},
        compiler_params=_cparams(("arbitrary",)),
        name="moe_dispatch",
    )(dest0, dest1, h2, jnp.zeros((rows, D_MODEL), F32))


def _expert_kernel(be_ref, nv_ref, x_ref, wg_ref, wu_ref, wd_ref, y_ref):
    i = pl.program_id(0)
    del be_ref

    @pl.when(nv_ref[i] > 0)
    def _():
        xb = x_ref[...].astype(BF16)
        gate = jnp.dot(xb, wg_ref[...].astype(BF16), preferred_element_type=F32)
        up = jnp.dot(xb, wu_ref[...].astype(BF16), preferred_element_type=F32)
        hdn = gate * _sigmoid(gate) * up
        y_ref[...] = jnp.dot(hdn.astype(BF16), wd_ref[...].astype(BF16),
                             preferred_element_type=F32)

    @pl.when(nv_ref[i] == 0)
    def _():
        y_ref[...] = jnp.zeros(y_ref.shape, F32)


def _experts(xb, blk_e, blk_used, w_gate, w_up, w_down):
    rows = xb.shape[0]
    bm = MOE_ROWS
    wspec = lambda a, b: pl.BlockSpec((None, a, b), lambda i, be, nv: (be[i], 0, 0))
    grid_spec = pltpu.PrefetchScalarGridSpec(
        num_scalar_prefetch=2,
        grid=(rows // bm,),
        in_specs=[pl.BlockSpec((bm, D_MODEL), lambda i, be, nv: (i, 0)),
                  wspec(D_MODEL, D_EXPERT), wspec(D_MODEL, D_EXPERT), wspec(D_EXPERT, D_MODEL)],
        out_specs=pl.BlockSpec((bm, D_MODEL), lambda i, be, nv: (i, 0)),
    )
    return pl.pallas_call(
        _expert_kernel,
        grid_spec=grid_spec,
        out_shape=jax.ShapeDtypeStruct((rows, D_MODEL), F32),
        compiler_params=_cparams(("arbitrary",)),
        name="experts",
    )(blk_e, blk_used, xb, w_gate, w_up, w_down)


def _combine_kernel(d0_ref, d1_ref, x1_ref, rt_ref, gt_ref, yb_ref, o_ref, ya_scr, yb_scr, sem):
    tm = x1_ref.shape[0]
    base = pl.program_id(0) * tm

    def issue(t, carry):
        _row_copy(yb_ref, d0_ref[base + t], ya_scr, t, sem).start()
        _row_copy(yb_ref, d1_ref[base + t], yb_scr, t, sem).start()
        return carry

    lax.fori_loop(0, tm, issue, 0, unroll=8)
    pltpu.make_async_copy(yb_ref.at[pl.ds(0, tm), :], ya_scr, sem).wait()
    pltpu.make_async_copy(yb_ref.at[pl.ds(0, tm), :], yb_scr, sem).wait()
    rt = rt_ref[...]
    moe = rt[:, 2:3] * ya_scr[...] + rt[:, 3:4] * yb_scr[...]
    o_ref[...] = x1_ref[...] + gt_ref[...] * moe


def _combine(x1, route, gt, yb, dest0, dest1):
    n = x1.shape[0]
    tm = _row_tile(n, 256)
    gt_spec = (pl.BlockSpec((1, D_MODEL), lambda i, d0, d1: (0, 0)) if gt.shape[0] == 1
               else pl.BlockSpec((tm, D_MODEL), lambda i, d0, d1: (i, 0)))
    grid_spec = pltpu.PrefetchScalarGridSpec(
        num_scalar_prefetch=2,
        grid=(n // tm,),
        in_specs=[pl.BlockSpec((tm, D_MODEL), lambda i, d0, d1: (i, 0)),
                  pl.BlockSpec((tm, LANES), lambda i, d0, d1: (i, 0)),
                  gt_spec, pl.BlockSpec(memory_space=pl.ANY)],
        out_specs=pl.BlockSpec((tm, D_MODEL), lambda i, d0, d1: (i, 0)),
        scratch_shapes=[pltpu.VMEM((tm, D_MODEL), F32), pltpu.VMEM((tm, D_MODEL), F32),
                        pltpu.SemaphoreType.DMA(())],
    )
    return pl.pallas_call(
        _combine_kernel,
        grid_spec=grid_spec,
        out_shape=jax.ShapeDtypeStruct((n, D_MODEL), F32),
        compiler_params=_cparams(("arbitrary",)),
        name="moe_combine",
    )(dest0, dest1, x1, route, gt, yb)


def _moe(h2, route, x1, gt, w_gate, w_up, w_down):
    n = h2.shape[0]
    bm = MOE_ROWS
    pos, cnt = _rank(route)
    n_blocks = -(-(n * TOP_K) // bm) + N_EXPERTS
    dest, blk = _slots(route, pos, cnt, n_blocks)
    dest0, dest1 = dest[:, 0], dest[:, 1]
    xb = _dispatch(h2, dest0, dest1, n_blocks * bm)
    yb = _experts(xb, blk[:n_blocks, 0], blk[:n_blocks, 1], w_gate, w_up, w_down)
    return _combine(x1, route, gt, yb, dest0, dest1)


def _layer(x, mod, pos, p, w, attend, rwkv):
    sh1, sc1, gt1, sh2, sc2, gt2 = [mod[:, i * D_MODEL:(i + 1) * D_MODEL] for i in range(6)]
    cos, sin = _rope_tables(pos)
    q, k, v, ga, gr, zr, kb, vb = _in_proj(x, sc1, sh1, p['g_mix'], w['w_in'], p['q_gain'],
                                           p['k_gain'], cos, sin)
    o = attend(q, k, v, kb, vb)
    ro, wkv1 = rwkv(zr)
    x1, h2, logits = _merge(x, o, ro, ga, gr, gt1, sc2, sh2, p['g_ffn'], w['w_br_a'], w['w_br_r'],
                            w['w_o'], w['w_router'], w['b_router'])
    y = _moe(h2, logits, x1, gt2, p['w_e_gate'], p['w_e_up'], p['w_e_down'])
    return y, k, v, wkv1, zr


def kernel(x_prompt, x_sample, cache_k, cache_v, state_wkv, state_shift, page_table, c_prompt, c_sample, w_ada, b_ada, g_mix, g_ffn, w_in, q_gain, k_gain, lam_q1, lam_k1, lam_q2, lam_k2, subln_gain, rw_mu, rw_w0, rw_w2, rw_a0, rw_a2, rw_g2, rw_kk, rw_ka, rw_rk, rw_ln_w, rw_ln_b, w_br_a, w_br_r, w_o, w_rg, b_rg, w_re, b_re, w_e_gate, w_e_up, w_e_down):
    assert w_ada.shape[0] == 1, "single-layer kernel"
    B, S, _ = x_prompt.shape
    DB, T, _ = x_sample.shape
    assert B == 1 and T == 1
    past = page_table.shape[1] * PAGE_SIZE
    p = dict(g_mix=g_mix[0], g_ffn=g_ffn[0], q_gain=q_gain[0], k_gain=k_gain[0],
             rw_mu=rw_mu[0], rw_w0=rw_w0[0], rw_w2=rw_w2[0], rw_a0=rw_a0[0], rw_a2=rw_a2[0],
             rw_g2=rw_g2[0], rw_kk=rw_kk[0], rw_ka=rw_ka[0], rw_rk=rw_rk[0],
             rw_ln_w=rw_ln_w[0], rw_ln_b=rw_ln_b[0],
             w_e_gate=w_e_gate[0], w_e_up=w_e_up[0], w_e_down=w_e_down[0])
    pad = ROUTER_PAD - N_GROUPS - N_EXPERTS
    w = dict(w_in=w_in[0].astype(BF16), w_br_a=w_br_a[0].astype(BF16),
             w_br_r=w_br_r[0].astype(BF16), w_o=w_o[0].astype(BF16),
             w_router=jnp.concatenate([w_rg[0], w_re[0], jnp.zeros((D_MODEL, pad), F32)], axis=1),
             b_router=jnp.concatenate([b_rg[0], b_re[0], jnp.zeros((pad,), F32)]).reshape(1, -1))
    lam_rows = [a.reshape(1, A_DH) for a in (lam_q1[0], lam_k1[0], lam_q2[0], lam_k2[0])]

    c_all = jnp.concatenate([c_prompt, jnp.zeros((7, D_MODEL), F32), c_sample], axis=0)
    mod = _ada(c_all, w_ada[0], b_ada[0])
    mod_p, mod_s = mod[0:1], mod[8:8 + DB]

    def rwkv_prompt(zr):
        r, k, v, kk, a, lw, g = _rwkv_prep(zr, jnp.zeros((1, R_IN), F32), True, p)
        y, s1 = _wkv_chunk(r, k, v, kk, a, lw, jnp.zeros((R_HEADS, R_DH, R_DH), F32))
        return _rwkv_post(y, r, k, v, g, p), s1

    def rwkv_sample(zr):
        r, k, v, kk, a, lw, g = _rwkv_prep(zr, state_shift[0], False, p)
        y, s1 = _wkv_step(state_wkv[0], r, k, v, kk, a, lw)
        return _rwkv_post(y, r, k, v, g, p), s1

    attend_p = lambda q, k, v, kb, vb: _attn_prompt(q, kb, vb, lam_rows, subln_gain[0])
    attend_s = lambda q, k, v, kb, vb: _attn_sample(q, k, v, cache_k[0], cache_v[0], page_table,
                                                    lam_rows, subln_gain[0])

    yp, kp, vp, wp, zrp = _layer(x_prompt[0], mod_p, jnp.arange(S), p, w, attend_p, rwkv_prompt)
    ys, ks_, vs_, ws_, zrs = _layer(x_sample[:, 0], mod_s, jnp.full((DB,), past), p, w, attend_s,
                                    rwkv_sample)
    return (yp.reshape(1, S, D_MODEL), ys.reshape(DB, 1, D_MODEL),
            kp.reshape(1, 1, S, A_HEADS, 2 * A_DH), vp.reshape(1, 1, S, A_HEADS, A_DV),
            wp.reshape(1, 1, R_HEADS, R_DH, R_DH), zrp[S - 1:S].reshape(1, 1, R_IN),
            ks_.reshape(1, DB, 1, A_HEADS, 2 * A_DH), vs_.reshape(1, DB, 1, A_HEADS, A_DV),
            ws_.reshape(1, DB, R_HEADS, R_DH, R_DH), zrs.reshape(1, DB, R_IN))
```

```python
import functools
import math

import jax
import jax.numpy as jnp
from jax import lax
from jax.experimental import pallas as pl
from jax.experimental.pallas import tpu as pltpu

F32 = jnp.float32
BF16 = jnp.bfloat16
HI = lax.Precision.HIGHEST

D_MODEL = 1024
PAGE_SIZE = 128
A_DH = 64
A_DV = 2 * A_DH
A_WIDTH = D_MODEL // 2
A_HEADS = A_WIDTH // A_DV
ROPE_THETA = 10000.0
R_DH = 64
R_WIDTH = D_MODEL // 2
R_HEADS = R_WIDTH // R_DH
DECAY_LORA = 64
AAA_LORA = 64
GATE_LORA = 160
GN_EPS = 64e-5
RMS_EPS = 1e-6
R_OFF_K = R_WIDTH
R_OFF_V = 2 * R_WIDTH
R_OFF_W = 3 * R_WIDTH
R_OFF_A = R_OFF_W + DECAY_LORA
R_OFF_G = R_OFF_A + AAA_LORA
R_IN = R_OFF_G + GATE_LORA
COL_K = A_HEADS * 2 * A_DH
COL_V = 2 * COL_K
COL_GA = COL_V + A_WIDTH
COL_GR = COL_GA + D_MODEL
COL_RW = COL_GR + D_MODEL
D_IN = COL_RW + R_IN
N_GROUPS = 4
EXPERTS_PER_GROUP = 8
N_EXPERTS = N_GROUPS * EXPERTS_PER_GROUP
TOP_K = 2
D_EXPERT = 512
LAM_INIT = 0.8 - 0.6 * math.exp(-0.3 * 0)

LANES = 128
ROUTER_PAD = LANES
NEG_BIG = -1e30
VMEM_LIMIT = 56 * 1024 * 1024

WKV_CHUNK = 64
WKV_CHUNKS_PER_STEP = 4
ATTN_Q_BLOCK = 1024
ATTN_K_BLOCK = 1024
ATTN_ROW_CHUNK = 256

_NN = (((1,), (0,)), ((), ()))
_NT = (((1,), (1,)), ((), ()))
_TN = (((0,), (0,)), ((), ()))
SEQS_PER_STEP = 4
PAGES_PER_STEP = 4
PAGE_BUFFERS = 3
MOE_ROWS = 256


def _cparams(sem):
    return pltpu.CompilerParams(dimension_semantics=sem, vmem_limit_bytes=VMEM_LIMIT)


def _row_tile(n, pref):
    t = min(n, pref)
    assert n % t == 0, (n, t)
    return t


def _seg_ones(width, seg, scale=1.0):
    r = lax.broadcasted_iota(jnp.int32, (width, width), 0) // seg
    c = lax.broadcasted_iota(jnp.int32, (width, width), 1) // seg
    return jnp.where(r == c, scale, 0.0).astype(F32)


def _seg_reduce(x, seg):
    hi, lo = _split(x)
    sb = seg.astype(BF16)
    return (jnp.dot(hi, sb, preferred_element_type=F32)
            + jnp.dot(lo, sb, preferred_element_type=F32))


def _sigmoid(x):
    return 1.0 / (1.0 + jnp.exp(-x))


def _ada_kernel(c_ref, w_ref, b_ref, o_ref):
    c = c_ref[...]
    s = c * _sigmoid(c)
    o_ref[...] = jnp.dot(s, w_ref[...], precision=HI, preferred_element_type=F32) + b_ref[...]


def _ada(c, w_ada, b_ada):
    rows = c.shape[0]
    n_out = w_ada.shape[1]
    tn = 1536
    return pl.pallas_call(
        _ada_kernel,
        grid=(n_out // tn,),
        in_specs=[pl.BlockSpec((rows, D_MODEL), lambda j: (0, 0)),
                  pl.BlockSpec((D_MODEL, tn), lambda j: (0, j)),
                  pl.BlockSpec((1, tn), lambda j: (0, j))],
        out_specs=pl.BlockSpec((rows, tn), lambda j: (0, j)),
        out_shape=jax.ShapeDtypeStruct((rows, n_out), F32),
        compiler_params=_cparams(("arbitrary",)),
        name="ada",
    )(c, w_ada, b_ada.reshape(1, n_out))


def _mod_spec(rows, tm):
    if rows == 1:
        return pl.BlockSpec((1, D_MODEL), lambda i: (0, 0))
    return pl.BlockSpec((tm, D_MODEL), lambda i: (i, 0))


def _inproj_kernel(x_ref, sc_ref, sh_ref, g_ref, w_ref, qg_ref, kg_ref, cos_ref, sin_ref,
                   q_ref, k_ref, v_ref, ga_ref, gr_ref, zr_ref, kb_ref, vb_ref):
    x = x_ref[...]
    tm = x.shape[0]
    ms = jnp.mean(x * x, axis=-1, keepdims=True)
    h = x * lax.rsqrt(ms + RMS_EPS) * g_ref[...]
    h = h * (1.0 + sc_ref[...]) + sh_ref[...]
    hb = h.astype(BF16)

    def sec(a, b):
        return jnp.dot(hb, w_ref[:, a:b], preferred_element_type=F32)

    seg_mean = _seg_ones(LANES, A_DH, 1.0 / A_DH)
    cos = cos_ref[...]
    sin = sin_ref[...]
    lane = lax.broadcasted_iota(jnp.int32, cos.shape, 1)
    first_half = (lane % A_DH) < (A_DH // 2)

    def norm_rope(z, gain):
        m = _seg_reduce(z * z, seg_mean)
        zn = z * lax.rsqrt(m + RMS_EPS) * gain
        swapped = jnp.where(first_half, pltpu.roll(zn, LANES - A_DH // 2, 1),
                            pltpu.roll(zn, A_DH // 2, 1))
        return zn * cos + swapped * sin

    zq = sec(0, COL_K)
    zk = sec(COL_K, COL_V)
    for hd in range(A_HEADS):
        sl = slice(hd * LANES, (hd + 1) * LANES)
        q_ref[:, sl] = norm_rope(zq[:, sl], qg_ref[...])
        kh = norm_rope(zk[:, sl], kg_ref[...])
        k_ref[pl.ds(hd, tm, stride=A_HEADS), :] = kh
        kb_ref[:, sl] = kh.astype(BF16)
    v = sec(COL_V, COL_GA)
    for hd in range(A_HEADS):
        v_ref[pl.ds(hd, tm, stride=A_HEADS), :] = v[:, hd * LANES:(hd + 1) * LANES]
    vb_ref[...] = v.astype(BF16)
    ga_ref[...] = _sigmoid(sec(COL_GA, COL_GR))
    gr_ref[...] = _sigmoid(sec(COL_GR, COL_RW))
    zr_ref[...] = sec(COL_RW, D_IN)


def _in_proj(x, sc, sh, g_mix, w_in_bf, q_gain, k_gain, cos, sin):
    n = x.shape[0]
    tm = _row_tile(n, 256)
    row = lambda w: pl.BlockSpec((tm, w), lambda i: (i, 0))
    const = lambda r, w: pl.BlockSpec((r, w), lambda i: (0, 0))
    gain2 = lambda g: jnp.tile(g.reshape(1, A_DH), (1, 2))
    f32_out = lambda w: (row(w), jax.ShapeDtypeStruct((n, w), F32))
    by_head = (pl.BlockSpec((tm * A_HEADS, A_DV), lambda i: (i, 0)),
               jax.ShapeDtypeStruct((n * A_HEADS, A_DV), F32))
    bf_out = (row(A_WIDTH), jax.ShapeDtypeStruct((n, A_WIDTH), BF16))
    outs = [f32_out(A_WIDTH), by_head, by_head, f32_out(D_MODEL), f32_out(D_MODEL), f32_out(R_IN),
            bf_out, bf_out]
    return pl.pallas_call(
        _inproj_kernel,
        grid=(n // tm,),
        in_specs=[row(D_MODEL), _mod_spec(sc.shape[0], tm), _mod_spec(sh.shape[0], tm),
                  const(1, D_MODEL), const(D_MODEL, D_IN), const(1, LANES), const(1, LANES),
                  row(LANES), row(LANES)],
        out_specs=[o[0] for o in outs],
        out_shape=[o[1] for o in outs],
        compiler_params=_cparams(("arbitrary",)),
        name="in_proj",
    )(x, sc, sh, g_mix.reshape(1, D_MODEL), w_in_bf, gain2(q_gain), gain2(k_gain), cos, sin)


def _rope_tables(pos):
    half = A_DH // 2
    inv = ROPE_THETA ** (-jnp.arange(half, dtype=F32) / half)
    ang = pos.astype(F32)[:, None] * inv[None, :]
    cos, sin = jnp.cos(ang), jnp.sin(ang)
    return jnp.tile(cos, (1, 4)), jnp.tile(jnp.concatenate([-sin, sin], axis=1), (1, 2))


def _lambda(lq1, lk1, lq2, lk2):
    s1 = jnp.sum(lq1 * lk1, axis=-1, keepdims=True)
    s2 = jnp.sum(lq2 * lk2, axis=-1, keepdims=True)
    return jnp.exp(s1) - jnp.exp(s2) + LAM_INIT


def _subln(o, gain):
    ms = jnp.mean(o * o, axis=-1, keepdims=True)
    return o * lax.rsqrt(ms + RMS_EPS) * gain * (1.0 - LAM_INIT)


def _attn_prompt_kernel(bk, q_ref, k_ref, v_ref, lq1_ref, lk1_ref, lq2_ref, lk2_ref, gain_ref,
                        o_ref, qs_scr, m_scr, acc_scr):
    i = pl.program_id(1)
    bq = q_ref.shape[0]
    rc = min(bq, ATTN_ROW_CHUNK)
    q = q_ref[...] * (A_DH ** -0.5 * math.log2(math.e))
    lane = lax.broadcasted_iota(jnp.int32, q.shape, 1)
    qs_scr[0:bq, :] = jnp.where(lane < A_DH, q, 0.0).astype(BF16)
    qs_scr[bq:2 * bq, :] = jnp.where(lane >= A_DH, q, 0.0).astype(BF16)
    m_scr[...] = jnp.full(m_scr.shape, NEG_BIG, F32)
    acc_scr[...] = jnp.zeros(acc_scr.shape, F32)
    ones = jnp.ones((bk, LANES), BF16)

    def update(start, mask_offset):
        kb = k_ref[pl.ds(start, bk), :]
        vx = jnp.concatenate([v_ref[pl.ds(start, bk), :], ones], axis=1)
        for c in range(2 * bq // rc):
            rows = slice(c * rc, (c + 1) * rc)
            s = lax.dot_general(qs_scr[rows, :], kb, _NT, preferred_element_type=F32)
            if mask_offset is not None:
                row = lax.broadcasted_iota(jnp.int32, (rc, bk), 0) + (c * rc) % bq
                col = lax.broadcasted_iota(jnp.int32, (rc, bk), 1) + mask_offset
                s = jnp.where(col <= row, s, NEG_BIG)
            m_prev = m_scr[rows, :]
            m_new = jnp.maximum(m_prev, jnp.max(s, axis=-1, keepdims=True))
            pr = jnp.exp2((s - jnp.tile(m_new, (1, bk // LANES))).astype(BF16))
            alpha = jnp.exp2(m_prev - m_new)
            acc_scr[rows, :] = jnp.tile(alpha, (1, 2)) * acc_scr[rows, :] + jnp.dot(
                pr, vx, preferred_element_type=F32)
            m_scr[rows, :] = m_new

    def below_diagonal(j, carry):
        update(pl.multiple_of(j * bk, bk), None)
        return carry

    lax.fori_loop(0, i * (bq // bk), below_diagonal, 0)
    for jj in range(bq // bk):
        update(pl.multiple_of(i * bq + jj * bk, bk), jj * bk)
    acc = acc_scr[...]
    d = acc[:, 0:LANES] / acc[:, LANES:2 * LANES]
    lam = _lambda(lq1_ref[...], lk1_ref[...], lq2_ref[...], lk2_ref[...])
    o_ref[...] = _subln(d[0:bq, :] - lam * d[bq:2 * bq, :], gain_ref[...])


def _attn_prompt(q, kb, vb, lam_rows, subln_gain):
    n = q.shape[0]
    bq = _row_tile(n, ATTN_Q_BLOCK)
    bk = _row_tile(bq, ATTN_K_BLOCK)
    const = lambda w: pl.BlockSpec((1, w), lambda h, i: (0, 0))
    head = pl.BlockSpec((n, LANES), lambda h, i: (0, h))
    return pl.pallas_call(
        functools.partial(_attn_prompt_kernel, bk),
        grid=(A_HEADS, n // bq),
        in_specs=[pl.BlockSpec((bq, LANES), lambda h, i: (i, h)), head, head,
                  const(A_DH), const(A_DH), const(A_DH), const(A_DH), const(A_DV)],
        out_specs=pl.BlockSpec((bq, LANES), lambda h, i: (i, h)),
        out_shape=jax.ShapeDtypeStruct((n, A_WIDTH), F32),
        scratch_shapes=[pltpu.VMEM((2 * bq, LANES), BF16), pltpu.VMEM((2 * bq, LANES), F32),
                        pltpu.VMEM((2 * bq, 2 * LANES), F32)],
        compiler_params=_cparams(("arbitrary", "arbitrary")),
        name="attn_prompt",
    )(q, kb, vb, *lam_rows, subln_gain.reshape(1, A_DV))


def _attn_sample_kernel(*refs):
    ns, pps, nbuf = SEQS_PER_STEP, PAGES_PER_STEP, PAGE_BUFFERS
    (pt_ref, q_ref, kn_ref, vn_ref, lq1_ref, lk1_ref, lq2_ref, lk2_ref, gain_ref, ck_ref, cv_ref,
     o_ref, kbuf, vbuf, sems, m_scr, l_scr, acc_scr) = refs
    g = pl.program_id(1)
    ng = pl.num_programs(1)
    step = pl.program_id(0) * ng + g
    total = pl.num_programs(0) * ng

    def page_copies(step_idx, slot):
        bb = step_idx // ng
        gg = step_idx - bb * ng
        copies = []
        for u in range(ns):
            for t in range(pps):
                page = pt_ref[(bb * ns + u) * (ng * pps) + gg * pps + t]
                j = u * pps + t
                copies.append(pltpu.make_async_copy(ck_ref.at[page], kbuf.at[slot, j], sems.at[0, slot]))
                copies.append(pltpu.make_async_copy(cv_ref.at[page], vbuf.at[slot, j], sems.at[1, slot]))
        return copies

    @pl.when(step == 0)
    def _():
        for d in range(nbuf - 1):
            for c in page_copies(d, d):
                c.start()

    ahead = step + (nbuf - 1)

    @pl.when(ahead < total)
    def _():
        for c in page_copies(ahead, ahead % nbuf):
            c.start()

    slot = step % nbuf
    for c in page_copies(step, slot):
        c.wait()
    k_refs = [kbuf.at[slot, j] for j in range(ns * pps)]
    v_refs = [vbuf.at[slot, j] for j in range(ns * pps)]
    rows = 2 * A_HEADS
    page_rows = PAGE_SIZE * A_HEADS
    by_head = lambda x: jnp.concatenate(
        [jnp.broadcast_to(x[:, hd * A_DV:(hd + 1) * A_DV], (2, A_DV)) for hd in range(A_HEADS)],
        axis=0)
    rid = lax.broadcasted_iota(jnp.int32, (rows, A_DV), 0)
    comp = lax.broadcasted_iota(jnp.int32, (rows, A_DV), 1) // A_DH
    srow = lax.broadcasted_iota(jnp.int32, (rows, pps * page_rows), 0) // 2
    scol = lax.broadcasted_iota(jnp.int32, (rows, pps * page_rows), 1) % A_HEADS
    lam = _lambda(lq1_ref[...], lk1_ref[...], lq2_ref[...], lk2_ref[...])

    qms = [jnp.where(rid % 2 == comp, by_head(q_ref[u] * (A_DH ** -0.5)), 0.0) for u in range(ns)]

    @pl.when(g == 0)
    def _():
        for u in range(ns):
            m_scr[u] = jnp.sum(qms[u] * by_head(kn_ref[u]), axis=-1, keepdims=True)
            l_scr[u] = jnp.ones((rows, 1), F32)
            acc_scr[u] = by_head(vn_ref[u])

    for u in range(ns):
        qb = qms[u].astype(BF16)
        s = jnp.concatenate(
            [lax.dot_general(qb, k_refs[u * pps + t][...].astype(BF16), _NT,
                             preferred_element_type=F32) for t in range(pps)], axis=1)
        s = jnp.where(srow == scol, s, NEG_BIG)
        m_prev = m_scr[u]
        m_new = jnp.maximum(m_prev, jnp.max(s, axis=-1, keepdims=True))
        alpha = jnp.exp(m_prev - m_new)
        pr = jnp.exp(s - m_new)
        l_scr[u] = alpha * l_scr[u] + jnp.sum(pr, axis=-1, keepdims=True)
        prb = pr.astype(BF16)
        pv = jnp.dot(prb[:, 0:page_rows], v_refs[u * pps][...].astype(BF16),
                     preferred_element_type=F32)
        for t in range(1, pps):
            pv = pv + jnp.dot(prb[:, t * page_rows:(t + 1) * page_rows],
                              v_refs[u * pps + t][...].astype(BF16), preferred_element_type=F32)
        acc_scr[u] = alpha * acc_scr[u] + pv
        m_scr[u] = m_new

    @pl.when(g == pl.num_programs(1) - 1)
    def _():
        for u in range(ns):
            d = acc_scr[u] / l_scr[u]
            for hd in range(A_HEADS):
                o = d[2 * hd:2 * hd + 1, :] - lam * d[2 * hd + 1:2 * hd + 2, :]
                o_ref[u, :, hd * A_DV:(hd + 1) * A_DV] = _subln(o, gain_ref[...])


def _attn_sample(q, k_new, v_new, cache_k, cache_v, page_table, lam_rows, subln_gain):
    nb, n_pages = page_table.shape
    ns, pps, nbuf = SEQS_PER_STEP, PAGES_PER_STEP, PAGE_BUFFERS
    assert n_pages % pps == 0 and nb % ns == 0
    assert (nb // ns) * (n_pages // pps) >= nbuf - 1
    page_rows = PAGE_SIZE * A_HEADS
    ck = cache_k.reshape(cache_k.shape[0], page_rows, A_DV)
    cv = cache_v.reshape(cache_v.shape[0], page_rows, A_DV)
    pt = page_table.reshape(-1)
    tok = pl.BlockSpec((ns, 1, A_WIDTH), lambda b, g, pt: (b, 0, 0))
    const = lambda w: pl.BlockSpec((1, w), lambda b, g, pt: (0, 0))
    hbm = pl.BlockSpec(memory_space=pl.ANY)
    grid_spec = pltpu.PrefetchScalarGridSpec(
        num_scalar_prefetch=1,
        grid=(nb // ns, n_pages // pps),
        in_specs=[tok, tok, tok, const(A_DH), const(A_DH), const(A_DH), const(A_DH), const(A_DV),
                  hbm, hbm],
        out_specs=tok,
        scratch_shapes=[pltpu.VMEM((nbuf, ns * pps, page_rows, A_DV), F32),
                        pltpu.VMEM((nbuf, ns * pps, page_rows, A_DV), F32),
                        pltpu.SemaphoreType.DMA((2, nbuf)),
                        pltpu.VMEM((ns, 2 * A_HEADS, 1), F32), pltpu.VMEM((ns, 2 * A_HEADS, 1), F32),
                        pltpu.VMEM((ns, 2 * A_HEADS, A_DV), F32)],
    )
    tok3 = lambda a: a.reshape(nb, 1, A_WIDTH)
    out = pl.pallas_call(
        _attn_sample_kernel,
        grid_spec=grid_spec,
        out_shape=jax.ShapeDtypeStruct((nb, 1, A_WIDTH), F32),
        compiler_params=_cparams(("arbitrary", "arbitrary")),
        name="attn_sample",
    )(pt, tok3(q), tok3(k_new), tok3(v_new), *lam_rows, subln_gain.reshape(1, A_DV), ck, cv)
    return out.reshape(nb, A_WIDTH)


def _rwkv_prep_kernel(seq_mode, zr_ref, prev_ref, mu_ref, w0_ref, w2_ref, a0_ref, a2_ref, g2_ref,
                      kkp_ref, ka_ref, r_ref, k_ref, v_ref, kk_ref, a_ref, lw_ref, g_ref, *scr):
    z = zr_ref[...]
    if seq_mode:
        (carry,) = scr

        @pl.when(pl.program_id(0) == 0)
        def _():
            carry[...] = prev_ref[...]

        row = lax.broadcasted_iota(jnp.int32, z.shape, 0)
        zp = jnp.where(row == 0, carry[...], pltpu.roll(z, 1, 0))
        carry[...] = z[z.shape[0] - 1:z.shape[0], :]
    else:
        zp = prev_ref[...]
    zs = z + (zp - z) * mu_ref[...]
    r = zs[:, 0:R_OFF_K]
    k = zs[:, R_OFF_K:R_OFF_V]
    v = zs[:, R_OFF_V:R_OFF_W]
    zw = zs[:, R_OFF_W:R_OFF_A]
    za = zs[:, R_OFF_A:R_OFF_G]
    zg = zs[:, R_OFF_G:R_IN]
    w_pre = w0_ref[...] + _dot3(jnp.tanh(zw), w2_ref[...], _NN)
    nx = -w_pre
    softplus = jnp.maximum(nx, 0.0) + jnp.log(1.0 + jnp.exp(-jnp.abs(nx)))
    lw_ref[...] = -jnp.exp(-softplus - 0.5)
    a = _sigmoid(a0_ref[...] + _dot3(za, a2_ref[...], _NN))
    g_ref[...] = _mm1(_sigmoid(zg), g2_ref[...])
    kkr = k * kkp_ref[...]
    seg_sum = _seg_ones(LANES, R_DH)
    for sb in range(R_WIDTH // LANES):
        sl = slice(sb * LANES, (sb + 1) * LANES)
        x = kkr[:, sl]
        ss = _seg_reduce(x * x, seg_sum)
        kk_ref[:, sl] = x / jnp.maximum(jnp.sqrt(ss), 1e-12)
    r_ref[...] = r
    v_ref[...] = v
    a_ref[...] = a
    k_ref[...] = k * (1.0 + (a - 1.0) * ka_ref[...])


def _rwkv_prep(zr, prev, seq_mode, p):
    n = zr.shape[0]
    tm = _row_tile(n, 256)
    row = lambda w: pl.BlockSpec((tm, w), lambda i: (i, 0))
    const = lambda r, w: pl.BlockSpec((r, w), lambda i: (0, 0))
    prev_spec = const(1, R_IN) if seq_mode else row(R_IN)
    vec = lambda a: a.reshape(1, -1)
    return pl.pallas_call(
        functools.partial(_rwkv_prep_kernel, seq_mode),
        grid=(n // tm,),
        in_specs=[row(R_IN), prev_spec, const(1, R_IN), const(1, R_WIDTH),
                  const(DECAY_LORA, R_WIDTH), const(1, R_WIDTH), const(AAA_LORA, R_WIDTH),
                  const(GATE_LORA, R_WIDTH), const(1, R_WIDTH), const(1, R_WIDTH)],
        out_specs=[row(R_WIDTH)] * 7,
        out_shape=[jax.ShapeDtypeStruct((n, R_WIDTH), F32)] * 7,
        scratch_shapes=[pltpu.VMEM((1, R_IN), F32)] if seq_mode else [],
        compiler_params=_cparams(("arbitrary",)),
        name="rwkv_prep_seq" if seq_mode else "rwkv_prep_batch",
    )(zr, prev, vec(p['rw_mu']), vec(p['rw_w0']), p['rw_w2'], vec(p['rw_a0']), p['rw_a2'],
      p['rw_g2'], vec(p['rw_kk']), vec(p['rw_ka']))


def _split(x):
    hi = x.astype(BF16)
    return hi, (x - hi.astype(F32)).astype(BF16)


def _dot3(a, b, dims):
    ah, al = _split(a)
    bh, bl = _split(b)
    d = lambda x, y: lax.dot_general(x, y, dims, preferred_element_type=F32)
    return d(ah, bh) + (d(ah, bl) + d(al, bh))


def _mm1(a, b):
    return jnp.dot(a.astype(BF16), b.astype(BF16), preferred_element_type=F32)


def _mm(a, b):
    return _dot3(a, b, _NN)


def _mm_nt(a, b):
    return _dot3(a, b, _NT)


def _mm_tn(a, b):
    return _dot3(a, b, _TN)


def _wkv_chunk_kernel(nch, r_ref, k_ref, v_ref, kk_ref, a_ref, lw_ref, s0_ref, y_ref, s_ref):
    rows = r_ref.shape[0]
    c = rows // nch

    @pl.when(pl.program_id(0) == 0)
    def _():
        s_ref[...] = s0_ref[...]

    ti = lax.broadcasted_iota(jnp.int32, (c, c), 0)
    si = lax.broadcasted_iota(jnp.int32, (c, c), 1)
    lower = si <= ti
    strict = si < ti
    lw = lw_ref[...]
    bt_i = lax.broadcasted_iota(jnp.int32, (rows, rows), 0)
    bs_i = lax.broadcasted_iota(jnp.int32, (rows, rows), 1)
    same_chunk_lower = jnp.logical_and(bs_i <= bt_i, bs_i // c == bt_i // c)
    cs = jnp.dot(jnp.where(same_chunk_lower, 1.0, 0.0).astype(F32), lw, precision=HI,
                 preferred_element_type=F32)
    chunk_rows = [slice(ci * c, (ci + 1) * c) for ci in range(nch)]
    total = jnp.concatenate(
        [jnp.broadcast_to(cs[(ci + 1) * c - 1:(ci + 1) * c, :], (c, R_WIDTH)) for ci in range(nch)],
        axis=0)
    e_pos = jnp.exp(cs)
    e_prev = jnp.exp(cs - lw)
    e_neg = jnp.exp(-cs)
    e_rem = jnp.exp(total - cs)
    e_tot = jnp.exp(total)
    kk = kk_ref[...]
    k = k_ref[...]
    b = kk * a_ref[...]
    at_all = kk * e_prev
    bt_all = b * e_neg
    kt_all = k * e_neg
    rt_all = r_ref[...] * e_pos
    bh_all = b * e_rem
    kh_all = k * e_rem
    v_all = v_ref[...]
    eye = jnp.where(si == ti, 1.0, 0.0).astype(F32)

    pairs = [(ci, h) for ci in range(nch) for h in range(R_HEADS)]
    heads = range(len(pairs))
    sub = lambda x, p: x[chunk_rows[p[0]], p[1] * R_DH:(p[1] + 1) * R_DH]
    at = [sub(at_all, p) for p in pairs]
    rt = [sub(rt_all, p) for p in pairs]
    v = [sub(v_all, p) for p in pairs]
    a4 = [_mm_nt(jnp.concatenate([at[h], rt[h]], axis=0),
                 jnp.concatenate([sub(bt_all, pairs[h]), sub(kt_all, pairs[h])], axis=0))
          for h in heads]
    aak = [jnp.where(strict, x[0:c, c:2 * c], 0.0) for x in a4]
    arb = [jnp.where(lower, x[c:2 * c, 0:c], 0.0) for x in a4]
    ark = [jnp.where(lower, x[c:2 * c, c:2 * c], 0.0) for x in a4]
    nl = [jnp.where(strict, -x[0:c, 0:c], 0.0) for x in a4]
    inv = [eye + x for x in nl]
    pw = [_mm(x, x) for x in nl]
    span = 2
    while span < c:
        if 2 * span < c:
            both = [_mm1(jnp.concatenate([inv[h], pw[h]], axis=0), pw[h]) for h in heads]
            inv = [inv[h] + both[h][0:c, :] for h in heads]
            pw = [x[c:2 * c, :] for x in both]
        else:
            inv = [inv[h] + _mm1(inv[h], pw[h]) for h in heads]
        span *= 2
    av = [_mm(jnp.concatenate([aak[h], ark[h]], axis=0), v[h]) for h in heads]
    tw = [_mm(inv[h], jnp.concatenate([at[h], av[h][0:c, :]], axis=1)) for h in heads]
    kv = [_mm_tn(v[h], sub(kh_all, pairs[h])) for h in heads]
    state = [s_ref[hd] for hd in range(R_HEADS)]
    for ci in range(nch):
        idx = [ci * R_HEADS + hd for hd in range(R_HEADS)]
        hs = [_mm_nt(jnp.concatenate([-tw[h][:, 0:R_DH], rt[h]], axis=0), state[h % R_HEADS])
              for h in idx]
        u = [hs[j][0:c, :] - tw[h][:, R_DH:2 * R_DH] for j, h in enumerate(idx)]
        au = [_mm(arb[h], u[j]) for j, h in enumerate(idx)]
        ub = [_mm_tn(u[j], sub(bh_all, pairs[h])) for j, h in enumerate(idx)]
        for j, h in enumerate(idx):
            p = pairs[h]
            lanes = slice(p[1] * R_DH, (p[1] + 1) * R_DH)
            y_ref[chunk_rows[ci], lanes] = hs[j][c:2 * c, :] + au[j] + av[h][c:2 * c, :]
            state[j] = state[j] * e_tot[ci * c:ci * c + 1, lanes] + ub[j] + kv[h]
    for hd in range(R_HEADS):
        s_ref[hd] = state[hd]


def _wkv_chunk(r, k, v, kk, a, lw, s0):
    n = r.shape[0]
    c = _row_tile(n, WKV_CHUNK)
    nch = WKV_CHUNKS_PER_STEP if n % (c * WKV_CHUNKS_PER_STEP) == 0 else 1
    row = pl.BlockSpec((c * nch, R_WIDTH), lambda i: (i, 0))
    st = pl.BlockSpec((R_HEADS, R_DH, R_DH), lambda i: (0, 0, 0))
    return pl.pallas_call(
        functools.partial(_wkv_chunk_kernel, nch),
        grid=(n // (c * nch),),
        in_specs=[row] * 6 + [st],
        out_specs=[row, st],
        out_shape=[jax.ShapeDtypeStruct((n, R_WIDTH), F32),
                   jax.ShapeDtypeStruct((R_HEADS, R_DH, R_DH), F32)],
        compiler_params=_cparams(("arbitrary",)),
        name="wkv_chunk",
    )(r, k, v, kk, a, lw, s0)


def _wkv_step_kernel(s_ref, r_ref, k_ref, v_ref, kk_ref, a_ref, lw_ref, y_ref, so_ref):
    s = s_ref[...]
    kk = kk_ref[...]
    sa = -jnp.sum(s * kk, axis=-1, keepdims=True)
    s2 = s * jnp.exp(lw_ref[...]) + sa * (kk * a_ref[...]) + v_ref[...] * k_ref[...]
    so_ref[...] = s2
    y_ref[...] = jnp.sum(s2 * r_ref[...], axis=-1, keepdims=True)


def _wkv_step(state, r, k, v, kk, a, lw):
    nb = state.shape[0]
    bs = _row_tile(nb, 8)
    rowv = lambda x: x.reshape(nb, R_HEADS, 1, R_DH)
    st = pl.BlockSpec((bs, R_HEADS, R_DH, R_DH), lambda i: (i, 0, 0, 0))
    rw = pl.BlockSpec((bs, R_HEADS, 1, R_DH), lambda i: (i, 0, 0, 0))
    col = pl.BlockSpec((bs, R_HEADS, R_DH, 1), lambda i: (i, 0, 0, 0))
    y, s_new = pl.pallas_call(
        _wkv_step_kernel,
        grid=(nb // bs,),
        in_specs=[st, rw, rw, col, rw, rw, rw],
        out_specs=[col, st],
        out_shape=[jax.ShapeDtypeStruct((nb, R_HEADS, R_DH, 1), F32),
                   jax.ShapeDtypeStruct(state.shape, F32)],
        compiler_params=_cparams(("arbitrary",)),
        name="wkv_step",
    )(state, rowv(r), rowv(k), v.reshape(nb, R_HEADS, R_DH, 1), rowv(kk), rowv(a), rowv(lw))
    return y.reshape(nb, R_WIDTH), s_new


def _rwkv_post_kernel(y_ref, r_ref, k_ref, v_ref, g_ref, lnw_ref, lnb_ref, rk_ref, o_ref):
    seg_mean = _seg_ones(LANES, R_DH, 1.0 / R_DH)
    seg_sum = _seg_ones(LANES, R_DH)
    for sb in range(R_WIDTH // LANES):
        sl = slice(sb * LANES, (sb + 1) * LANES)
        y = y_ref[:, sl]
        mean = _seg_reduce(y, seg_mean)
        d = y - mean
        var = _seg_reduce(d * d, seg_mean)
        yn = d * lax.rsqrt(var + GN_EPS) * lnw_ref[:, sl] + lnb_ref[:, sl]
        bonus = _seg_reduce(r_ref[:, sl] * k_ref[:, sl] * rk_ref[:, sl], seg_sum)
        o_ref[:, sl] = (yn + bonus * v_ref[:, sl]) * g_ref[:, sl]


def _rwkv_post(y, r, k, v, g, p):
    n = y.shape[0]
    tm = _row_tile(n, 512)
    row = pl.BlockSpec((tm, R_WIDTH), lambda i: (i, 0))
    const = pl.BlockSpec((1, R_WIDTH), lambda i: (0, 0))
    vec = lambda a: a.reshape(1, R_WIDTH)
    return pl.pallas_call(
        _rwkv_post_kernel,
        grid=(n // tm,),
        in_specs=[row] * 5 + [const] * 3,
        out_specs=row,
        out_shape=jax.ShapeDtypeStruct((n, R_WIDTH), F32),
        compiler_params=_cparams(("arbitrary",)),
        name="rwkv_post",
    )(y, r, k, v, g, vec(p['rw_ln_w']), vec(p['rw_ln_b']), vec(p['rw_rk']))


def _merge_kernel(x_ref, o_ref, ro_ref, ga_ref, gr_ref, gt_ref, sc_ref, sh_ref, g_ref,
                  wa_ref, wr_ref, wo_ref, wrt_ref, brt_ref, x1_ref, h2_ref, rt_ref):
    ma = jnp.dot(o_ref[...].astype(BF16), wa_ref[...], preferred_element_type=F32)
    mr = jnp.dot(ro_ref[...].astype(BF16), wr_ref[...], preferred_element_type=F32)
    mg = ga_ref[...] * ma + gr_ref[...] * mr
    merged = jnp.dot(mg.astype(BF16), wo_ref[...], preferred_element_type=F32)
    x1 = x_ref[...] + gt_ref[...] * merged
    x1_ref[...] = x1
    ms = jnp.mean(x1 * x1, axis=-1, keepdims=True)
    h2 = x1 * lax.rsqrt(ms + RMS_EPS) * g_ref[...]
    h2 = h2 * (1.0 + sc_ref[...]) + sh_ref[...]
    h2_ref[...] = h2
    logits = jnp.dot(h2, wrt_ref[...], precision=HI, preferred_element_type=F32) + brt_ref[...]
    rt_ref[...] = _route(logits)


def _route(logits):
    lane = lax.broadcasted_iota(jnp.int32, logits.shape, 1)
    lane_f = lane.astype(F32)
    first_max = lambda x, m: jnp.min(jnp.where(x == m, lane_f, float(LANES)), axis=-1, keepdims=True)
    is_g = lane < N_GROUPS
    lg = jnp.where(is_g, logits, NEG_BIG)
    gmax = jnp.max(lg, axis=-1, keepdims=True)
    g_idx = first_max(lg, gmax)
    sum_g = jnp.sum(jnp.where(is_g, jnp.exp(lg - gmax), 0.0), axis=-1, keepdims=True)
    group_of_lane = ((lane - N_GROUPS) // EXPERTS_PER_GROUP).astype(F32)
    in_group = jnp.where(lane >= N_GROUPS, group_of_lane, -1.0) == g_idx
    le = jnp.where(in_group, logits, NEG_BIG)
    m1 = jnp.max(le, axis=-1, keepdims=True)
    i1 = first_max(le, m1)
    le2 = jnp.where(lane_f == i1, NEG_BIG, le)
    m2 = jnp.max(le2, axis=-1, keepdims=True)
    i2 = first_max(le2, m2)
    t = jnp.exp(m2 - m1)
    w1 = 1.0 / (sum_g * (1.0 + t))
    out = jnp.where(lane == 0, i1 - N_GROUPS, 0.0)
    out = jnp.where(lane == 1, i2 - N_GROUPS, out)
    out = jnp.where(lane == 2, w1, out)
    return jnp.where(lane == 3, w1 * t, out)


def _merge(x, o, ro, ga, gr, gt, sc, sh, g_ffn, wa_bf, wr_bf, wo_bf, w_router, b_router):
    n = x.shape[0]
    tm = _row_tile(n, 256)
    row = lambda w: pl.BlockSpec((tm, w), lambda i: (i, 0))
    const = lambda r, w: pl.BlockSpec((r, w), lambda i: (0, 0))
    mod = lambda a: _mod_spec(a.shape[0], tm)
    return pl.pallas_call(
        _merge_kernel,
        grid=(n // tm,),
        in_specs=[row(D_MODEL), row(A_WIDTH), row(R_WIDTH), row(D_MODEL), row(D_MODEL),
                  mod(gt), mod(sc), mod(sh), const(1, D_MODEL),
                  const(A_WIDTH, D_MODEL), const(R_WIDTH, D_MODEL), const(D_MODEL, D_MODEL),
                  const(D_MODEL, ROUTER_PAD), const(1, ROUTER_PAD)],
        out_specs=[row(D_MODEL), row(D_MODEL), row(ROUTER_PAD)],
        out_shape=[jax.ShapeDtypeStruct((n, D_MODEL), F32), jax.ShapeDtypeStruct((n, D_MODEL), F32),
                   jax.ShapeDtypeStruct((n, ROUTER_PAD), F32)],
        compiler_params=_cparams(("arbitrary",)),
        name="merge",
    )(x, o, ro, ga, gr, gt, sc, sh, g_ffn.reshape(1, D_MODEL), wa_bf, wr_bf, wo_bf,
      w_router, b_router)


def _rank_kernel(rt_ref, pos_ref, cnt_ref, carry):
    tm = rt_ref.shape[0]

    @pl.when(pl.program_id(0) == 0)
    def _():
        carry[...] = jnp.zeros(carry.shape, F32)

    rt = rt_ref[...]
    lane = lax.broadcasted_iota(jnp.int32, rt.shape, 1)
    lane_f = lane.astype(F32)
    oh0 = jnp.where(lane_f == rt[:, 0:1], 1.0, 0.0)
    oh1 = jnp.where(lane_f == rt[:, 1:2], 1.0, 0.0)
    ti = lax.broadcasted_iota(jnp.int32, (tm, tm), 0)
    si = lax.broadcasted_iota(jnp.int32, (tm, tm), 1)
    earlier = jnp.where(si < ti, 1.0, 0.0).astype(BF16)
    pre = jnp.dot(earlier, jnp.concatenate([oh0, oh1], axis=1).astype(BF16),
                  preferred_element_type=F32)
    c = carry[...]
    rank0 = jnp.sum(oh0 * (pre[:, 0:LANES] + c[0:1, :]), axis=-1, keepdims=True)
    rank1 = jnp.sum(oh1 * (pre[:, LANES:2 * LANES] + c[1:2, :]), axis=-1, keepdims=True)
    pos_ref[...] = jnp.where(lane == 0, rank0, jnp.where(lane == 1, rank1, 0.0))
    row = lax.broadcasted_iota(jnp.int32, c.shape, 0)
    c = c + jnp.where(row == 0, jnp.sum(oh0, axis=0, keepdims=True), 0.0) \
          + jnp.where(row == 1, jnp.sum(oh1, axis=0, keepdims=True), 0.0)
    carry[...] = c
    cnt_ref[...] = c


def _rank(route):
    n = route.shape[0]
    tm = _row_tile(n, 256)
    return pl.pallas_call(
        _rank_kernel,
        grid=(n // tm,),
        in_specs=[pl.BlockSpec((tm, LANES), lambda i: (i, 0))],
        out_specs=[pl.BlockSpec((tm, LANES), lambda i: (i, 0)),
                   pl.BlockSpec((8, LANES), lambda i: (0, 0))],
        out_shape=[jax.ShapeDtypeStruct((n, LANES), F32), jax.ShapeDtypeStruct((8, LANES), F32)],
        scratch_shapes=[pltpu.VMEM((8, LANES), F32)],
        compiler_params=_cparams(("arbitrary",)),
        name="moe_rank",
    )(route)


def _slots_kernel(bm, rt_ref, pos_ref, cnt_ref, dest_ref, blk_ref):
    cnt = cnt_ref[...]
    lane = lax.broadcasted_iota(jnp.int32, cnt.shape, 1)
    is_expert = lane < N_EXPERTS
    c0 = jnp.broadcast_to(cnt[0:1, :], cnt.shape)
    padded = jnp.floor((c0 + cnt[1:2, :] + (bm - 1)) * (1.0 / bm)) * bm
    src = lax.broadcasted_iota(jnp.int32, (LANES, LANES), 0)
    dst = lax.broadcasted_iota(jnp.int32, (LANES, LANES), 1)
    pad_end = jnp.dot(padded, jnp.where(src <= dst, 1.0, 0.0).astype(F32), precision=HI,
                      preferred_element_type=F32)
    pad_start = pad_end - padded
    rt = rt_ref[...]
    pos = pos_ref[...]
    tlane = lax.broadcasted_iota(jnp.int32, rt.shape, 1)
    tlane_f = tlane.astype(F32)
    pick = lambda e, table: jnp.sum(jnp.where(tlane_f == e, table[0:1, :], 0.0), axis=-1, keepdims=True)
    d0 = pick(rt[:, 0:1], pad_start) + pos[:, 0:1]
    d1 = pick(rt[:, 1:2], pad_start + c0) + pos[:, 1:2]
    dest_ref[...] = jnp.where(tlane == 0, d0, jnp.where(tlane == 1, d1, 0.0)).astype(jnp.int32)
    nb = blk_ref.shape[0]
    blane = lax.broadcasted_iota(jnp.int32, (nb, LANES), 1)
    start = (lax.broadcasted_iota(jnp.int32, (nb, LANES), 0) * bm).astype(F32)
    ends = jnp.where(blane < N_EXPERTS, pad_end[0:1, :], 3e38)
    expert = jnp.minimum(jnp.sum(jnp.where(ends <= start, 1.0, 0.0), axis=-1, keepdims=True),
                         N_EXPERTS - 1.0)
    total = jnp.max(jnp.where(is_expert, pad_end, 0.0), axis=-1, keepdims=True)[0:1, :]
    used = jnp.where(start < total, 1.0, 0.0)
    blk_ref[...] = jnp.where(blane == 0, expert, jnp.where(blane == 1, used, 0.0)).astype(jnp.int32)


def _slots(route, pos, cnt, n_blocks):
    n = route.shape[0]
    tm = _row_tile(n, 512)
    nbp = -(-n_blocks // 8) * 8
    row = pl.BlockSpec((tm, LANES), lambda i: (i, 0))
    return pl.pallas_call(
        functools.partial(_slots_kernel, MOE_ROWS),
        grid=(n // tm,),
        in_specs=[row, row, pl.BlockSpec((8, LANES), lambda i: (0, 0))],
        out_specs=[row, pl.BlockSpec((nbp, LANES), lambda i: (0, 0))],
        out_shape=[jax.ShapeDtypeStruct((n, LANES), jnp.int32),
                   jax.ShapeDtypeStruct((nbp, LANES), jnp.int32)],
        compiler_params=_cparams(("arbitrary",)),
        name="moe_slots",
    )(route, pos, cnt)


def _row_copy(src, src_row, dst, dst_row, sem):
    return pltpu.make_async_copy(src.at[pl.ds(src_row, 1), :], dst.at[pl.ds(dst_row, 1), :], sem)


def _dispatch_kernel(d0_ref, d1_ref, x_ref, xb_in, xb_ref, sem):
    del xb_in
    tm = x_ref.shape[0]
    base = pl.program_id(0) * tm

    def issue(t, carry):
        _row_copy(x_ref, t, xb_ref, d0_ref[base + t], sem).start()
        _row_copy(x_ref, t, xb_ref, d1_ref[base + t], sem).start()
        return carry

    lax.fori_loop(0, tm, issue, 0, unroll=8)
    for _ in range(TOP_K):
        pltpu.make_async_copy(x_ref, xb_ref.at[pl.ds(0, tm), :], sem).wait()


def _dispatch(h2, dest0, dest1, rows):
    n = h2.shape[0]
    tm = _row_tile(n, 256)
    grid_spec = pltpu.PrefetchScalarGridSpec(
        num_scalar_prefetch=2,
        grid=(n // tm,),
        in_specs=[pl.BlockSpec((tm, D_MODEL), lambda i, d0, d1: (i, 0)),
                  pl.BlockSpec(memory_space=pl.ANY)],
        out_specs=pl.BlockSpec(memory_space=pl.ANY),
        scratch_shapes=[pltpu.SemaphoreType.DMA(())],
    )
    return pl.pallas_call(
        _dispatch_kernel,
        grid_spec=grid_spec,
        out_shape=jax.ShapeDtypeStruct((rows, D_MODEL), F32),
        input_output_aliases={3: 0},
        compiler_params=_cparams(("arbitrary",)),
        name="moe_dispatch",
    )(dest0, dest1, h2, jnp.zeros((rows, D_MODEL), F32))


def _expert_kernel(be_ref, nv_ref, x_ref, wg_ref, wu_ref, wd_ref, y_ref):
    i = pl.program_id(0)
    del be_ref

    @pl.when(nv_ref[i] > 0)
    def _():
        xb = x_ref[...].astype(BF16)
        gate = jnp.dot(xb, wg_ref[...].astype(BF16), preferred_element_type=F32)
        up = jnp.dot(xb, wu_ref[...].astype(BF16), preferred_element_type=F32)
        hdn = gate * _sigmoid(gate) * up
        y_ref[...] = jnp.dot(hdn.astype(BF16), wd_ref[...].astype(BF16),
                             preferred_element_type=F32)

    @pl.when(nv_ref[i] == 0)
    def _():
        y_ref[...] = jnp.zeros(y_ref.shape, F32)


def _experts(xb, blk_e, blk_used, w_gate, w_up, w_down):
    rows = xb.shape[0]
    bm = MOE_ROWS
    wspec = lambda a, b: pl.BlockSpec((None, a, b), lambda i, be, nv: (be[i], 0, 0))
    grid_spec = pltpu.PrefetchScalarGridSpec(
        num_scalar_prefetch=2,
        grid=(rows // bm,),
        in_specs=[pl.BlockSpec((bm, D_MODEL), lambda i, be, nv: (i, 0)),
                  wspec(D_MODEL, D_EXPERT), wspec(D_MODEL, D_EXPERT), wspec(D_EXPERT, D_MODEL)],
        out_specs=pl.BlockSpec((bm, D_MODEL), lambda i, be, nv: (i, 0)),
    )
    return pl.pallas_call(
        _expert_kernel,
        grid_spec=grid_spec,
        out_shape=jax.ShapeDtypeStruct((rows, D_MODEL), F32),
        compiler_params=_cparams(("arbitrary",)),
        name="experts",
    )(blk_e, blk_used, xb, w_gate, w_up, w_down)


def _combine_kernel(d0_ref, d1_ref, x1_ref, rt_ref, gt_ref, yb_ref, o_ref, ya_scr, yb_scr, sem):
    tm = x1_ref.shape[0]
    base = pl.program_id(0) * tm

    def issue(t, carry):
        _row_copy(yb_ref, d0_ref[base + t], ya_scr, t, sem).start()
        _row_copy(yb_ref, d1_ref[base + t], yb_scr, t, sem).start()
        return carry

    lax.fori_loop(0, tm, issue, 0, unroll=8)
    pltpu.make_async_copy(yb_ref.at[pl.ds(0, tm), :], ya_scr, sem).wait()
    pltpu.make_async_copy(yb_ref.at[pl.ds(0, tm), :], yb_scr, sem).wait()
    rt = rt_ref[...]
    moe = rt[:, 2:3] * ya_scr[...] + rt[:, 3:4] * yb_scr[...]
    o_ref[...] = x1_ref[...] + gt_ref[...] * moe


def _combine(x1, route, gt, yb, dest0, dest1):
    n = x1.shape[0]
    tm = _row_tile(n, 256)
    gt_spec = (pl.BlockSpec((1, D_MODEL), lambda i, d0, d1: (0, 0)) if gt.shape[0] == 1
               else pl.BlockSpec((tm, D_MODEL), lambda i, d0, d1: (i, 0)))
    grid_spec = pltpu.PrefetchScalarGridSpec(
        num_scalar_prefetch=2,
        grid=(n // tm,),
        in_specs=[pl.BlockSpec((tm, D_MODEL), lambda i, d0, d1: (i, 0)),
                  pl.BlockSpec((tm, LANES), lambda i, d0, d1: (i, 0)),
                  gt_spec, pl.BlockSpec(memory_space=pl.ANY)],
        out_specs=pl.BlockSpec((tm, D_MODEL), lambda i, d0, d1: (i, 0)),
        scratch_shapes=[pltpu.VMEM((tm, D_MODEL), F32), pltpu.VMEM((tm, D_MODEL), F32),
                        pltpu.SemaphoreType.DMA(())],
    )
    return pl.pallas_call(
        _combine_kernel,
        grid_spec=grid_spec,
        out_shape=jax.ShapeDtypeStruct((n, D_MODEL), F32),
        compiler_params=_cparams(("arbitrary",)),
        name="moe_combine",
    )(dest0, dest1, x1, route, gt, yb)


def _moe(h2, route, x1, gt, w_gate, w_up, w_down):
    n = h2.shape[0]
    bm = MOE_ROWS
    pos, cnt = _rank(route)
    n_blocks = -(-(n * TOP_K) // bm) + N_EXPERTS
    dest, blk = _slots(route, pos, cnt, n_blocks)
    dest0, dest1 = dest[:, 0], dest[:, 1]
    xb = _dispatch(h2, dest0, dest1, n_blocks * bm)
    yb = _experts(xb, blk[:n_blocks, 0], blk[:n_blocks, 1], w_gate, w_up, w_down)
    return _combine(x1, route, gt, yb, dest0, dest1)


def _layer(x, mod, pos, p, w, attend, rwkv):
    sh1, sc1, gt1, sh2, sc2, gt2 = [mod[:, i * D_MODEL:(i + 1) * D_MODEL] for i in range(6)]
    cos, sin = _rope_tables(pos)
    q, k, v, ga, gr, zr, kb, vb = _in_proj(x, sc1, sh1, p['g_mix'], w['w_in'], p['q_gain'],
                                           p['k_gain'], cos, sin)
    o = attend(q, k, v, kb, vb)
    ro, wkv1 = rwkv(zr)
    x1, h2, logits = _merge(x, o, ro, ga, gr, gt1, sc2, sh2, p['g_ffn'], w['w_br_a'], w['w_br_r'],
                            w['w_o'], w['w_router'], w['b_router'])
    y = _moe(h2, logits, x1, gt2, p['w_e_gate'], p['w_e_up'], p['w_e_down'])
    return y, k, v, wkv1, zr


def kernel(x_prompt, x_sample, cache_k, cache_v, state_wkv, state_shift, page_table, c_prompt, c_sample, w_ada, b_ada, g_mix, g_ffn, w_in, q_gain, k_gain, lam_q1, lam_k1, lam_q2, lam_k2, subln_gain, rw_mu, rw_w0, rw_w2, rw_a0, rw_a2, rw_g2, rw_kk, rw_ka, rw_rk, rw_ln_w, rw_ln_b, w_br_a, w_br_r, w_o, w_rg, b_rg, w_re, b_re, w_e_gate, w_e_up, w_e_down):
    assert w_ada.shape[0] == 1, "single-layer kernel"
    B, S, _ = x_prompt.shape
    DB, T, _ = x_sample.shape
    assert B == 1 and T == 1
    past = page_table.shape[1] * PAGE_SIZE
    p = dict(g_mix=g_mix[0], g_ffn=g_ffn[0], q_gain=q_gain[0], k_gain=k_gain[0],
             rw_mu=rw_mu[0], rw_w0=rw_w0[0], rw_w2=rw_w2[0], rw_a0=rw_a0[0], rw_a2=rw_a2[0],
             rw_g2=rw_g2[0], rw_kk=rw_kk[0], rw_ka=rw_ka[0], rw_rk=rw_rk[0],
             rw_ln_w=rw_ln_w[0], rw_ln_b=rw_ln_b[0],
             w_e_gate=w_e_gate[0], w_e_up=w_e_up[0], w_e_down=w_e_down[0])
    pad = ROUTER_PAD - N_GROUPS - N_EXPERTS
    w = dict(w_in=w_in[0].astype(BF16), w_br_a=w_br_a[0].astype(BF16),
             w_br_r=w_br_r[0].astype(BF16), w_o=w_o[0].astype(BF16),
             w_router=jnp.concatenate([w_rg[0], w_re[0], jnp.zeros((D_MODEL, pad), F32)], axis=1),
             b_router=jnp.concatenate([b_rg[0], b_re[0], jnp.zeros((pad,), F32)]).reshape(1, -1))
    lam_rows = [a.reshape(1, A_DH) for a in (lam_q1[0], lam_k1[0], lam_q2[0], lam_k2[0])]

    c_all = jnp.concatenate([c_prompt, jnp.zeros((7, D_MODEL), F32), c_sample], axis=0)
    mod = _ada(c_all, w_ada[0], b_ada[0])
    mod_p, mod_s = mod[0:1], mod[8:8 + DB]

    def rwkv_prompt(zr):
        r, k, v, kk, a, lw, g = _rwkv_prep(zr, jnp.zeros((1, R_IN), F32), True, p)
        y, s1 = _wkv_chunk(r, k, v, kk, a, lw, jnp.zeros((R_HEADS, R_DH, R_DH), F32))
        return _rwkv_post(y, r, k, v, g, p), s1

    def rwkv_sample(zr):
        r, k, v, kk, a, lw, g = _rwkv_prep(zr, state_shift[0], False, p)
        y, s1 = _wkv_step(state_wkv[0], r, k, v, kk, a, lw)
        return _rwkv_post(y, r, k, v, g, p), s1

    attend_p = lambda q, k, v, kb, vb: _attn_prompt(q, kb, vb, lam_rows, subln_gain[0])
    attend_s = lambda q, k, v, kb, vb: _attn_sample(q, k, v, cache_k[0], cache_v[0], page_table,
                                                    lam_rows, subln_gain[0])

    yp, kp, vp, wp, zrp = _layer(x_prompt[0], mod_p, jnp.arange(S), p, w, attend_p, rwkv_prompt)
    ys, ks_, vs_, ws_, zrs = _layer(x_sample[:, 0], mod_s, jnp.full((DB,), past), p, w, attend_s,
                                    rwkv_sample)
    return (yp.reshape(1, S, D_MODEL), ys.reshape(DB, 1, D_MODEL),
            kp.reshape(1, 1, S, A_HEADS, 2 * A_DH), vp.reshape(1, 1, S, A_HEADS, A_DV),
            wp.reshape(1, 1, R_HEADS, R_DH, R_DH), zrp[S - 1:S].reshape(1, 1, R_IN),
            ks_.reshape(1, DB, 1, A_HEADS, 2 * A_DH), vs_.reshape(1, DB, 1, A_HEADS, A_DV),
            ws_.reshape(1, DB, R_HEADS, R_DH, R_DH), zrs.reshape(1, DB, R_IN))
```

```python
import functools
import math

import jax
import jax.numpy as jnp
from jax import lax
from jax.experimental import pallas as pl
from jax.experimental.pallas import tpu as pltpu

F32 = jnp.float32
BF16 = jnp.bfloat16
HI = lax.Precision.HIGHEST

D_MODEL = 1024
PAGE_SIZE = 128
A_DH = 64
A_DV = 2 * A_DH
A_WIDTH = D_MODEL // 2
A_HEADS = A_WIDTH // A_DV
ROPE_THETA = 10000.0
R_DH = 64
R_WIDTH = D_MODEL // 2
R_HEADS = R_WIDTH // R_DH
DECAY_LORA = 64
AAA_LORA = 64
GATE_LORA = 160
GN_EPS = 64e-5
RMS_EPS = 1e-6
R_OFF_K = R_WIDTH
R_OFF_V = 2 * R_WIDTH
R_OFF_W = 3 * R_WIDTH
R_OFF_A = R_OFF_W + DECAY_LORA
R_OFF_G = R_OFF_A + AAA_LORA
R_IN = R_OFF_G + GATE_LORA
COL_K = A_HEADS * 2 * A_DH
COL_V = 2 * COL_K
COL_GA = COL_V + A_WIDTH
COL_GR = COL_GA + D_MODEL
COL_RW = COL_GR + D_MODEL
D_IN = COL_RW + R_IN
N_GROUPS = 4
EXPERTS_PER_GROUP = 8
N_EXPERTS = N_GROUPS * EXPERTS_PER_GROUP
TOP_K = 2
D_EXPERT = 512
LAM_INIT = 0.8 - 0.6 * math.exp(-0.3 * 0)

LANES = 128
ROUTER_PAD = LANES
NEG_BIG = -1e30
VMEM_LIMIT = 56 * 1024 * 1024

WKV_CHUNK = 64
WKV_CHUNKS_PER_STEP = 4
ATTN_Q_BLOCK = 1024
ATTN_K_BLOCK = 1024
ATTN_ROW_CHUNK = 256

_NN = (((1,), (0,)), ((), ()))
_NT = (((1,), (1,)), ((), ()))
_TN = (((0,), (0,)), ((), ()))
SEQS_PER_STEP = 4
PAGES_PER_STEP = 4
PAGE_BUFFERS = 3
MOE_ROWS = 256
MERGE_ROW_CHUNK = 128


def _cparams(sem):
    return pltpu.CompilerParams(dimension_semantics=sem, vmem_limit_bytes=VMEM_LIMIT)


def _row_tile(n, pref):
    t = min(n, pref)
    assert n % t == 0, (n, t)
    return t


def _seg_ones(width, seg, scale=1.0):
    r = lax.broadcasted_iota(jnp.int32, (width, width), 0) // seg
    c = lax.broadcasted_iota(jnp.int32, (width, width), 1) // seg
    return jnp.where(r == c, scale, 0.0).astype(F32)


def _seg_reduce(x, seg):
    hi, lo = _split(x)
    sb = seg.astype(BF16)
    return (jnp.dot(hi, sb, preferred_element_type=F32)
            + jnp.dot(lo, sb, preferred_element_type=F32))


def _sigmoid(x):
    return 1.0 / (1.0 + jnp.exp(-x))


def _ada_kernel(c_ref, w_ref, b_ref, o_ref):
    c = c_ref[...]
    s = c * _sigmoid(c)
    o_ref[...] = jnp.dot(s, w_ref[...], precision=HI, preferred_element_type=F32) + b_ref[...]


def _ada(c, w_ada, b_ada):
    rows = c.shape[0]
    n_out = w_ada.shape[1]
    tn = 1536
    return pl.pallas_call(
        _ada_kernel,
        grid=(n_out // tn,),
        in_specs=[pl.BlockSpec((rows, D_MODEL), lambda j: (0, 0)),
                  pl.BlockSpec((D_MODEL, tn), lambda j: (0, j)),
                  pl.BlockSpec((1, tn), lambda j: (0, j))],
        out_specs=pl.BlockSpec((rows, tn), lambda j: (0, j)),
        out_shape=jax.ShapeDtypeStruct((rows, n_out), F32),
        compiler_params=_cparams(("arbitrary",)),
        name="ada",
    )(c, w_ada, b_ada.reshape(1, n_out))


def _mod_spec(rows, tm):
    if rows == 1:
        return pl.BlockSpec((1, D_MODEL), lambda i: (0, 0))
    return pl.BlockSpec((tm, D_MODEL), lambda i: (i, 0))


def _inproj_kernel(x_ref, sc_ref, sh_ref, g_ref, w_ref, qg_ref, kg_ref, cos_ref, sin_ref,
                   q_ref, k_ref, v_ref, ga_ref, gr_ref, zr_ref, kb_ref, vb_ref):
    x = x_ref[...]
    tm = x.shape[0]
    ms = jnp.mean(x * x, axis=-1, keepdims=True)
    h = x * lax.rsqrt(ms + RMS_EPS) * g_ref[...]
    h = h * (1.0 + sc_ref[...]) + sh_ref[...]
    hb = h.astype(BF16)

    def sec(a, b):
        return jnp.dot(hb, w_ref[:, a:b], preferred_element_type=F32)

    seg_mean = _seg_ones(LANES, A_DH, 1.0 / A_DH)
    cos = cos_ref[...]
    sin = sin_ref[...]
    lane = lax.broadcasted_iota(jnp.int32, cos.shape, 1)
    first_half = (lane % A_DH) < (A_DH // 2)

    def norm_rope(z, gain):
        m = _seg_reduce(z * z, seg_mean)
        zn = z * lax.rsqrt(m + RMS_EPS) * gain
        swapped = jnp.where(first_half, pltpu.roll(zn, LANES - A_DH // 2, 1),
                            pltpu.roll(zn, A_DH // 2, 1))
        return zn * cos + swapped * sin

    zq = sec(0, COL_K)
    zk = sec(COL_K, COL_V)
    for hd in range(A_HEADS):
        sl = slice(hd * LANES, (hd + 1) * LANES)
        q_ref[:, sl] = norm_rope(zq[:, sl], qg_ref[...])
        kh = norm_rope(zk[:, sl], kg_ref[...])
        k_ref[pl.ds(hd, tm, stride=A_HEADS), :] = kh
        kb_ref[:, sl] = kh.astype(BF16)
    v = sec(COL_V, COL_GA)
    for hd in range(A_HEADS):
        v_ref[pl.ds(hd, tm, stride=A_HEADS), :] = v[:, hd * LANES:(hd + 1) * LANES]
    vb_ref[...] = v.astype(BF16)
    ga_ref[...] = _sigmoid(sec(COL_GA, COL_GR))
    gr_ref[...] = _sigmoid(sec(COL_GR, COL_RW))
    zr_ref[...] = sec(COL_RW, D_IN)


def _in_proj(x, sc, sh, g_mix, w_in_bf, q_gain, k_gain, cos, sin):
    n = x.shape[0]
    tm = _row_tile(n, 256)
    row = lambda w: pl.BlockSpec((tm, w), lambda i: (i, 0))
    const = lambda r, w: pl.BlockSpec((r, w), lambda i: (0, 0))
    gain2 = lambda g: jnp.tile(g.reshape(1, A_DH), (1, 2))
    f32_out = lambda w: (row(w), jax.ShapeDtypeStruct((n, w), F32))
    by_head = (pl.BlockSpec((tm * A_HEADS, A_DV), lambda i: (i, 0)),
               jax.ShapeDtypeStruct((n * A_HEADS, A_DV), F32))
    bf_out = (row(A_WIDTH), jax.ShapeDtypeStruct((n, A_WIDTH), BF16))
    outs = [f32_out(A_WIDTH), by_head, by_head, f32_out(D_MODEL), f32_out(D_MODEL), f32_out(R_IN),
            bf_out, bf_out]
    return pl.pallas_call(
        _inproj_kernel,
        grid=(n // tm,),
        in_specs=[row(D_MODEL), _mod_spec(sc.shape[0], tm), _mod_spec(sh.shape[0], tm),
                  const(1, D_MODEL), const(D_MODEL, D_IN), const(1, LANES), const(1, LANES),
                  row(LANES), row(LANES)],
        out_specs=[o[0] for o in outs],
        out_shape=[o[1] for o in outs],
        compiler_params=_cparams(("arbitrary",)),
        name="in_proj",
    )(x, sc, sh, g_mix.reshape(1, D_MODEL), w_in_bf, gain2(q_gain), gain2(k_gain), cos, sin)


def _rope_tables(pos):
    half = A_DH // 2
    inv = ROPE_THETA ** (-jnp.arange(half, dtype=F32) / half)
    lane = jnp.arange(LANES)
    inv_l = inv[lane % half]
    sign = jnp.where((lane % A_DH) < half, -1.0, 1.0).astype(F32)
    ang = pos.astype(F32)[:, None] * inv_l[None, :]
    return jnp.cos(ang), jnp.sin(ang) * sign[None, :]


def _lambda(lq1, lk1, lq2, lk2):
    s1 = jnp.sum(lq1 * lk1, axis=-1, keepdims=True)
    s2 = jnp.sum(lq2 * lk2, axis=-1, keepdims=True)
    return jnp.exp(s1) - jnp.exp(s2) + LAM_INIT


def _subln(o, gain):
    ms = jnp.mean(o * o, axis=-1, keepdims=True)
    return o * lax.rsqrt(ms + RMS_EPS) * gain * (1.0 - LAM_INIT)


def _attn_prompt_kernel(bk, q_ref, k_ref, v_ref, lq1_ref, lk1_ref, lq2_ref, lk2_ref, gain_ref,
                        o_ref, qs_scr, m_scr, acc_scr):
    i = pl.program_id(1)
    bq = q_ref.shape[0]
    rc = min(bq, ATTN_ROW_CHUNK)
    q = q_ref[...] * (A_DH ** -0.5 * math.log2(math.e))
    lane = lax.broadcasted_iota(jnp.int32, q.shape, 1)
    qs_scr[0:bq, :] = jnp.where(lane < A_DH, q, 0.0).astype(BF16)
    qs_scr[bq:2 * bq, :] = jnp.where(lane >= A_DH, q, 0.0).astype(BF16)
    m_scr[...] = jnp.full(m_scr.shape, NEG_BIG, F32)
    acc_scr[...] = jnp.zeros(acc_scr.shape, F32)
    ones = jnp.ones((bk, LANES), BF16)

    def update(start, mask_offset):
        kb = k_ref[pl.ds(start, bk), :]
        vx = jnp.concatenate([v_ref[pl.ds(start, bk), :], ones], axis=1)
        for c in range(2 * bq // rc):
            rows = slice(c * rc, (c + 1) * rc)
            s = lax.dot_general(qs_scr[rows, :], kb, _NT, preferred_element_type=F32)
            if mask_offset is not None:
                row = lax.broadcasted_iota(jnp.int32, (rc, bk), 0) + (c * rc) % bq
                col = lax.broadcasted_iota(jnp.int32, (rc, bk), 1) + mask_offset
                s = jnp.where(col <= row, s, NEG_BIG)
            m_prev = m_scr[rows, :]
            m_new = jnp.maximum(m_prev, jnp.max(s, axis=-1, keepdims=True))
            pr = jnp.exp2((s - jnp.tile(m_new, (1, bk // LANES))).astype(BF16))
            alpha = jnp.exp2(m_prev - m_new)
            acc_scr[rows, :] = jnp.tile(alpha, (1, 2)) * acc_scr[rows, :] + jnp.dot(
                pr, vx, preferred_element_type=F32)
            m_scr[rows, :] = m_new

    n_below = i * (bq // bk)

    def below_diagonal(j, carry):
        update(pl.multiple_of(2 * j * bk, bk), None)
        update(pl.multiple_of((2 * j + 1) * bk, bk), None)
        return carry

    lax.fori_loop(0, n_below // 2, below_diagonal, 0)

    @pl.when(n_below % 2 == 1)
    def _():
        update(pl.multiple_of((n_below - 1) * bk, bk), None)

    for jj in range(bq // bk):
        update(pl.multiple_of(i * bq + jj * bk, bk), jj * bk)
    acc = acc_scr[...]
    d = acc[:, 0:LANES] / acc[:, LANES:2 * LANES]
    lam = _lambda(lq1_ref[...], lk1_ref[...], lq2_ref[...], lk2_ref[...])
    o_ref[...] = _subln(d[0:bq, :] - lam * d[bq:2 * bq, :], gain_ref[...])


def _attn_prompt(q, kb, vb, lam_rows, subln_gain):
    n = q.shape[0]
    bq = _row_tile(n, ATTN_Q_BLOCK)
    bk = _row_tile(bq, ATTN_K_BLOCK)
    const = lambda w: pl.BlockSpec((1, w), lambda h, i: (0, 0))
    head = pl.BlockSpec((n, LANES), lambda h, i: (0, h))
    return pl.pallas_call(
        functools.partial(_attn_prompt_kernel, bk),
        grid=(A_HEADS, n // bq),
        in_specs=[pl.BlockSpec((bq, LANES), lambda h, i: (i, h)), head, head,
                  const(A_DH), const(A_DH), const(A_DH), const(A_DH), const(A_DV)],
        out_specs=pl.BlockSpec((bq, LANES), lambda h, i: (i, h)),
        out_shape=jax.ShapeDtypeStruct((n, A_WIDTH), F32),
        scratch_shapes=[pltpu.VMEM((2 * bq, LANES), BF16), pltpu.VMEM((2 * bq, LANES), F32),
                        pltpu.VMEM((2 * bq, 2 * LANES), F32)],
        compiler_params=_cparams(("arbitrary", "arbitrary")),
        name="attn_prompt",
    )(q, kb, vb, *lam_rows, subln_gain.reshape(1, A_DV))


def _attn_sample_kernel(*refs):
    ns, pps, nbuf = SEQS_PER_STEP, PAGES_PER_STEP, PAGE_BUFFERS
    (pt_ref, q_ref, kn_ref, vn_ref, lq1_ref, lk1_ref, lq2_ref, lk2_ref, gain_ref, ck_ref, cv_ref,
     o_ref, kbuf, vbuf, sems, m_scr, l_scr, acc_scr) = refs
    g = pl.program_id(1)
    ng = pl.num_programs(1)
    step = pl.program_id(0) * ng + g
    total = pl.num_programs(0) * ng

    def page_copies(step_idx, slot):
        bb = step_idx // ng
        gg = step_idx - bb * ng
        copies = []
        for u in range(ns):
            for t in range(pps):
                page = pt_ref[(bb * ns + u) * (ng * pps) + gg * pps + t]
                j = u * pps + t
                copies.append(pltpu.make_async_copy(ck_ref.at[page], kbuf.at[slot, j], sems.at[0, slot]))
                copies.append(pltpu.make_async_copy(cv_ref.at[page], vbuf.at[slot, j], sems.at[1, slot]))
        return copies

    @pl.when(step == 0)
    def _():
        for d in range(nbuf - 1):
            for c in page_copies(d, d):
                c.start()

    ahead = step + (nbuf - 1)

    @pl.when(ahead < total)
    def _():
        for c in page_copies(ahead, ahead % nbuf):
            c.start()

    slot = step % nbuf
    for c in page_copies(step, slot):
        c.wait()
    k_refs = [kbuf.at[slot, j] for j in range(ns * pps)]
    v_refs = [vbuf.at[slot, j] for j in range(ns * pps)]
    rows = 2 * A_HEADS
    page_rows = PAGE_SIZE * A_HEADS
    by_head = lambda x: jnp.concatenate(
        [jnp.broadcast_to(x[:, hd * A_DV:(hd + 1) * A_DV], (2, A_DV)) for hd in range(A_HEADS)],
        axis=0)
    rid = lax.broadcasted_iota(jnp.int32, (rows, A_DV), 0)
    comp = lax.broadcasted_iota(jnp.int32, (rows, A_DV), 1) // A_DH
    srow = lax.broadcasted_iota(jnp.int32, (rows, pps * page_rows), 0) // 2
    scol = lax.broadcasted_iota(jnp.int32, (rows, pps * page_rows), 1) % A_HEADS
    lam = _lambda(lq1_ref[...], lk1_ref[...], lq2_ref[...], lk2_ref[...])

    qms = [jnp.where(rid % 2 == comp, by_head(q_ref[u] * (A_DH ** -0.5)), 0.0) for u in range(ns)]

    @pl.when(g == 0)
    def _():
        for u in range(ns):
            m_scr[u] = jnp.sum(qms[u] * by_head(kn_ref[u]), axis=-1, keepdims=True)
            l_scr[u] = jnp.ones((rows, 1), F32)
            acc_scr[u] = by_head(vn_ref[u])

    for u in range(ns):
        qb = qms[u].astype(BF16)
        s = jnp.concatenate(
            [lax.dot_general(qb, k_refs[u * pps + t][...].astype(BF16), _NT,
                             preferred_element_type=F32) for t in range(pps)], axis=1)
        s = jnp.where(srow == scol, s, NEG_BIG)
        m_prev = m_scr[u]
        m_new = jnp.maximum(m_prev, jnp.max(s, axis=-1, keepdims=True))
        alpha = jnp.exp(m_prev - m_new)
        pr = jnp.exp(s - m_new)
        l_scr[u] = alpha * l_scr[u] + jnp.sum(pr, axis=-1, keepdims=True)
        prb = pr.astype(BF16)
        pv = jnp.dot(prb[:, 0:page_rows], v_refs[u * pps][...].astype(BF16),
                     preferred_element_type=F32)
        for t in range(1, pps):
            pv = pv + jnp.dot(prb[:, t * page_rows:(t + 1) * page_rows],
                              v_refs[u * pps + t][...].astype(BF16), preferred_element_type=F32)
        acc_scr[u] = alpha * acc_scr[u] + pv
        m_scr[u] = m_new

    @pl.when(g == pl.num_programs(1) - 1)
    def _():
        for u in range(ns):
            d = acc_scr[u] / l_scr[u]
            for hd in range(A_HEADS):
                o = d[2 * hd:2 * hd + 1, :] - lam * d[2 * hd + 1:2 * hd + 2, :]
                o_ref[u, :, hd * A_DV:(hd + 1) * A_DV] = _subln(o, gain_ref[...])


def _attn_sample(q, k_new, v_new, cache_k, cache_v, page_table, lam_rows, subln_gain):
    nb, n_pages = page_table.shape
    ns, pps, nbuf = SEQS_PER_STEP, PAGES_PER_STEP, PAGE_BUFFERS
    assert n_pages % pps == 0 and nb % ns == 0
    assert (nb // ns) * (n_pages // pps) >= nbuf - 1
    page_rows = PAGE_SIZE * A_HEADS
    ck = cache_k.reshape(cache_k.shape[0], page_rows, A_DV)
    cv = cache_v.reshape(cache_v.shape[0], page_rows, A_DV)
    pt = page_table.reshape(-1)
    tok = pl.BlockSpec((ns, 1, A_WIDTH), lambda b, g, pt: (b, 0, 0))
    const = lambda w: pl.BlockSpec((1, w), lambda b, g, pt: (0, 0))
    hbm = pl.BlockSpec(memory_space=pl.ANY)
    grid_spec = pltpu.PrefetchScalarGridSpec(
        num_scalar_prefetch=1,
        grid=(nb // ns, n_pages // pps),
        in_specs=[tok, tok, tok, const(A_DH), const(A_DH), const(A_DH), const(A_DH), const(A_DV),
                  hbm, hbm],
        out_specs=tok,
        scratch_shapes=[pltpu.VMEM((nbuf, ns * pps, page_rows, A_DV), F32),
                        pltpu.VMEM((nbuf, ns * pps, page_rows, A_DV), F32),
                        pltpu.SemaphoreType.DMA((2, nbuf)),
                        pltpu.VMEM((ns, 2 * A_HEADS, 1), F32), pltpu.VMEM((ns, 2 * A_HEADS, 1), F32),
                        pltpu.VMEM((ns, 2 * A_HEADS, A_DV), F32)],
    )
    tok3 = lambda a: a.reshape(nb, 1, A_WIDTH)
    out = pl.pallas_call(
        _attn_sample_kernel,
        grid_spec=grid_spec,
        out_shape=jax.ShapeDtypeStruct((nb, 1, A_WIDTH), F32),
        compiler_params=_cparams(("arbitrary", "arbitrary")),
        name="attn_sample",
    )(pt, tok3(q), tok3(k_new), tok3(v_new), *lam_rows, subln_gain.reshape(1, A_DV), ck, cv)
    return out.reshape(nb, A_WIDTH)


def _rwkv_prep_kernel(seq_mode, zr_ref, prev_ref, mu_ref, w0_ref, w2_ref, a0_ref, a2_ref, g2_ref,
                      kkp_ref, ka_ref, r_ref, k_ref, v_ref, kk_ref, a_ref, lw_ref, g_ref, *scr):
    z = zr_ref[...]
    if seq_mode:
        (carry,) = scr

        @pl.when(pl.program_id(0) == 0)
        def _():
            carry[...] = prev_ref[...]

        row = lax.broadcasted_iota(jnp.int32, z.shape, 0)
        zp = jnp.where(row == 0, carry[...], pltpu.roll(z, 1, 0))
        carry[...] = z[z.shape[0] - 1:z.shape[0], :]
    else:
        zp = prev_ref[...]
    zs = z + (zp - z) * mu_ref[...]
    r = zs[:, 0:R_OFF_K]
    k = zs[:, R_OFF_K:R_OFF_V]
    v = zs[:, R_OFF_V:R_OFF_W]
    zw = zs[:, R_OFF_W:R_OFF_A]
    za = zs[:, R_OFF_A:R_OFF_G]
    zg = zs[:, R_OFF_G:R_IN]
    w_pre = w0_ref[...] + _dot3(jnp.tanh(zw), w2_ref[...], _NN)
    nx = -w_pre
    softplus = jnp.maximum(nx, 0.0) + jnp.log(1.0 + jnp.exp(-jnp.abs(nx)))
    lw_ref[...] = -jnp.exp(-softplus - 0.5)
    a = _sigmoid(a0_ref[...] + _dot3(za, a2_ref[...], _NN))
    g_ref[...] = _mm1(_sigmoid(zg), g2_ref[...])
    kkr = k * kkp_ref[...]
    seg_sum = _seg_ones(LANES, R_DH)
    for sb in range(R_WIDTH // LANES):
        sl = slice(sb * LANES, (sb + 1) * LANES)
        x = kkr[:, sl]
        ss = _seg_reduce(x * x, seg_sum)
        kk_ref[:, sl] = x / jnp.maximum(jnp.sqrt(ss), 1e-12)
    r_ref[...] = r
    v_ref[...] = v
    a_ref[...] = a
    k_ref[...] = k * (1.0 + (a - 1.0) * ka_ref[...])


def _rwkv_prep(zr, prev, seq_mode, p):
    n = zr.shape[0]
    tm = _row_tile(n, 256)
    row = lambda w: pl.BlockSpec((tm, w), lambda i: (i, 0))
    const = lambda r, w: pl.BlockSpec((r, w), lambda i: (0, 0))
    prev_spec = const(1, R_IN) if seq_mode else row(R_IN)
    vec = lambda a: a.reshape(1, -1)
    return pl.pallas_call(
        functools.partial(_rwkv_prep_kernel, seq_mode),
        grid=(n // tm,),
        in_specs=[row(R_IN), prev_spec, const(1, R_IN), const(1, R_WIDTH),
                  const(DECAY_LORA, R_WIDTH), const(1, R_WIDTH), const(AAA_LORA, R_WIDTH),
                  const(GATE_LORA, R_WIDTH), const(1, R_WIDTH), const(1, R_WIDTH)],
        out_specs=[row(R_WIDTH)] * 7,
        out_shape=[jax.ShapeDtypeStruct((n, R_WIDTH), F32)] * 7,
        scratch_shapes=[pltpu.VMEM((1, R_IN), F32)] if seq_mode else [],
        compiler_params=_cparams(("arbitrary",)),
        name="rwkv_prep_seq" if seq_mode else "rwkv_prep_batch",
    )(zr, prev, vec(p['rw_mu']), vec(p['rw_w0']), p['rw_w2'], vec(p['rw_a0']), p['rw_a2'],
      p['rw_g2'], vec(p['rw_kk']), vec(p['rw_ka']))


def _split(x):
    hi = x.astype(BF16)
    return hi, (x - hi.astype(F32)).astype(BF16)


def _dot3(a, b, dims):
    ah, al = _split(a)
    bh, bl = _split(b)
    d = lambda x, y: lax.dot_general(x, y, dims, preferred_element_type=F32)
    return d(ah, bh) + (d(ah, bl) + d(al, bh))


def _mm1(a, b):
    return jnp.dot(a.astype(BF16), b.astype(BF16), preferred_element_type=F32)


def _mm(a, b):
    return _dot3(a, b, _NN)


def _mm_nt(a, b):
    return _dot3(a, b, _NT)


def _mm_tn(a, b):
    return _dot3(a, b, _TN)


def _wkv_chunk_kernel(nch, r_ref, k_ref, v_ref, kk_ref, a_ref, lw_ref, s0_ref, y_ref, s_ref):
    rows = r_ref.shape[0]
    c = rows // nch

    @pl.when(pl.program_id(0) == 0)
    def _():
        s_ref[...] = s0_ref[...]

    ti = lax.broadcasted_iota(jnp.int32, (c, c), 0)
    si = lax.broadcasted_iota(jnp.int32, (c, c), 1)
    lower = si <= ti
    strict = si < ti
    lw = lw_ref[...]
    bt_i = lax.broadcasted_iota(jnp.int32, (rows, rows), 0)
    bs_i = lax.broadcasted_iota(jnp.int32, (rows, rows), 1)
    same_chunk_lower = jnp.logical_and(bs_i <= bt_i, bs_i // c == bt_i // c)
    cs = jnp.dot(jnp.where(same_chunk_lower, 1.0, 0.0).astype(F32), lw, precision=HI,
                 preferred_element_type=F32)
    chunk_rows = [slice(ci * c, (ci + 1) * c) for ci in range(nch)]
    total = jnp.concatenate(
        [jnp.broadcast_to(cs[(ci + 1) * c - 1:(ci + 1) * c, :], (c, R_WIDTH)) for ci in range(nch)],
        axis=0)
    e_pos = jnp.exp(cs)
    e_prev = jnp.exp(cs - lw)
    e_neg = jnp.exp(-cs)
    e_rem = jnp.exp(total - cs)
    e_tot = jnp.exp(total)
    kk = kk_ref[...]
    k = k_ref[...]
    b = kk * a_ref[...]
    at_all = kk * e_prev
    bt_all = b * e_neg
    kt_all = k * e_neg
    rt_all = r_ref[...] * e_pos
    bh_all = b * e_rem
    kh_all = k * e_rem
    v_all = v_ref[...]
    eye = jnp.where(si == ti, 1.0, 0.0).astype(F32)

    pairs = [(ci, h) for ci in range(nch) for h in range(R_HEADS)]
    heads = range(len(pairs))
    sub = lambda x, p: x[chunk_rows[p[0]], p[1] * R_DH:(p[1] + 1) * R_DH]
    at = [sub(at_all, p) for p in pairs]
    rt = [sub(rt_all, p) for p in pairs]
    v = [sub(v_all, p) for p in pairs]
    a4 = [_mm_nt(jnp.concatenate([at[h], rt[h]], axis=0),
                 jnp.concatenate([sub(bt_all, pairs[h]), sub(kt_all, pairs[h])], axis=0))
          for h in heads]
    aak = [jnp.where(strict, x[0:c, c:2 * c], 0.0) for x in a4]
    arb = [jnp.where(lower, x[c:2 * c, 0:c], 0.0) for x in a4]
    ark = [jnp.where(lower, x[c:2 * c, c:2 * c], 0.0) for x in a4]
    nl = [jnp.where(strict, -x[0:c, 0:c], 0.0) for x in a4]
    inv = [eye + x for x in nl]
    pw = [_mm(x, x) for x in nl]
    span = 2
    while span < c:
        if 2 * span < c:
            both = [_mm1(jnp.concatenate([inv[h], pw[h]], axis=0), pw[h]) for h in heads]
            inv = [inv[h] + both[h][0:c, :] for h in heads]
            pw = [x[c:2 * c, :] for x in both]
        else:
            inv = [inv[h] + _mm1(inv[h], pw[h]) for h in heads]
        span *= 2
    av = [_mm(jnp.concatenate([aak[h], ark[h]], axis=0), v[h]) for h in heads]
    tw = [_mm(inv[h], jnp.concatenate([at[h], av[h][0:c, :]], axis=1)) for h in heads]
    kv = [_mm_tn(v[h], sub(kh_all, pairs[h])) for h in heads]
    state = [s_ref[hd] for hd in range(R_HEADS)]
    for ci in range(nch):
        idx = [ci * R_HEADS + hd for hd in range(R_HEADS)]
        hs = [_mm_nt(jnp.concatenate([-tw[h][:, 0:R_DH], rt[h]], axis=0), state[h % R_HEADS])
              for h in idx]
        u = [hs[j][0:c, :] - tw[h][:, R_DH:2 * R_DH] for j, h in enumerate(idx)]
        au = [_mm(arb[h], u[j]) for j, h in enumerate(idx)]
        ub = [_mm_tn(u[j], sub(bh_all, pairs[h])) for j, h in enumerate(idx)]
        for j, h in enumerate(idx):
            p = pairs[h]
            lanes = slice(p[1] * R_DH, (p[1] + 1) * R_DH)
            y_ref[chunk_rows[ci], lanes] = hs[j][c:2 * c, :] + au[j] + av[h][c:2 * c, :]
            state[j] = state[j] * e_tot[ci * c:ci * c + 1, lanes] + ub[j] + kv[h]
    for hd in range(R_HEADS):
        s_ref[hd] = state[hd]


def _wkv_chunk(r, k, v, kk, a, lw, s0):
    n = r.shape[0]
    c = _row_tile(n, WKV_CHUNK)
    nch = WKV_CHUNKS_PER_STEP if n % (c * WKV_CHUNKS_PER_STEP) == 0 else 1
    row = pl.BlockSpec((c * nch, R_WIDTH), lambda i: (i, 0))
    st = pl.BlockSpec((R_HEADS, R_DH, R_DH), lambda i: (0, 0, 0))
    return pl.pallas_call(
        functools.partial(_wkv_chunk_kernel, nch),
        grid=(n // (c * nch),),
        in_specs=[row] * 6 + [st],
        out_specs=[row, st],
        out_shape=[jax.ShapeDtypeStruct((n, R_WIDTH), F32),
                   jax.ShapeDtypeStruct((R_HEADS, R_DH, R_DH), F32)],
        compiler_params=_cparams(("arbitrary",)),
        name="wkv_chunk",
    )(r, k, v, kk, a, lw, s0)


def _wkv_step_kernel(s_ref, r_ref, k_ref, v_ref, kk_ref, a_ref, lw_ref, y_ref, so_ref):
    s = s_ref[...]
    kk = kk_ref[...]
    sa = -jnp.sum(s * kk, axis=-1, keepdims=True)
    s2 = s * jnp.exp(lw_ref[...]) + sa * (kk * a_ref[...]) + v_ref[...] * k_ref[...]
    so_ref[...] = s2
    y_ref[...] = jnp.sum(s2 * r_ref[...], axis=-1, keepdims=True)


def _wkv_step(state, r, k, v, kk, a, lw):
    nb = state.shape[0]
    bs = _row_tile(nb, 8)
    rowv = lambda x: x.reshape(nb, R_HEADS, 1, R_DH)
    st = pl.BlockSpec((bs, R_HEADS, R_DH, R_DH), lambda i: (i, 0, 0, 0))
    rw = pl.BlockSpec((bs, R_HEADS, 1, R_DH), lambda i: (i, 0, 0, 0))
    col = pl.BlockSpec((bs, R_HEADS, R_DH, 1), lambda i: (i, 0, 0, 0))
    y, s_new = pl.pallas_call(
        _wkv_step_kernel,
        grid=(nb // bs,),
        in_specs=[st, rw, rw, col, rw, rw, rw],
        out_specs=[col, st],
        out_shape=[jax.ShapeDtypeStruct((nb, R_HEADS, R_DH, 1), F32),
                   jax.ShapeDtypeStruct(state.shape, F32)],
        compiler_params=_cparams(("arbitrary",)),
        name="wkv_step",
    )(state, rowv(r), rowv(k), v.reshape(nb, R_HEADS, R_DH, 1), rowv(kk), rowv(a), rowv(lw))
    return y.reshape(nb, R_WIDTH), s_new


def _rwkv_post_kernel(y_ref, r_ref, k_ref, v_ref, g_ref, lnw_ref, lnb_ref, rk_ref, o_ref):
    seg_mean = _seg_ones(LANES, R_DH, 1.0 / R_DH)
    seg_sum = _seg_ones(LANES, R_DH)
    for sb in range(R_WIDTH // LANES):
        sl = slice(sb * LANES, (sb + 1) * LANES)
        y = y_ref[:, sl]
        mean = _seg_reduce(y, seg_mean)
        d = y - mean
        var = _seg_reduce(d * d, seg_mean)
        yn = d * lax.rsqrt(var + GN_EPS) * lnw_ref[:, sl] + lnb_ref[:, sl]
        bonus = _seg_reduce(r_ref[:, sl] * k_ref[:, sl] * rk_ref[:, sl], seg_sum)
        o_ref[:, sl] = (yn + bonus * v_ref[:, sl]) * g_ref[:, sl]


def _rwkv_post(y, r, k, v, g, p):
    n = y.shape[0]
    tm = _row_tile(n, 512)
    row = pl.BlockSpec((tm, R_WIDTH), lambda i: (i, 0))
    const = pl.BlockSpec((1, R_WIDTH), lambda i: (0, 0))
    vec = lambda a: a.reshape(1, R_WIDTH)
    return pl.pallas_call(
        _rwkv_post_kernel,
        grid=(n // tm,),
        in_specs=[row] * 5 + [const] * 3,
        out_specs=row,
        out_shape=jax.ShapeDtypeStruct((n, R_WIDTH), F32),
        compiler_params=_cparams(("arbitrary",)),
        name="rwkv_post",
    )(y, r, k, v, g, vec(p['rw_ln_w']), vec(p['rw_ln_b']), vec(p['rw_rk']))


def _merge_kernel(x_ref, o_ref, ro_ref, ga_ref, gr_ref, gt_ref, sc_ref, sh_ref, g_ref,
                  wa_ref, wr_ref, wo_ref, wrt_ref, brt_ref, x1_ref, h2_ref, rt_ref):
    tm = x_ref.shape[0]
    rc = min(tm, MERGE_ROW_CHUNK)
    mod = lambda ref, sl: ref[...] if ref.shape[0] == 1 else ref[sl, :]
    for c in range(tm // rc):
        sl = slice(c * rc, (c + 1) * rc)
        ma = jnp.dot(o_ref[sl, :].astype(BF16), wa_ref[...], preferred_element_type=F32)
        mr = jnp.dot(ro_ref[sl, :].astype(BF16), wr_ref[...], preferred_element_type=F32)
        mg = ga_ref[sl, :] * ma + gr_ref[sl, :] * mr
        merged = jnp.dot(mg.astype(BF16), wo_ref[...], preferred_element_type=F32)
        x1 = x_ref[sl, :] + mod(gt_ref, sl) * merged
        x1_ref[sl, :] = x1
        ms = jnp.mean(x1 * x1, axis=-1, keepdims=True)
        h2 = x1 * lax.rsqrt(ms + RMS_EPS) * g_ref[...]
        h2 = h2 * (1.0 + mod(sc_ref, sl)) + mod(sh_ref, sl)
        h2_ref[sl, :] = h2
        logits = _dot3(h2, wrt_ref[...], _NN) + brt_ref[...]
        rt_ref[sl, :] = _route(logits)


def _route(logits):
    lane = lax.broadcasted_iota(jnp.int32, logits.shape, 1)
    lane_f = lane.astype(F32)
    first_max = lambda x, m: jnp.min(jnp.where(x == m, lane_f, float(LANES)), axis=-1, keepdims=True)
    is_g = lane < N_GROUPS
    lg = jnp.where(is_g, logits, NEG_BIG)
    gmax = jnp.max(lg, axis=-1, keepdims=True)
    g_idx = first_max(lg, gmax)
    sum_g = jnp.sum(jnp.where(is_g, jnp.exp(lg - gmax), 0.0), axis=-1, keepdims=True)
    group_of_lane = ((lane - N_GROUPS) // EXPERTS_PER_GROUP).astype(F32)
    in_group = jnp.where(lane >= N_GROUPS, group_of_lane, -1.0) == g_idx
    le = jnp.where(in_group, logits, NEG_BIG)
    m1 = jnp.max(le, axis=-1, keepdims=True)
    i1 = first_max(le, m1)
    le2 = jnp.where(lane_f == i1, NEG_BIG, le)
    m2 = jnp.max(le2, axis=-1, keepdims=True)
    i2 = first_max(le2, m2)
    t = jnp.exp(m2 - m1)
    w1 = 1.0 / (sum_g * (1.0 + t))
    out = jnp.where(lane == 0, i1 - N_GROUPS, 0.0)
    out = jnp.where(lane == 1, i2 - N_GROUPS, out)
    out = jnp.where(lane == 2, w1, out)
    return jnp.where(lane == 3, w1 * t, out)


def _merge(x, o, ro, ga, gr, gt, sc, sh, g_ffn, wa_bf, wr_bf, wo_bf, w_router, b_router):
    n = x.shape[0]
    tm = _row_tile(n, 512)
    row = lambda w: pl.BlockSpec((tm, w), lambda i: (i, 0))
    const = lambda r, w: pl.BlockSpec((r, w), lambda i: (0, 0))
    mod = lambda a: _mod_spec(a.shape[0], tm)
    return pl.pallas_call(
        _merge_kernel,
        grid=(n // tm,),
        in_specs=[row(D_MODEL), row(A_WIDTH), row(R_WIDTH), row(D_MODEL), row(D_MODEL),
                  mod(gt), mod(sc), mod(sh), const(1, D_MODEL),
                  const(A_WIDTH, D_MODEL), const(R_WIDTH, D_MODEL), const(D_MODEL, D_MODEL),
                  const(D_MODEL, ROUTER_PAD), const(1, ROUTER_PAD)],
        out_specs=[row(D_MODEL), row(D_MODEL), row(ROUTER_PAD)],
        out_shape=[jax.ShapeDtypeStruct((n, D_MODEL), F32), jax.ShapeDtypeStruct((n, D_MODEL), F32),
                   jax.ShapeDtypeStruct((n, ROUTER_PAD), F32)],
        compiler_params=_cparams(("arbitrary",)),
        name="merge",
    )(x, o, ro, ga, gr, gt, sc, sh, g_ffn.reshape(1, D_MODEL), wa_bf, wr_bf, wo_bf,
      w_router, b_router)


def _rank_kernel(rt_ref, pos_ref, cnt_ref, carry):
    tm = rt_ref.shape[0]

    @pl.when(pl.program_id(0) == 0)
    def _():
        carry[...] = jnp.zeros(carry.shape, F32)

    rt = rt_ref[...]
    lane = lax.broadcasted_iota(jnp.int32, rt.shape, 1)
    lane_f = lane.astype(F32)
    oh0 = jnp.where(lane_f == rt[:, 0:1], 1.0, 0.0)
    oh1 = jnp.where(lane_f == rt[:, 1:2], 1.0, 0.0)
    ti = lax.broadcasted_iota(jnp.int32, (tm, tm), 0)
    si = lax.broadcasted_iota(jnp.int32, (tm, tm), 1)
    earlier = jnp.where(si < ti, 1.0, 0.0).astype(BF16)
    pre = jnp.dot(earlier, jnp.concatenate([oh0, oh1], axis=1).astype(BF16),
                  preferred_element_type=F32)
    c = carry[...]
    rank0 = jnp.sum(oh0 * (pre[:, 0:LANES] + c[0:1, :]), axis=-1, keepdims=True)
    rank1 = jnp.sum(oh1 * (pre[:, LANES:2 * LANES] + c[1:2, :]), axis=-1, keepdims=True)
    pos_ref[...] = jnp.where(lane == 0, rank0, jnp.where(lane == 1, rank1, 0.0))
    row = lax.broadcasted_iota(jnp.int32, c.shape, 0)
    c = c + jnp.where(row == 0, jnp.sum(oh0, axis=0, keepdims=True), 0.0) \
          + jnp.where(row == 1, jnp.sum(oh1, axis=0, keepdims=True), 0.0)
    carry[...] = c
    cnt_ref[...] = c


def _rank(route):
    n = route.shape[0]
    tm = _row_tile(n, 256)
    return pl.pallas_call(
        _rank_kernel,
        grid=(n // tm,),
        in_specs=[pl.BlockSpec((tm, LANES), lambda i: (i, 0))],
        out_specs=[pl.BlockSpec((tm, LANES), lambda i: (i, 0)),
                   pl.BlockSpec((8, LANES), lambda i: (0, 0))],
        out_shape=[jax.ShapeDtypeStruct((n, LANES), F32), jax.ShapeDtypeStruct((8, LANES), F32)],
        scratch_shapes=[pltpu.VMEM((8, LANES), F32)],
        compiler_params=_cparams(("arbitrary",)),
        name="moe_rank",
    )(route)


def _slots_kernel(bm, rt_ref, pos_ref, cnt_ref, dest_ref, blk_ref):
    cnt = cnt_ref[...]
    lane = lax.broadcasted_iota(jnp.int32, cnt.shape, 1)
    is_expert = lane < N_EXPERTS
    c0 = jnp.broadcast_to(cnt[0:1, :], cnt.shape)
    padded = jnp.floor((c0 + cnt[1:2, :] + (bm - 1)) * (1.0 / bm)) * bm
    src = lax.broadcasted_iota(jnp.int32, (LANES, LANES), 0)
    dst = lax.broadcasted_iota(jnp.int32, (LANES, LANES), 1)
    pad_end = jnp.dot(padded, jnp.where(src <= dst, 1.0, 0.0).astype(F32), precision=HI,
                      preferred_element_type=F32)
    pad_start = pad_end - padded
    rt = rt_ref[...]
    pos = pos_ref[...]
    tlane = lax.broadcasted_iota(jnp.int32, rt.shape, 1)
    tlane_f = tlane.astype(F32)
    pick = lambda e, table: jnp.sum(jnp.where(tlane_f == e, table[0:1, :], 0.0), axis=-1, keepdims=True)
    d0 = pick(rt[:, 0:1], pad_start) + pos[:, 0:1]
    d1 = pick(rt[:, 1:2], pad_start + c0) + pos[:, 1:2]
    dest_ref[...] = jnp.where(tlane == 0, d0, jnp.where(tlane == 1, d1, 0.0)).astype(jnp.int32)
    nb = blk_ref.shape[0]
    blane = lax.broadcasted_iota(jnp.int32, (nb, LANES), 1)
    start = (lax.broadcasted_iota(jnp.int32, (nb, LANES), 0) * bm).astype(F32)
    ends = jnp.where(blane < N_EXPERTS, pad_end[0:1, :], 3e38)
    expert = jnp.minimum(jnp.sum(jnp.where(ends <= start, 1.0, 0.0), axis=-1, keepdims=True),
                         N_EXPERTS - 1.0)
    total = jnp.max(jnp.where(is_expert, pad_end, 0.0), axis=-1, keepdims=True)[0:1, :]
    used = jnp.where(start < total, 1.0, 0.0)
    blk_ref[...] = jnp.where(blane == 0, expert, jnp.where(blane == 1, used, 0.0)).astype(jnp.int32)


def _slots(route, pos, cnt, n_blocks):
    n = route.shape[0]
    tm = _row_tile(n, 512)
    nbp = -(-n_blocks // 8) * 8
    row = pl.BlockSpec((tm, LANES), lambda i: (i, 0))
    return pl.pallas_call(
        functools.partial(_slots_kernel, MOE_ROWS),
        grid=(n // tm,),
        in_specs=[row, row, pl.BlockSpec((8, LANES), lambda i: (0, 0))],
        out_specs=[row, pl.BlockSpec((nbp, LANES), lambda i: (0, 0))],
        out_shape=[jax.ShapeDtypeStruct((n, LANES), jnp.int32),
                   jax.ShapeDtypeStruct((nbp, LANES), jnp.int32)],
        compiler_params=_cparams(("arbitrary",)),
        name="moe_slots",
    )(route, pos, cnt)


def _row_copy(src, src_row, dst, dst_row, sem):
    return pltpu.make_async_copy(src.at[pl.ds(src_row, 1), :], dst.at[pl.ds(dst_row, 1), :], sem)


def _dispatch_kernel(d0_ref, d1_ref, x_ref, xb_in, xb_ref, sem):
    del xb_in
    tm = x_ref.shape[0]
    base = pl.program_id(0) * tm

    def issue(t, carry):
        _row_copy(x_ref, t, xb_ref, d0_ref[base + t], sem).start()
        _row_copy(x_ref, t, xb_ref, d1_ref[base + t], sem).start()
        return carry

    lax.fori_loop(0, tm, issue, 0, unroll=8)
    for _ in range(TOP_K):
        pltpu.make_async_copy(x_ref, xb_ref.at[pl.ds(0, tm), :], sem).wait()


def _dispatch(h2, dest0, dest1, rows):
    n = h2.shape[0]
    tm = _row_tile(n, 256)
    grid_spec = pltpu.PrefetchScalarGridSpec(
        num_scalar_prefetch=2,
        grid=(n // tm,),
        in_specs=[pl.BlockSpec((tm, D_MODEL), lambda i, d0, d1: (i, 0)),
                  pl.BlockSpec(memory_space=pl.ANY)],
        out_specs=pl.BlockSpec(memory_space=pl.ANY),
        scratch_shapes=[pltpu.SemaphoreType.DMA(())],
    )
    return pl.pallas_call(
        _dispatch_kernel,
        grid_spec=grid_spec,
        out_shape=jax.ShapeDtypeStruct((rows, D_MODEL), F32),
        input_output_aliases={3: 0},
        compiler_params=_cparams(("arbitrary",)),
        name="moe_dispatch",
    )(dest0, dest1, h2, jnp.zeros((rows, D_MODEL), F32))


def _expert_kernel(be_ref, nv_ref, x_ref, wg_ref, wu_ref, wd_ref, y_ref):
    i = pl.program_id(0)
    del be_ref

    @pl.when(nv_ref[i] > 0)
    def _():
        xb = x_ref[...].astype(BF16)
        gate = jnp.dot(xb, wg_ref[...].astype(BF16), preferred_element_type=F32)
        up = jnp.dot(xb, wu_ref[...].astype(BF16), preferred_element_type=F32)
        hdn = gate * _sigmoid(gate) * up
        y_ref[...] = jnp.dot(hdn.astype(BF16), wd_ref[...].astype(BF16),
                             preferred_element_type=F32)

    @pl.when(nv_ref[i] == 0)
    def _():
        y_ref[...] = jnp.zeros(y_ref.shape, F32)


def _experts(xb, blk_e, blk_used, w_gate, w_up, w_down):
    rows = xb.shape[0]
    bm = MOE_ROWS
    wspec = lambda a, b: pl.BlockSpec((None, a, b), lambda i, be, nv: (be[i], 0, 0))
    grid_spec = pltpu.PrefetchScalarGridSpec(
        num_scalar_prefetch=2,
        grid=(rows // bm,),
        in_specs=[pl.BlockSpec((bm, D_MODEL), lambda i, be, nv: (i, 0)),
                  wspec(D_MODEL, D_EXPERT), wspec(D_MODEL, D_EXPERT), wspec(D_EXPERT, D_MODEL)],
        out_specs=pl.BlockSpec((bm, D_MODEL), lambda i, be, nv: (i, 0)),
    )
    return pl.pallas_call(
        _expert_kernel,
        grid_spec=grid_spec,
        out_shape=jax.ShapeDtypeStruct((rows, D_MODEL), F32),
        compiler_params=_cparams(("arbitrary",)),
        name="experts",
    )(blk_e, blk_used, xb, w_gate, w_up, w_down)


def _combine_kernel(d0_ref, d1_ref, x1_ref, rt_ref, gt_ref, yb_ref, o_ref, ya_scr, yb_scr, sem):
    tm = x1_ref.shape[0]
    base = pl.program_id(0) * tm

    def issue(t, carry):
        _row_copy(yb_ref, d0_ref[base + t], ya_scr, t, sem).start()
        _row_copy(yb_ref, d1_ref[base + t], yb_scr, t, sem).start()
        return carry

    lax.fori_loop(0, tm, issue, 0, unroll=8)
    pltpu.make_async_copy(yb_ref.at[pl.ds(0, tm), :], ya_scr, sem).wait()
    pltpu.make_async_copy(yb_ref.at[pl.ds(0, tm), :], yb_scr, sem).wait()
    rt = rt_ref[...]
    moe = rt[:, 2:3] * ya_scr[...] + rt[:, 3:4] * yb_scr[...]
    o_ref[...] = x1_ref[...] + gt_ref[...] * moe


def _combine(x1, route, gt, yb, dest0, dest1):
    n = x1.shape[0]
    tm = _row_tile(n, 256)
    gt_spec = (pl.BlockSpec((1, D_MODEL), lambda i, d0, d1: (0, 0)) if gt.shape[0] == 1
               else pl.BlockSpec((tm, D_MODEL), lambda i, d0, d1: (i, 0)))
    grid_spec = pltpu.PrefetchScalarGridSpec(
        num_scalar_prefetch=2,
        grid=(n // tm,),
        in_specs=[pl.BlockSpec((tm, D_MODEL), lambda i, d0, d1: (i, 0)),
                  pl.BlockSpec((tm, LANES), lambda i, d0, d1: (i, 0)),
                  gt_spec, pl.BlockSpec(memory_space=pl.ANY)],
        out_specs=pl.BlockSpec((tm, D_MODEL), lambda i, d0, d1: (i, 0)),
        scratch_shapes=[pltpu.VMEM((tm, D_MODEL), F32), pltpu.VMEM((tm, D_MODEL), F32),
                        pltpu.SemaphoreType.DMA(())],
    )
    return pl.pallas_call(
        _combine_kernel,
        grid_spec=grid_spec,
        out_shape=jax.ShapeDtypeStruct((n, D_MODEL), F32),
        compiler_params=_cparams(("arbitrary",)),
        name="moe_combine",
    )(dest0, dest1, x1, route, gt, yb)


def _moe(h2, route, x1, gt, w_gate, w_up, w_down):
    n = h2.shape[0]
    bm = MOE_ROWS
    pos, cnt = _rank(route)
    n_blocks = -(-(n * TOP_K) // bm) + N_EXPERTS
    dest, blk = _slots(route, pos, cnt, n_blocks)
    dest0, dest1 = dest[:, 0], dest[:, 1]
    xb = _dispatch(h2, dest0, dest1, n_blocks * bm)
    yb = _experts(xb, blk[:n_blocks, 0], blk[:n_blocks, 1], w_gate, w_up, w_down)
    return _combine(x1, route, gt, yb, dest0, dest1)


def _layer(x, mod, pos, p, w, attend, rwkv):
    sh1, sc1, gt1, sh2, sc2, gt2 = [mod[:, i * D_MODEL:(i + 1) * D_MODEL] for i in range(6)]
    cos, sin = _rope_tables(pos)
    q, k, v, ga, gr, zr, kb, vb = _in_proj(x, sc1, sh1, p['g_mix'], w['w_in'], p['q_gain'],
                                           p['k_gain'], cos, sin)
    o = attend(q, k, v, kb, vb)
    ro, wkv1 = rwkv(zr)
    x1, h2, logits = _merge(x, o, ro, ga, gr, gt1, sc2, sh2, p['g_ffn'], w['w_br_a'], w['w_br_r'],
                            w['w_o'], w['w_router'], w['b_router'])
    y = _moe(h2, logits, x1, gt2, p['w_e_gate'], p['w_e_up'], p['w_e_down'])
    return y, k, v, wkv1, zr


def kernel(x_prompt, x_sample, cache_k, cache_v, state_wkv, state_shift, page_table, c_prompt, c_sample, w_ada, b_ada, g_mix, g_ffn, w_in, q_gain, k_gain, lam_q1, lam_k1, lam_q2, lam_k2, subln_gain, rw_mu, rw_w0, rw_w2, rw_a0, rw_a2, rw_g2, rw_kk, rw_ka, rw_rk, rw_ln_w, rw_ln_b, w_br_a, w_br_r, w_o, w_rg, b_rg, w_re, b_re, w_e_gate, w_e_up, w_e_down):
    assert w_ada.shape[0] == 1, "single-layer kernel"
    B, S, _ = x_prompt.shape
    DB, T, _ = x_sample.shape
    assert B == 1 and T == 1
    past = page_table.shape[1] * PAGE_SIZE
    p = dict(g_mix=g_mix[0], g_ffn=g_ffn[0], q_gain=q_gain[0], k_gain=k_gain[0],
             rw_mu=rw_mu[0], rw_w0=rw_w0[0], rw_w2=rw_w2[0], rw_a0=rw_a0[0], rw_a2=rw_a2[0],
             rw_g2=rw_g2[0], rw_kk=rw_kk[0], rw_ka=rw_ka[0], rw_rk=rw_rk[0],
             rw_ln_w=rw_ln_w[0], rw_ln_b=rw_ln_b[0],
             w_e_gate=w_e_gate[0], w_e_up=w_e_up[0], w_e_down=w_e_down[0])
    pad = ROUTER_PAD - N_GROUPS - N_EXPERTS
    w = dict(w_in=w_in[0].astype(BF16), w_br_a=w_br_a[0].astype(BF16),
             w_br_r=w_br_r[0].astype(BF16), w_o=w_o[0].astype(BF16),
             w_router=jnp.concatenate([w_rg[0], w_re[0], jnp.zeros((D_MODEL, pad), F32)], axis=1),
             b_router=jnp.concatenate([b_rg[0], b_re[0], jnp.zeros((pad,), F32)]).reshape(1, -1))
    lam_rows = [a.reshape(1, A_DH) for a in (lam_q1[0], lam_k1[0], lam_q2[0], lam_k2[0])]

    c_all = jnp.concatenate([c_prompt, jnp.zeros((7, D_MODEL), F32), c_sample], axis=0)
    mod = _ada(c_all, w_ada[0], b_ada[0])
    mod_p, mod_s = mod[0:1], mod[8:8 + DB]

    def rwkv_prompt(zr):
        r, k, v, kk, a, lw, g = _rwkv_prep(zr, jnp.zeros((1, R_IN), F32), True, p)
        y, s1 = _wkv_chunk(r, k, v, kk, a, lw, jnp.zeros((R_HEADS, R_DH, R_DH), F32))
        return _rwkv_post(y, r, k, v, g, p), s1

    def rwkv_sample(zr):
        r, k, v, kk, a, lw, g = _rwkv_prep(zr, state_shift[0], False, p)
        y, s1 = _wkv_step(state_wkv[0], r, k, v, kk, a, lw)
        return _rwkv_post(y, r, k, v, g, p), s1

    attend_p = lambda q, k, v, kb, vb: _attn_prompt(q, kb, vb, lam_rows, subln_gain[0])
    attend_s = lambda q, k, v, kb, vb: _attn_sample(q, k, v, cache_k[0], cache_v[0], page_table,
                                                    lam_rows, subln_gain[0])

    yp, kp, vp, wp, zrp = _layer(x_prompt[0], mod_p, jnp.arange(S), p, w, attend_p, rwkv_prompt)
    ys, ks_, vs_, ws_, zrs = _layer(x_sample[:, 0], mod_s, jnp.full((DB,), past), p, w, attend_s,
                                    rwkv_sample)
    return (yp.reshape(1, S, D_MODEL), ys.reshape(DB, 1, D_MODEL),
            kp.reshape(1, 1, S, A_HEADS, 2 * A_DH), vp.reshape(1, 1, S, A_HEADS, A_DV),
            wp.reshape(1, 1, R_HEADS, R_DH, R_DH), zrp[S - 1:S].reshape(1, 1, R_IN),
            ks_.reshape(1, DB, 1, A_HEADS, 2 * A_DH), vs_.reshape(1, DB, 1, A_HEADS, A_DV),
            ws_.reshape(1, DB, R_HEADS, R_DH, R_DH), zrs.reshape(1, DB, R_IN))
```

```python
import functools
import math

import jax
import jax.numpy as jnp
from jax import lax
from jax.experimental import pallas as pl
from jax.experimental.pallas import tpu as pltpu

F32 = jnp.float32
BF16 = jnp.bfloat16
HI = lax.Precision.HIGHEST

D_MODEL = 1024
PAGE_SIZE = 128
A_DH = 64
A_DV = 2 * A_DH
A_WIDTH = D_MODEL // 2
A_HEADS = A_WIDTH // A_DV
ROPE_THETA = 10000.0
R_DH = 64
R_WIDTH = D_MODEL // 2
R_HEADS = R_WIDTH // R_DH
DECAY_LORA = 64
AAA_LORA = 64
GATE_LORA = 160
GN_EPS = 64e-5
RMS_EPS = 1e-6
R_OFF_K = R_WIDTH
R_OFF_V = 2 * R_WIDTH
R_OFF_W = 3 * R_WIDTH
R_OFF_A = R_OFF_W + DECAY_LORA
R_OFF_G = R_OFF_A + AAA_LORA
R_IN = R_OFF_G + GATE_LORA
COL_K = A_HEADS * 2 * A_DH
COL_V = 2 * COL_K
COL_GA = COL_V + A_WIDTH
COL_GR = COL_GA + D_MODEL
COL_RW = COL_GR + D_MODEL
D_IN = COL_RW + R_IN
N_GROUPS = 4
EXPERTS_PER_GROUP = 8
N_EXPERTS = N_GROUPS * EXPERTS_PER_GROUP
TOP_K = 2
D_EXPERT = 512
LAM_INIT = 0.8 - 0.6 * math.exp(-0.3 * 0)

LANES = 128
ROUTER_PAD = LANES
NEG_BIG = -1e30
VMEM_LIMIT = 56 * 1024 * 1024

WKV_CHUNK = 64
WKV_CHUNKS_PER_STEP = 4
ATTN_Q_BLOCK = 1024
ATTN_K_BLOCK = 1024
ATTN_ROW_CHUNK = 256

_NN = (((1,), (0,)), ((), ()))
_NT = (((1,), (1,)), ((), ()))
_TN = (((0,), (0,)), ((), ()))
SEQS_PER_STEP = 4
PAGES_PER_STEP = 4
PAGE_BUFFERS = 3
MOE_ROWS = 512
MERGE_ROW_CHUNK = 128


def _cparams(sem):
    return pltpu.CompilerParams(dimension_semantics=sem, vmem_limit_bytes=VMEM_LIMIT)


def _row_tile(n, pref):
    t = min(n, pref)
    assert n % t == 0, (n, t)
    return t


def _seg_ones(width, seg, scale=1.0):
    r = lax.broadcasted_iota(jnp.int32, (width, width), 0) // seg
    c = lax.broadcasted_iota(jnp.int32, (width, width), 1) // seg
    return jnp.where(r == c, scale, 0.0).astype(F32)


def _seg_reduce(x, seg):
    hi, lo = _split(x)
    sb = seg.astype(BF16)
    return (jnp.dot(hi, sb, preferred_element_type=F32)
            + jnp.dot(lo, sb, preferred_element_type=F32))


def _sigmoid(x):
    return 1.0 / (1.0 + jnp.exp(-x))


def _ada_kernel(c_ref, w_ref, b_ref, o_ref):
    c = c_ref[...]
    s = c * _sigmoid(c)
    o_ref[...] = jnp.dot(s, w_ref[...], precision=HI, preferred_element_type=F32) + b_ref[...]


def _ada(c, w_ada, b_ada):
    rows = c.shape[0]
    n_out = w_ada.shape[1]
    tn = 1536
    return pl.pallas_call(
        _ada_kernel,
        grid=(n_out // tn,),
        in_specs=[pl.BlockSpec((rows, D_MODEL), lambda j: (0, 0)),
                  pl.BlockSpec((D_MODEL, tn), lambda j: (0, j)),
                  pl.BlockSpec((1, tn), lambda j: (0, j))],
        out_specs=pl.BlockSpec((rows, tn), lambda j: (0, j)),
        out_shape=jax.ShapeDtypeStruct((rows, n_out), F32),
        compiler_params=_cparams(("arbitrary",)),
        name="ada",
    )(c, w_ada, b_ada.reshape(1, n_out))


def _mod_spec(rows, tm):
    if rows == 1:
        return pl.BlockSpec((1, D_MODEL), lambda i: (0, 0))
    return pl.BlockSpec((tm, D_MODEL), lambda i: (i, 0))


def _inproj_kernel(x_ref, sc_ref, sh_ref, g_ref, w_ref, qg_ref, kg_ref, cos_ref, sin_ref,
                   q_ref, k_ref, v_ref, ga_ref, gr_ref, zr_ref, kb_ref, vb_ref):
    x = x_ref[...]
    tm = x.shape[0]
    ms = jnp.mean(x * x, axis=-1, keepdims=True)
    h = x * lax.rsqrt(ms + RMS_EPS) * g_ref[...]
    h = h * (1.0 + sc_ref[...]) + sh_ref[...]
    hb = h.astype(BF16)

    def sec(a, b):
        return jnp.dot(hb, w_ref[:, a:b], preferred_element_type=F32)

    seg_mean = _seg_ones(LANES, A_DH, 1.0 / A_DH)
    cos = cos_ref[...]
    sin = sin_ref[...]
    lane = lax.broadcasted_iota(jnp.int32, cos.shape, 1)
    first_half = (lane % A_DH) < (A_DH // 2)

    def norm_rope(z, gain):
        m = _seg_reduce(z * z, seg_mean)
        zn = z * lax.rsqrt(m + RMS_EPS) * gain
        swapped = jnp.where(first_half, pltpu.roll(zn, LANES - A_DH // 2, 1),
                            pltpu.roll(zn, A_DH // 2, 1))
        return zn * cos + swapped * sin

    zq = sec(0, COL_K)
    zk = sec(COL_K, COL_V)
    for hd in range(A_HEADS):
        sl = slice(hd * LANES, (hd + 1) * LANES)
        q_ref[:, sl] = norm_rope(zq[:, sl], qg_ref[...])
        kh = norm_rope(zk[:, sl], kg_ref[...])
        k_ref[pl.ds(hd, tm, stride=A_HEADS), :] = kh
        kb_ref[:, sl] = kh.astype(BF16)
    v = sec(COL_V, COL_GA)
    for hd in range(A_HEADS):
        v_ref[pl.ds(hd, tm, stride=A_HEADS), :] = v[:, hd * LANES:(hd + 1) * LANES]
    vb_ref[...] = v.astype(BF16)
    ga_ref[...] = _sigmoid(sec(COL_GA, COL_GR))
    gr_ref[...] = _sigmoid(sec(COL_GR, COL_RW))
    zr_ref[...] = sec(COL_RW, D_IN)


def _in_proj(x, sc, sh, g_mix, w_in_bf, q_gain, k_gain, cos, sin):
    n = x.shape[0]
    tm = _row_tile(n, 256)
    row = lambda w: pl.BlockSpec((tm, w), lambda i: (i, 0))
    const = lambda r, w: pl.BlockSpec((r, w), lambda i: (0, 0))
    gain2 = lambda g: jnp.tile(g.reshape(1, A_DH), (1, 2))
    f32_out = lambda w: (row(w), jax.ShapeDtypeStruct((n, w), F32))
    by_head = (pl.BlockSpec((tm * A_HEADS, A_DV), lambda i: (i, 0)),
               jax.ShapeDtypeStruct((n * A_HEADS, A_DV), F32))
    bf_out = (row(A_WIDTH), jax.ShapeDtypeStruct((n, A_WIDTH), BF16))
    outs = [f32_out(A_WIDTH), by_head, by_head, f32_out(D_MODEL), f32_out(D_MODEL), f32_out(R_IN),
            bf_out, bf_out]
    return pl.pallas_call(
        _inproj_kernel,
        grid=(n // tm,),
        in_specs=[row(D_MODEL), _mod_spec(sc.shape[0], tm), _mod_spec(sh.shape[0], tm),
                  const(1, D_MODEL), const(D_MODEL, D_IN), const(1, LANES), const(1, LANES),
                  row(LANES), row(LANES)],
        out_specs=[o[0] for o in outs],
        out_shape=[o[1] for o in outs],
        compiler_params=_cparams(("arbitrary",)),
        name="in_proj",
    )(x, sc, sh, g_mix.reshape(1, D_MODEL), w_in_bf, gain2(q_gain), gain2(k_gain), cos, sin)


def _rope_tables(pos):
    half = A_DH // 2
    inv = ROPE_THETA ** (-jnp.arange(half, dtype=F32) / half)
    lane = jnp.arange(LANES)
    inv_l = inv[lane % half]
    sign = jnp.where((lane % A_DH) < half, -1.0, 1.0).astype(F32)
    ang = pos.astype(F32)[:, None] * inv_l[None, :]
    return jnp.cos(ang), jnp.sin(ang) * sign[None, :]


def _lambda(lq1, lk1, lq2, lk2):
    s1 = jnp.sum(lq1 * lk1, axis=-1, keepdims=True)
    s2 = jnp.sum(lq2 * lk2, axis=-1, keepdims=True)
    return jnp.exp(s1) - jnp.exp(s2) + LAM_INIT


def _subln(o, gain):
    ms = jnp.mean(o * o, axis=-1, keepdims=True)
    return o * lax.rsqrt(ms + RMS_EPS) * gain * (1.0 - LAM_INIT)


def _attn_prompt_kernel(bk, q_ref, k_ref, v_ref, lq1_ref, lk1_ref, lq2_ref, lk2_ref, gain_ref,
                        o_ref, qs_scr, m_scr, acc_scr):
    i = pl.program_id(1)
    bq = q_ref.shape[0]
    rc = min(bq, ATTN_ROW_CHUNK)
    q = q_ref[...] * (A_DH ** -0.5 * math.log2(math.e))
    lane = lax.broadcasted_iota(jnp.int32, q.shape, 1)
    qs_scr[0:bq, :] = jnp.where(lane < A_DH, q, 0.0).astype(BF16)
    qs_scr[bq:2 * bq, :] = jnp.where(lane >= A_DH, q, 0.0).astype(BF16)
    m_scr[...] = jnp.full(m_scr.shape, NEG_BIG, F32)
    acc_scr[...] = jnp.zeros(acc_scr.shape, F32)
    ones = jnp.ones((bk, LANES), BF16)

    def update(start, mask_offset):
        kb = k_ref[pl.ds(start, bk), :]
        vx = jnp.concatenate([v_ref[pl.ds(start, bk), :], ones], axis=1)
        for c in range(2 * bq // rc):
            rows = slice(c * rc, (c + 1) * rc)
            s = lax.dot_general(qs_scr[rows, :], kb, _NT, preferred_element_type=F32)
            if mask_offset is not None:
                row = lax.broadcasted_iota(jnp.int32, (rc, bk), 0) + (c * rc) % bq
                col = lax.broadcasted_iota(jnp.int32, (rc, bk), 1) + mask_offset
                s = jnp.where(col <= row, s, NEG_BIG)
            m_prev = m_scr[rows, :]
            m_new = jnp.maximum(m_prev, jnp.max(s, axis=-1, keepdims=True))
            pr = jnp.exp2((s - jnp.tile(m_new, (1, bk // LANES))).astype(BF16))
            alpha = jnp.exp2(m_prev - m_new)
            acc_scr[rows, :] = jnp.tile(alpha, (1, 2)) * acc_scr[rows, :] + jnp.dot(
                pr, vx, preferred_element_type=F32)
            m_scr[rows, :] = m_new

    n_below = i * (bq // bk)

    def below_diagonal(j, carry):
        update(pl.multiple_of(2 * j * bk, bk), None)
        update(pl.multiple_of((2 * j + 1) * bk, bk), None)
        return carry

    lax.fori_loop(0, n_below // 2, below_diagonal, 0)

    @pl.when(n_below % 2 == 1)
    def _():
        update(pl.multiple_of((n_below - 1) * bk, bk), None)

    for jj in range(bq // bk):
        update(pl.multiple_of(i * bq + jj * bk, bk), jj * bk)
    acc = acc_scr[...]
    d = acc[:, 0:LANES] / acc[:, LANES:2 * LANES]
    lam = _lambda(lq1_ref[...], lk1_ref[...], lq2_ref[...], lk2_ref[...])
    o_ref[...] = _subln(d[0:bq, :] - lam * d[bq:2 * bq, :], gain_ref[...])


def _attn_prompt(q, kb, vb, lam_rows, subln_gain):
    n = q.shape[0]
    bq = _row_tile(n, ATTN_Q_BLOCK)
    bk = _row_tile(bq, ATTN_K_BLOCK)
    const = lambda w: pl.BlockSpec((1, w), lambda h, i: (0, 0))
    head = pl.BlockSpec((n, LANES), lambda h, i: (0, h))
    return pl.pallas_call(
        functools.partial(_attn_prompt_kernel, bk),
        grid=(A_HEADS, n // bq),
        in_specs=[pl.BlockSpec((bq, LANES), lambda h, i: (i, h)), head, head,
                  const(A_DH), const(A_DH), const(A_DH), const(A_DH), const(A_DV)],
        out_specs=pl.BlockSpec((bq, LANES), lambda h, i: (i, h)),
        out_shape=jax.ShapeDtypeStruct((n, A_WIDTH), F32),
        scratch_shapes=[pltpu.VMEM((2 * bq, LANES), BF16), pltpu.VMEM((2 * bq, LANES), F32),
                        pltpu.VMEM((2 * bq, 2 * LANES), F32)],
        compiler_params=_cparams(("arbitrary", "arbitrary")),
        name="attn_prompt",
    )(q, kb, vb, *lam_rows, subln_gain.reshape(1, A_DV))


def _attn_sample_kernel(*refs):
    ns, pps, nbuf = SEQS_PER_STEP, PAGES_PER_STEP, PAGE_BUFFERS
    (pt_ref, q_ref, kn_ref, vn_ref, lq1_ref, lk1_ref, lq2_ref, lk2_ref, gain_ref, ck_ref, cv_ref,
     o_ref, kbuf, vbuf, sems, m_scr, l_scr, acc_scr) = refs
    g = pl.program_id(1)
    ng = pl.num_programs(1)
    step = pl.program_id(0) * ng + g
    total = pl.num_programs(0) * ng

    def page_copies(step_idx, slot):
        bb = step_idx // ng
        gg = step_idx - bb * ng
        copies = []
        for u in range(ns):
            for t in range(pps):
                page = pt_ref[(bb * ns + u) * (ng * pps) + gg * pps + t]
                j = u * pps + t
                copies.append(pltpu.make_async_copy(ck_ref.at[page], kbuf.at[slot, j], sems.at[0, slot]))
                copies.append(pltpu.make_async_copy(cv_ref.at[page], vbuf.at[slot, j], sems.at[1, slot]))
        return copies

    @pl.when(step == 0)
    def _():
        for d in range(nbuf - 1):
            for c in page_copies(d, d):
                c.start()

    ahead = step + (nbuf - 1)

    @pl.when(ahead < total)
    def _():
        for c in page_copies(ahead, ahead % nbuf):
            c.start()

    slot = step % nbuf
    for c in page_copies(step, slot):
        c.wait()
    k_refs = [kbuf.at[slot, j] for j in range(ns * pps)]
    v_refs = [vbuf.at[slot, j] for j in range(ns * pps)]
    rows = 2 * A_HEADS
    page_rows = PAGE_SIZE * A_HEADS
    by_head = lambda x: jnp.concatenate(
        [jnp.broadcast_to(x[:, hd * A_DV:(hd + 1) * A_DV], (2, A_DV)) for hd in range(A_HEADS)],
        axis=0)
    rid = lax.broadcasted_iota(jnp.int32, (rows, A_DV), 0)
    comp = lax.broadcasted_iota(jnp.int32, (rows, A_DV), 1) // A_DH
    srow = lax.broadcasted_iota(jnp.int32, (rows, pps * page_rows), 0) // 2
    scol = lax.broadcasted_iota(jnp.int32, (rows, pps * page_rows), 1) % A_HEADS
    lam = _lambda(lq1_ref[...], lk1_ref[...], lq2_ref[...], lk2_ref[...])

    qms = [jnp.where(rid % 2 == comp, by_head(q_ref[u] * (A_DH ** -0.5)), 0.0) for u in range(ns)]

    @pl.when(g == 0)
    def _():
        for u in range(ns):
            m_scr[u] = jnp.sum(qms[u] * by_head(kn_ref[u]), axis=-1, keepdims=True)
            l_scr[u] = jnp.ones((rows, 1), F32)
            acc_scr[u] = by_head(vn_ref[u])

    for u in range(ns):
        qb = qms[u].astype(BF16)
        s = jnp.concatenate(
            [lax.dot_general(qb, k_refs[u * pps + t][...].astype(BF16), _NT,
                             preferred_element_type=F32) for t in range(pps)], axis=1)
        s = jnp.where(srow == scol, s, NEG_BIG)
        m_prev = m_scr[u]
        m_new = jnp.maximum(m_prev, jnp.max(s, axis=-1, keepdims=True))
        alpha = jnp.exp(m_prev - m_new)
        pr = jnp.exp(s - m_new)
        l_scr[u] = alpha * l_scr[u] + jnp.sum(pr, axis=-1, keepdims=True)
        prb = pr.astype(BF16)
        pv = jnp.dot(prb[:, 0:page_rows], v_refs[u * pps][...].astype(BF16),
                     preferred_element_type=F32)
        for t in range(1, pps):
            pv = pv + jnp.dot(prb[:, t * page_rows:(t + 1) * page_rows],
                              v_refs[u * pps + t][...].astype(BF16), preferred_element_type=F32)
        acc_scr[u] = alpha * acc_scr[u] + pv
        m_scr[u] = m_new

    @pl.when(g == pl.num_programs(1) - 1)
    def _():
        for u in range(ns):
            d = acc_scr[u] / l_scr[u]
            for hd in range(A_HEADS):
                o = d[2 * hd:2 * hd + 1, :] - lam * d[2 * hd + 1:2 * hd + 2, :]
                o_ref[u, :, hd * A_DV:(hd + 1) * A_DV] = _subln(o, gain_ref[...])


def _attn_sample(q, k_new, v_new, cache_k, cache_v, page_table, lam_rows, subln_gain):
    nb, n_pages = page_table.shape
    ns, pps, nbuf = SEQS_PER_STEP, PAGES_PER_STEP, PAGE_BUFFERS
    assert n_pages % pps == 0 and nb % ns == 0
    assert (nb // ns) * (n_pages // pps) >= nbuf - 1
    page_rows = PAGE_SIZE * A_HEADS
    ck = cache_k.reshape(cache_k.shape[0], page_rows, A_DV)
    cv = cache_v.reshape(cache_v.shape[0], page_rows, A_DV)
    pt = page_table.reshape(-1)
    tok = pl.BlockSpec((ns, 1, A_WIDTH), lambda b, g, pt: (b, 0, 0))
    const = lambda w: pl.BlockSpec((1, w), lambda b, g, pt: (0, 0))
    hbm = pl.BlockSpec(memory_space=pl.ANY)
    grid_spec = pltpu.PrefetchScalarGridSpec(
        num_scalar_prefetch=1,
        grid=(nb // ns, n_pages // pps),
        in_specs=[tok, tok, tok, const(A_DH), const(A_DH), const(A_DH), const(A_DH), const(A_DV),
                  hbm, hbm],
        out_specs=tok,
        scratch_shapes=[pltpu.VMEM((nbuf, ns * pps, page_rows, A_DV), F32),
                        pltpu.VMEM((nbuf, ns * pps, page_rows, A_DV), F32),
                        pltpu.SemaphoreType.DMA((2, nbuf)),
                        pltpu.VMEM((ns, 2 * A_HEADS, 1), F32), pltpu.VMEM((ns, 2 * A_HEADS, 1), F32),
                        pltpu.VMEM((ns, 2 * A_HEADS, A_DV), F32)],
    )
    tok3 = lambda a: a.reshape(nb, 1, A_WIDTH)
    out = pl.pallas_call(
        _attn_sample_kernel,
        grid_spec=grid_spec,
        out_shape=jax.ShapeDtypeStruct((nb, 1, A_WIDTH), F32),
        compiler_params=_cparams(("arbitrary", "arbitrary")),
        name="attn_sample",
    )(pt, tok3(q), tok3(k_new), tok3(v_new), *lam_rows, subln_gain.reshape(1, A_DV), ck, cv)
    return out.reshape(nb, A_WIDTH)


def _rwkv_prep_kernel(seq_mode, zr_ref, prev_ref, mu_ref, w0_ref, w2_ref, a0_ref, a2_ref, g2_ref,
                      kkp_ref, ka_ref, r_ref, k_ref, v_ref, kk_ref, a_ref, lw_ref, g_ref, *scr):
    z = zr_ref[...]
    if seq_mode:
        (carry,) = scr

        @pl.when(pl.program_id(0) == 0)
        def _():
            carry[...] = prev_ref[...]

        row = lax.broadcasted_iota(jnp.int32, z.shape, 0)
        zp = jnp.where(row == 0, carry[...], pltpu.roll(z, 1, 0))
        carry[...] = z[z.shape[0] - 1:z.shape[0], :]
    else:
        zp = prev_ref[...]
    zs = z + (zp - z) * mu_ref[...]
    r = zs[:, 0:R_OFF_K]
    k = zs[:, R_OFF_K:R_OFF_V]
    v = zs[:, R_OFF_V:R_OFF_W]
    zw = zs[:, R_OFF_W:R_OFF_A]
    za = zs[:, R_OFF_A:R_OFF_G]
    zg = zs[:, R_OFF_G:R_IN]
    w_pre = w0_ref[...] + _dot3(jnp.tanh(zw), w2_ref[...], _NN)
    nx = -w_pre
    softplus = jnp.maximum(nx, 0.0) + jnp.log(1.0 + jnp.exp(-jnp.abs(nx)))
    lw_ref[...] = -jnp.exp(-softplus - 0.5)
    a = _sigmoid(a0_ref[...] + _dot3(za, a2_ref[...], _NN))
    g_ref[...] = _mm1(_sigmoid(zg), g2_ref[...])
    kkr = k * kkp_ref[...]
    seg_sum = _seg_ones(LANES, R_DH)
    for sb in range(R_WIDTH // LANES):
        sl = slice(sb * LANES, (sb + 1) * LANES)
        x = kkr[:, sl]
        ss = _seg_reduce(x * x, seg_sum)
        kk_ref[:, sl] = x / jnp.maximum(jnp.sqrt(ss), 1e-12)
    r_ref[...] = r
    v_ref[...] = v
    a_ref[...] = a
    k_ref[...] = k * (1.0 + (a - 1.0) * ka_ref[...])


def _rwkv_prep(zr, prev, seq_mode, p):
    n = zr.shape[0]
    tm = _row_tile(n, 256)
    row = lambda w: pl.BlockSpec((tm, w), lambda i: (i, 0))
    const = lambda r, w: pl.BlockSpec((r, w), lambda i: (0, 0))
    prev_spec = const(1, R_IN) if seq_mode else row(R_IN)
    vec = lambda a: a.reshape(1, -1)
    return pl.pallas_call(
        functools.partial(_rwkv_prep_kernel, seq_mode),
        grid=(n // tm,),
        in_specs=[row(R_IN), prev_spec, const(1, R_IN), const(1, R_WIDTH),
                  const(DECAY_LORA, R_WIDTH), const(1, R_WIDTH), const(AAA_LORA, R_WIDTH),
                  const(GATE_LORA, R_WIDTH), const(1, R_WIDTH), const(1, R_WIDTH)],
        out_specs=[row(R_WIDTH)] * 7,
        out_shape=[jax.ShapeDtypeStruct((n, R_WIDTH), F32)] * 7,
        scratch_shapes=[pltpu.VMEM((1, R_IN), F32)] if seq_mode else [],
        compiler_params=_cparams(("arbitrary",)),
        name="rwkv_prep_seq" if seq_mode else "rwkv_prep_batch",
    )(zr, prev, vec(p['rw_mu']), vec(p['rw_w0']), p['rw_w2'], vec(p['rw_a0']), p['rw_a2'],
      p['rw_g2'], vec(p['rw_kk']), vec(p['rw_ka']))


def _split(x):
    hi = x.astype(BF16)
    return hi, (x - hi.astype(F32)).astype(BF16)


def _dot3(a, b, dims):
    ah, al = _split(a)
    bh, bl = _split(b)
    d = lambda x, y: lax.dot_general(x, y, dims, preferred_element_type=F32)
    return d(ah, bh) + (d(ah, bl) + d(al, bh))


def _mm1(a, b):
    return jnp.dot(a.astype(BF16), b.astype(BF16), preferred_element_type=F32)


def _mm(a, b):
    return _dot3(a, b, _NN)


def _mm_nt(a, b):
    return _dot3(a, b, _NT)


def _mm_tn(a, b):
    return _dot3(a, b, _TN)


def _wkv_chunk_kernel(nch, r_ref, k_ref, v_ref, kk_ref, a_ref, lw_ref, s0_ref, y_ref, s_ref):
    rows = r_ref.shape[0]
    c = rows // nch

    @pl.when(pl.program_id(0) == 0)
    def _():
        s_ref[...] = s0_ref[...]

    ti = lax.broadcasted_iota(jnp.int32, (c, c), 0)
    si = lax.broadcasted_iota(jnp.int32, (c, c), 1)
    lower = si <= ti
    strict = si < ti
    lw = lw_ref[...]
    bt_i = lax.broadcasted_iota(jnp.int32, (rows, rows), 0)
    bs_i = lax.broadcasted_iota(jnp.int32, (rows, rows), 1)
    same_chunk_lower = jnp.logical_and(bs_i <= bt_i, bs_i // c == bt_i // c)
    cs = jnp.dot(jnp.where(same_chunk_lower, 1.0, 0.0).astype(F32), lw, precision=HI,
                 preferred_element_type=F32)
    chunk_rows = [slice(ci * c, (ci + 1) * c) for ci in range(nch)]
    total = jnp.concatenate(
        [jnp.broadcast_to(cs[(ci + 1) * c - 1:(ci + 1) * c, :], (c, R_WIDTH)) for ci in range(nch)],
        axis=0)
    e_pos = jnp.exp(cs)
    e_prev = jnp.exp(cs - lw)
    e_neg = jnp.exp(-cs)
    e_rem = jnp.exp(total - cs)
    e_tot = jnp.exp(total)
    kk = kk_ref[...]
    k = k_ref[...]
    b = kk * a_ref[...]
    at_all = kk * e_prev
    bt_all = b * e_neg
    kt_all = k * e_neg
    rt_all = r_ref[...] * e_pos
    bh_all = b * e_rem
    kh_all = k * e_rem
    v_all = v_ref[...]
    eye = jnp.where(si == ti, 1.0, 0.0).astype(F32)

    pairs = [(ci, h) for ci in range(nch) for h in range(R_HEADS)]
    heads = range(len(pairs))
    sub = lambda x, p: x[chunk_rows[p[0]], p[1] * R_DH:(p[1] + 1) * R_DH]
    at = [sub(at_all, p) for p in pairs]
    rt = [sub(rt_all, p) for p in pairs]
    v = [sub(v_all, p) for p in pairs]
    a4 = [_mm_nt(jnp.concatenate([at[h], rt[h]], axis=0),
                 jnp.concatenate([sub(bt_all, pairs[h]), sub(kt_all, pairs[h])], axis=0))
          for h in heads]
    aak = [jnp.where(strict, x[0:c, c:2 * c], 0.0) for x in a4]
    arb = [jnp.where(lower, x[c:2 * c, 0:c], 0.0) for x in a4]
    ark = [jnp.where(lower, x[c:2 * c, c:2 * c], 0.0) for x in a4]
    nl = [jnp.where(strict, -x[0:c, 0:c], 0.0) for x in a4]
    inv = [eye + x for x in nl]
    pw = [_mm(x, x) for x in nl]
    span = 2
    while span < c:
        if 2 * span < c:
            both = [_mm1(jnp.concatenate([inv[h], pw[h]], axis=0), pw[h]) for h in heads]
            inv = [inv[h] + both[h][0:c, :] for h in heads]
            pw = [x[c:2 * c, :] for x in both]
        else:
            inv = [inv[h] + _mm1(inv[h], pw[h]) for h in heads]
        span *= 2
    av = [_mm(jnp.concatenate([aak[h], ark[h]], axis=0), v[h]) for h in heads]
    tw = [_mm(inv[h], jnp.concatenate([at[h], av[h][0:c, :]], axis=1)) for h in heads]
    kv = [_mm_tn(v[h], sub(kh_all, pairs[h])) for h in heads]
    state = [s_ref[hd] for hd in range(R_HEADS)]
    for ci in range(nch):
        idx = [ci * R_HEADS + hd for hd in range(R_HEADS)]
        hs = [_mm_nt(jnp.concatenate([-tw[h][:, 0:R_DH], rt[h]], axis=0), state[h % R_HEADS])
              for h in idx]
        u = [hs[j][0:c, :] - tw[h][:, R_DH:2 * R_DH] for j, h in enumerate(idx)]
        au = [_mm(arb[h], u[j]) for j, h in enumerate(idx)]
        ub = [_mm_tn(u[j], sub(bh_all, pairs[h])) for j, h in enumerate(idx)]
        for j, h in enumerate(idx):
            p = pairs[h]
            lanes = slice(p[1] * R_DH, (p[1] + 1) * R_DH)
            y_ref[chunk_rows[ci], lanes] = hs[j][c:2 * c, :] + au[j] + av[h][c:2 * c, :]
            state[j] = state[j] * e_tot[ci * c:ci * c + 1, lanes] + ub[j] + kv[h]
    for hd in range(R_HEADS):
        s_ref[hd] = state[hd]


def _wkv_chunk(r, k, v, kk, a, lw, s0):
    n = r.shape[0]
    c = _row_tile(n, WKV_CHUNK)
    nch = WKV_CHUNKS_PER_STEP if n % (c * WKV_CHUNKS_PER_STEP) == 0 else 1
    row = pl.BlockSpec((c * nch, R_WIDTH), lambda i: (i, 0))
    st = pl.BlockSpec((R_HEADS, R_DH, R_DH), lambda i: (0, 0, 0))
    return pl.pallas_call(
        functools.partial(_wkv_chunk_kernel, nch),
        grid=(n // (c * nch),),
        in_specs=[row] * 6 + [st],
        out_specs=[row, st],
        out_shape=[jax.ShapeDtypeStruct((n, R_WIDTH), F32),
                   jax.ShapeDtypeStruct((R_HEADS, R_DH, R_DH), F32)],
        compiler_params=_cparams(("arbitrary",)),
        name="wkv_chunk",
    )(r, k, v, kk, a, lw, s0)


def _wkv_step_kernel(s_ref, r_ref, k_ref, v_ref, kk_ref, a_ref, lw_ref, y_ref, so_ref):
    s = s_ref[...]
    kk = kk_ref[...]
    sa = -jnp.sum(s * kk, axis=-1, keepdims=True)
    s2 = s * jnp.exp(lw_ref[...]) + sa * (kk * a_ref[...]) + v_ref[...] * k_ref[...]
    so_ref[...] = s2
    y_ref[...] = jnp.sum(s2 * r_ref[...], axis=-1, keepdims=True)


def _wkv_step(state, r, k, v, kk, a, lw):
    nb = state.shape[0]
    bs = _row_tile(nb, 8)
    rowv = lambda x: x.reshape(nb, R_HEADS, 1, R_DH)
    st = pl.BlockSpec((bs, R_HEADS, R_DH, R_DH), lambda i: (i, 0, 0, 0))
    rw = pl.BlockSpec((bs, R_HEADS, 1, R_DH), lambda i: (i, 0, 0, 0))
    col = pl.BlockSpec((bs, R_HEADS, R_DH, 1), lambda i: (i, 0, 0, 0))
    y, s_new = pl.pallas_call(
        _wkv_step_kernel,
        grid=(nb // bs,),
        in_specs=[st, rw, rw, col, rw, rw, rw],
        out_specs=[col, st],
        out_shape=[jax.ShapeDtypeStruct((nb, R_HEADS, R_DH, 1), F32),
                   jax.ShapeDtypeStruct(state.shape, F32)],
        compiler_params=_cparams(("arbitrary",)),
        name="wkv_step",
    )(state, rowv(r), rowv(k), v.reshape(nb, R_HEADS, R_DH, 1), rowv(kk), rowv(a), rowv(lw))
    return y.reshape(nb, R_WIDTH), s_new


def _rwkv_post_kernel(y_ref, r_ref, k_ref, v_ref, g_ref, lnw_ref, lnb_ref, rk_ref, o_ref):
    seg_mean = _seg_ones(LANES, R_DH, 1.0 / R_DH)
    seg_sum = _seg_ones(LANES, R_DH)
    for sb in range(R_WIDTH // LANES):
        sl = slice(sb * LANES, (sb + 1) * LANES)
        y = y_ref[:, sl]
        mean = _seg_reduce(y, seg_mean)
        d = y - mean
        var = _seg_reduce(d * d, seg_mean)
        yn = d * lax.rsqrt(var + GN_EPS) * lnw_ref[:, sl] + lnb_ref[:, sl]
        bonus = _seg_reduce(r_ref[:, sl] * k_ref[:, sl] * rk_ref[:, sl], seg_sum)
        o_ref[:, sl] = (yn + bonus * v_ref[:, sl]) * g_ref[:, sl]


def _rwkv_post(y, r, k, v, g, p):
    n = y.shape[0]
    tm = _row_tile(n, 512)
    row = pl.BlockSpec((tm, R_WIDTH), lambda i: (i, 0))
    const = pl.BlockSpec((1, R_WIDTH), lambda i: (0, 0))
    vec = lambda a: a.reshape(1, R_WIDTH)
    return pl.pallas_call(
        _rwkv_post_kernel,
        grid=(n // tm,),
        in_specs=[row] * 5 + [const] * 3,
        out_specs=row,
        out_shape=jax.ShapeDtypeStruct((n, R_WIDTH), F32),
        compiler_params=_cparams(("arbitrary",)),
        name="rwkv_post",
    )(y, r, k, v, g, vec(p['rw_ln_w']), vec(p['rw_ln_b']), vec(p['rw_rk']))


def _merge_kernel(x_ref, o_ref, ro_ref, ga_ref, gr_ref, gt_ref, sc_ref, sh_ref, g_ref,
                  wa_ref, wr_ref, wo_ref, wrt_ref, brt_ref, x1_ref, h2_ref, rt_ref):
    tm = x_ref.shape[0]
    rc = min(tm, MERGE_ROW_CHUNK)
    mod = lambda ref, sl: ref[...] if ref.shape[0] == 1 else ref[sl, :]
    for c in range(tm // rc):
        sl = slice(c * rc, (c + 1) * rc)
        ma = jnp.dot(o_ref[sl, :].astype(BF16), wa_ref[...], preferred_element_type=F32)
        mr = jnp.dot(ro_ref[sl, :].astype(BF16), wr_ref[...], preferred_element_type=F32)
        mg = ga_ref[sl, :] * ma + gr_ref[sl, :] * mr
        merged = jnp.dot(mg.astype(BF16), wo_ref[...], preferred_element_type=F32)
        x1 = x_ref[sl, :] + mod(gt_ref, sl) * merged
        x1_ref[sl, :] = x1
        ms = jnp.mean(x1 * x1, axis=-1, keepdims=True)
        h2 = x1 * lax.rsqrt(ms + RMS_EPS) * g_ref[...]
        h2 = h2 * (1.0 + mod(sc_ref, sl)) + mod(sh_ref, sl)
        h2_ref[sl, :] = h2
        logits = _dot3(h2, wrt_ref[...], _NN) + brt_ref[...]
        rt_ref[sl, :] = _route(logits)


def _route(logits):
    lane = lax.broadcasted_iota(jnp.int32, logits.shape, 1)
    lane_f = lane.astype(F32)
    first_max = lambda x, m: jnp.min(jnp.where(x == m, lane_f, float(LANES)), axis=-1, keepdims=True)
    is_g = lane < N_GROUPS
    lg = jnp.where(is_g, logits, NEG_BIG)
    gmax = jnp.max(lg, axis=-1, keepdims=True)
    g_idx = first_max(lg, gmax)
    sum_g = jnp.sum(jnp.where(is_g, jnp.exp(lg - gmax), 0.0), axis=-1, keepdims=True)
    group_of_lane = ((lane - N_GROUPS) // EXPERTS_PER_GROUP).astype(F32)
    in_group = jnp.where(lane >= N_GROUPS, group_of_lane, -1.0) == g_idx
    le = jnp.where(in_group, logits, NEG_BIG)
    m1 = jnp.max(le, axis=-1, keepdims=True)
    i1 = first_max(le, m1)
    le2 = jnp.where(lane_f == i1, NEG_BIG, le)
    m2 = jnp.max(le2, axis=-1, keepdims=True)
    i2 = first_max(le2, m2)
    t = jnp.exp(m2 - m1)
    w1 = 1.0 / (sum_g * (1.0 + t))
    out = jnp.where(lane == 0, i1 - N_GROUPS, 0.0)
    out = jnp.where(lane == 1, i2 - N_GROUPS, out)
    out = jnp.where(lane == 2, w1, out)
    return jnp.where(lane == 3, w1 * t, out)


def _merge(x, o, ro, ga, gr, gt, sc, sh, g_ffn, wa_bf, wr_bf, wo_bf, w_router, b_router):
    n = x.shape[0]
    tm = _row_tile(n, 512)
    row = lambda w: pl.BlockSpec((tm, w), lambda i: (i, 0))
    const = lambda r, w: pl.BlockSpec((r, w), lambda i: (0, 0))
    mod = lambda a: _mod_spec(a.shape[0], tm)
    return pl.pallas_call(
        _merge_kernel,
        grid=(n // tm,),
        in_specs=[row(D_MODEL), row(A_WIDTH), row(R_WIDTH), row(D_MODEL), row(D_MODEL),
                  mod(gt), mod(sc), mod(sh), const(1, D_MODEL),
                  const(A_WIDTH, D_MODEL), const(R_WIDTH, D_MODEL), const(D_MODEL, D_MODEL),
                  const(D_MODEL, ROUTER_PAD), const(1, ROUTER_PAD)],
        out_specs=[row(D_MODEL), row(D_MODEL), row(ROUTER_PAD)],
        out_shape=[jax.ShapeDtypeStruct((n, D_MODEL), F32), jax.ShapeDtypeStruct((n, D_MODEL), F32),
                   jax.ShapeDtypeStruct((n, ROUTER_PAD), F32)],
        compiler_params=_cparams(("arbitrary",)),
        name="merge",
    )(x, o, ro, ga, gr, gt, sc, sh, g_ffn.reshape(1, D_MODEL), wa_bf, wr_bf, wo_bf,
      w_router, b_router)


def _rank_kernel(rt_ref, pos_ref, cnt_ref, carry):
    tm = rt_ref.shape[0]

    @pl.when(pl.program_id(0) == 0)
    def _():
        carry[...] = jnp.zeros(carry.shape, F32)

    rt = rt_ref[...]
    lane = lax.broadcasted_iota(jnp.int32, rt.shape, 1)
    lane_f = lane.astype(F32)
    oh0 = jnp.where(lane_f == rt[:, 0:1], 1.0, 0.0)
    oh1 = jnp.where(lane_f == rt[:, 1:2], 1.0, 0.0)
    ti = lax.broadcasted_iota(jnp.int32, (tm, tm), 0)
    si = lax.broadcasted_iota(jnp.int32, (tm, tm), 1)
    earlier = jnp.where(si < ti, 1.0, 0.0).astype(BF16)
    pre = jnp.dot(earlier, jnp.concatenate([oh0, oh1], axis=1).astype(BF16),
                  preferred_element_type=F32)
    c = carry[...]
    rank0 = jnp.sum(oh0 * (pre[:, 0:LANES] + c[0:1, :]), axis=-1, keepdims=True)
    rank1 = jnp.sum(oh1 * (pre[:, LANES:2 * LANES] + c[1:2, :]), axis=-1, keepdims=True)
    pos_ref[...] = jnp.where(lane == 0, rank0, jnp.where(lane == 1, rank1, 0.0))
    row = lax.broadcasted_iota(jnp.int32, c.shape, 0)
    c = c + jnp.where(row == 0, jnp.sum(oh0, axis=0, keepdims=True), 0.0) \
          + jnp.where(row == 1, jnp.sum(oh1, axis=0, keepdims=True), 0.0)
    carry[...] = c
    cnt_ref[...] = c


def _rank(route):
    n = route.shape[0]
    tm = _row_tile(n, 256)
    return pl.pallas_call(
        _rank_kernel,
        grid=(n // tm,),
        in_specs=[pl.BlockSpec((tm, LANES), lambda i: (i, 0))],
        out_specs=[pl.BlockSpec((tm, LANES), lambda i: (i, 0)),
                   pl.BlockSpec((8, LANES), lambda i: (0, 0))],
        out_shape=[jax.ShapeDtypeStruct((n, LANES), F32), jax.ShapeDtypeStruct((8, LANES), F32)],
        scratch_shapes=[pltpu.VMEM((8, LANES), F32)],
        compiler_params=_cparams(("arbitrary",)),
        name="moe_rank",
    )(route)


def _slots_kernel(bm, rt_ref, pos_ref, cnt_ref, dest_ref, blk_ref):
    cnt = cnt_ref[...]
    lane = lax.broadcasted_iota(jnp.int32, cnt.shape, 1)
    is_expert = lane < N_EXPERTS
    c0 = jnp.broadcast_to(cnt[0:1, :], cnt.shape)
    padded = jnp.floor((c0 + cnt[1:2, :] + (bm - 1)) * (1.0 / bm)) * bm
    src = lax.broadcasted_iota(jnp.int32, (LANES, LANES), 0)
    dst = lax.broadcasted_iota(jnp.int32, (LANES, LANES), 1)
    pad_end = jnp.dot(padded, jnp.where(src <= dst, 1.0, 0.0).astype(F32), precision=HI,
                      preferred_element_type=F32)
    pad_start = pad_end - padded
    rt = rt_ref[...]
    pos = pos_ref[...]
    tlane = lax.broadcasted_iota(jnp.int32, rt.shape, 1)
    tlane_f = tlane.astype(F32)
    pick = lambda e, table: jnp.sum(jnp.where(tlane_f == e, table[0:1, :], 0.0), axis=-1, keepdims=True)
    d0 = pick(rt[:, 0:1], pad_start) + pos[:, 0:1]
    d1 = pick(rt[:, 1:2], pad_start + c0) + pos[:, 1:2]
    dest_ref[...] = jnp.where(tlane == 0, d0, jnp.where(tlane == 1, d1, 0.0)).astype(jnp.int32)
    nb = blk_ref.shape[0]
    blane = lax.broadcasted_iota(jnp.int32, (nb, LANES), 1)
    start = (lax.broadcasted_iota(jnp.int32, (nb, LANES), 0) * bm).astype(F32)
    ends = jnp.where(blane < N_EXPERTS, pad_end[0:1, :], 3e38)
    expert = jnp.minimum(jnp.sum(jnp.where(ends <= start, 1.0, 0.0), axis=-1, keepdims=True),
                         N_EXPERTS - 1.0)
    total = jnp.max(jnp.where(is_expert, pad_end, 0.0), axis=-1, keepdims=True)[0:1, :]
    used = jnp.where(start < total, 1.0, 0.0)
    blk_ref[...] = jnp.where(blane == 0, expert, jnp.where(blane == 1, used, 0.0)).astype(jnp.int32)


def _slots(route, pos, cnt, n_blocks):
    n = route.shape[0]
    tm = _row_tile(n, 512)
    nbp = -(-n_blocks // 8) * 8
    row = pl.BlockSpec((tm, LANES), lambda i: (i, 0))
    return pl.pallas_call(
        functools.partial(_slots_kernel, MOE_ROWS),
        grid=(n // tm,),
        in_specs=[row, row, pl.BlockSpec((8, LANES), lambda i: (0, 0))],
        out_specs=[row, pl.BlockSpec((nbp, LANES), lambda i: (0, 0))],
        out_shape=[jax.ShapeDtypeStruct((n, LANES), jnp.int32),
                   jax.ShapeDtypeStruct((nbp, LANES), jnp.int32)],
        compiler_params=_cparams(("arbitrary",)),
        name="moe_slots",
    )(route, pos, cnt)


def _row_copy(src, src_row, dst, dst_row, sem):
    return pltpu.make_async_copy(src.at[pl.ds(src_row, 1), :], dst.at[pl.ds(dst_row, 1), :], sem)


def _dispatch_kernel(d0_ref, d1_ref, x_ref, xb_in, xb_ref, sem):
    del xb_in
    tm = x_ref.shape[0]
    base = pl.program_id(0) * tm

    def issue(t, carry):
        _row_copy(x_ref, t, xb_ref, d0_ref[base + t], sem).start()
        _row_copy(x_ref, t, xb_ref, d1_ref[base + t], sem).start()
        return carry

    lax.fori_loop(0, tm, issue, 0, unroll=8)
    for _ in range(TOP_K):
        pltpu.make_async_copy(x_ref, xb_ref.at[pl.ds(0, tm), :], sem).wait()


def _dispatch(h2, dest0, dest1, rows):
    n = h2.shape[0]
    tm = _row_tile(n, 256)
    grid_spec = pltpu.PrefetchScalarGridSpec(
        num_scalar_prefetch=2,
        grid=(n // tm,),
        in_specs=[pl.BlockSpec((tm, D_MODEL), lambda i, d0, d1: (i, 0)),
                  pl.BlockSpec(memory_space=pl.ANY)],
        out_specs=pl.BlockSpec(memory_space=pl.ANY),
        scratch_shapes=[pltpu.SemaphoreType.DMA(())],
    )
    return pl.pallas_call(
        _dispatch_kernel,
        grid_spec=grid_spec,
        out_shape=jax.ShapeDtypeStruct((rows, D_MODEL), F32),
        input_output_aliases={3: 0},
        compiler_params=_cparams(("arbitrary",)),
        name="moe_dispatch",
    )(dest0, dest1, h2, jnp.zeros((rows, D_MODEL), F32))


def _expert_kernel(be_ref, nv_ref, x_ref, wg_ref, wu_ref, wd_ref, y_ref):
    i = pl.program_id(0)
    del be_ref

    @pl.when(nv_ref[i] > 0)
    def _():
        xb = x_ref[...].astype(BF16)
        gate = jnp.dot(xb, wg_ref[...].astype(BF16), preferred_element_type=F32)
        up = jnp.dot(xb, wu_ref[...].astype(BF16), preferred_element_type=F32)
        hdn = gate * _sigmoid(gate) * up
        y_ref[...] = jnp.dot(hdn.astype(BF16), wd_ref[...].astype(BF16),
                             preferred_element_type=F32)

    @pl.when(nv_ref[i] == 0)
    def _():
        y_ref[...] = jnp.zeros(y_ref.shape, F32)


def _experts(xb, blk_e, blk_used, w_gate, w_up, w_down):
    rows = xb.shape[0]
    bm = MOE_ROWS
    wspec = lambda a, b: pl.BlockSpec((None, a, b), lambda i, be, nv: (be[i], 0, 0))
    grid_spec = pltpu.PrefetchScalarGridSpec(
        num_scalar_prefetch=2,
        grid=(rows // bm,),
        in_specs=[pl.BlockSpec((bm, D_MODEL), lambda i, be, nv: (i, 0)),
                  wspec(D_MODEL, D_EXPERT), wspec(D_MODEL, D_EXPERT), wspec(D_EXPERT, D_MODEL)],
        out_specs=pl.BlockSpec((bm, D_MODEL), lambda i, be, nv: (i, 0)),
    )
    return pl.pallas_call(
        _expert_kernel,
        grid_spec=grid_spec,
        out_shape=jax.ShapeDtypeStruct((rows, D_MODEL), F32),
        compiler_params=_cparams(("arbitrary",)),
        name="experts",
    )(blk_e, blk_used, xb, w_gate, w_up, w_down)


def _combine_kernel(d0_ref, d1_ref, x1_ref, rt_ref, gt_ref, yb_ref, o_ref, ya_scr, yb_scr, sem):
    tm = x1_ref.shape[0]
    base = pl.program_id(0) * tm

    def issue(t, carry):
        _row_copy(yb_ref, d0_ref[base + t], ya_scr, t, sem).start()
        _row_copy(yb_ref, d1_ref[base + t], yb_scr, t, sem).start()
        return carry

    lax.fori_loop(0, tm, issue, 0, unroll=8)
    pltpu.make_async_copy(yb_ref.at[pl.ds(0, tm), :], ya_scr, sem).wait()
    pltpu.make_async_copy(yb_ref.at[pl.ds(0, tm), :], yb_scr, sem).wait()
    rt = rt_ref[...]
    moe = rt[:, 2:3] * ya_scr[...] + rt[:, 3:4] * yb_scr[...]
    o_ref[...] = x1_ref[...] + gt_ref[...] * moe


def _combine(x1, route, gt, yb, dest0, dest1):
    n = x1.shape[0]
    tm = _row_tile(n, 256)
    gt_spec = (pl.BlockSpec((1, D_MODEL), lambda i, d0, d1: (0, 0)) if gt.shape[0] == 1
               else pl.BlockSpec((tm, D_MODEL), lambda i, d0, d1: (i, 0)))
    grid_spec = pltpu.PrefetchScalarGridSpec(
        num_scalar_prefetch=2,
        grid=(n // tm,),
        in_specs=[pl.BlockSpec((tm, D_MODEL), lambda i, d0, d1: (i, 0)),
                  pl.BlockSpec((tm, LANES), lambda i, d0, d1: (i, 0)),
                  gt_spec, pl.BlockSpec(memory_space=pl.ANY)],
        out_specs=pl.BlockSpec((tm, D_MODEL), lambda i, d0, d1: (i, 0)),
        scratch_shapes=[pltpu.VMEM((tm, D_MODEL), F32), pltpu.VMEM((tm, D_MODEL), F32),
                        pltpu.SemaphoreType.DMA(())],
    )
    return pl.pallas_call(
        _combine_kernel,
        grid_spec=grid_spec,
        out_shape=jax.ShapeDtypeStruct((n, D_MODEL), F32),
        compiler_params=_cparams(("arbitrary",)),
        name="moe_combine",
    )(dest0, dest1, x1, route, gt, yb)


def _moe(h2, route, x1, gt, w_gate, w_up, w_down):
    n = h2.shape[0]
    bm = MOE_ROWS
    pos, cnt = _rank(route)
    n_blocks = -(-(n * TOP_K) // bm) + N_EXPERTS
    dest, blk = _slots(route, pos, cnt, n_blocks)
    dest0, dest1 = dest[:, 0], dest[:, 1]
    xb = _dispatch(h2, dest0, dest1, n_blocks * bm)
    yb = _experts(xb, blk[:n_blocks, 0], blk[:n_blocks, 1], w_gate, w_up, w_down)
    return _combine(x1, route, gt, yb, dest0, dest1)


def _layer(x, mod, pos, p, w, attend, rwkv):
    sh1, sc1, gt1, sh2, sc2, gt2 = [mod[:, i * D_MODEL:(i + 1) * D_MODEL] for i in range(6)]
    cos, sin = _rope_tables(pos)
    q, k, v, ga, gr, zr, kb, vb = _in_proj(x, sc1, sh1, p['g_mix'], w['w_in'], p['q_gain'],
                                           p['k_gain'], cos, sin)
    o = attend(q, k, v, kb, vb)
    ro, wkv1 = rwkv(zr)
    x1, h2, logits = _merge(x, o, ro, ga, gr, gt1, sc2, sh2, p['g_ffn'], w['w_br_a'], w['w_br_r'],
                            w['w_o'], w['w_router'], w['b_router'])
    y = _moe(h2, logits, x1, gt2, p['w_e_gate'], p['w_e_up'], p['w_e_down'])
    return y, k, v, wkv1, zr


def kernel(x_prompt, x_sample, cache_k, cache_v, state_wkv, state_shift, page_table, c_prompt, c_sample, w_ada, b_ada, g_mix, g_ffn, w_in, q_gain, k_gain, lam_q1, lam_k1, lam_q2, lam_k2, subln_gain, rw_mu, rw_w0, rw_w2, rw_a0, rw_a2, rw_g2, rw_kk, rw_ka, rw_rk, rw_ln_w, rw_ln_b, w_br_a, w_br_r, w_o, w_rg, b_rg, w_re, b_re, w_e_gate, w_e_up, w_e_down):
    assert w_ada.shape[0] == 1, "single-layer kernel"
    B, S, _ = x_prompt.shape
    DB, T, _ = x_sample.shape
    assert B == 1 and T == 1
    past = page_table.shape[1] * PAGE_SIZE
    p = dict(g_mix=g_mix[0], g_ffn=g_ffn[0], q_gain=q_gain[0], k_gain=k_gain[0],
             rw_mu=rw_mu[0], rw_w0=rw_w0[0], rw_w2=rw_w2[0], rw_a0=rw_a0[0], rw_a2=rw_a2[0],
             rw_g2=rw_g2[0], rw_kk=rw_kk[0], rw_ka=rw_ka[0], rw_rk=rw_rk[0],
             rw_ln_w=rw_ln_w[0], rw_ln_b=rw_ln_b[0],
             w_e_gate=w_e_gate[0], w_e_up=w_e_up[0], w_e_down=w_e_down[0])
    pad = ROUTER_PAD - N_GROUPS - N_EXPERTS
    w = dict(w_in=w_in[0].astype(BF16), w_br_a=w_br_a[0].astype(BF16),
             w_br_r=w_br_r[0].astype(BF16), w_o=w_o[0].astype(BF16),
             w_router=jnp.concatenate([w_rg[0], w_re[0], jnp.zeros((D_MODEL, pad), F32)], axis=1),
             b_router=jnp.concatenate([b_rg[0], b_re[0], jnp.zeros((pad,), F32)]).reshape(1, -1))
    lam_rows = [a.reshape(1, A_DH) for a in (lam_q1[0], lam_k1[0], lam_q2[0], lam_k2[0])]

    c_all = jnp.concatenate([c_prompt, jnp.zeros((7, D_MODEL), F32), c_sample], axis=0)
    mod = _ada(c_all, w_ada[0], b_ada[0])
    mod_p, mod_s = mod[0:1], mod[8:8 + DB]

    def rwkv_prompt(zr):
        r, k, v, kk, a, lw, g = _rwkv_prep(zr, jnp.zeros((1, R_IN), F32), True, p)
        y, s1 = _wkv_chunk(r, k, v, kk, a, lw, jnp.zeros((R_HEADS, R_DH, R_DH), F32))
        return _rwkv_post(y, r, k, v, g, p), s1

    def rwkv_sample(zr):
        r, k, v, kk, a, lw, g = _rwkv_prep(zr, state_shift[0], False, p)
        y, s1 = _wkv_step(state_wkv[0], r, k, v, kk, a, lw)
        return _rwkv_post(y, r, k, v, g, p), s1

    attend_p = lambda q, k, v, kb, vb: _attn_prompt(q, kb, vb, lam_rows, subln_gain[0])
    attend_s = lambda q, k, v, kb, vb: _attn_sample(q, k, v, cache_k[0], cache_v[0], page_table,
                                                    lam_rows, subln_gain[0])

    yp, kp, vp, wp, zrp = _layer(x_prompt[0], mod_p, jnp.arange(S), p, w, attend_p, rwkv_prompt)
    ys, ks_, vs_, ws_, zrs = _layer(x_sample[:, 0], mod_s, jnp.full((DB,), past), p, w, attend_s,
                                    rwkv_sample)
    return (yp.reshape(1, S, D_MODEL), ys.reshape(DB, 1, D_MODEL),
            kp.reshape(1, 1, S, A_HEADS, 2 * A_DH), vp.reshape(1, 1, S, A_HEADS, A_DV),
            wp.reshape(1, 1, R_HEADS, R_DH, R_DH), zrp[S - 1:S].reshape(1, 1, R_IN),
            ks_.reshape(1, DB, 1, A_HEADS, 2 * A_DH), vs_.reshape(1, DB, 1, A_HEADS, A_DV),
            ws_.reshape(1, DB, R_HEADS, R_DH, R_DH), zrs.reshape(1, DB, R_IN))
```

```python
import functools
import math

import jax
import jax.numpy as jnp
from jax import lax
from jax.experimental import pallas as pl
from jax.experimental.pallas import tpu as pltpu

F32 = jnp.float32
BF16 = jnp.bfloat16
HI = lax.Precision.HIGHEST

D_MODEL = 1024
PAGE_SIZE = 128
A_DH = 64
A_DV = 2 * A_DH
A_WIDTH = D_MODEL // 2
A_HEADS = A_WIDTH // A_DV
ROPE_THETA = 10000.0
R_DH = 64
R_WIDTH = D_MODEL // 2
R_HEADS = R_WIDTH // R_DH
DECAY_LORA = 64
AAA_LORA = 64
GATE_LORA = 160
GN_EPS = 64e-5
RMS_EPS = 1e-6
R_OFF_K = R_WIDTH
R_OFF_V = 2 * R_WIDTH
R_OFF_W = 3 * R_WIDTH
R_OFF_A = R_OFF_W + DECAY_LORA
R_OFF_G = R_OFF_A + AAA_LORA
R_IN = R_OFF_G + GATE_LORA
COL_K = A_HEADS * 2 * A_DH
COL_V = 2 * COL_K
COL_GA = COL_V + A_WIDTH
COL_GR = COL_GA + D_MODEL
COL_RW = COL_GR + D_MODEL
D_IN = COL_RW + R_IN
N_GROUPS = 4
EXPERTS_PER_GROUP = 8
N_EXPERTS = N_GROUPS * EXPERTS_PER_GROUP
TOP_K = 2
D_EXPERT = 512
LAM_INIT = 0.8 - 0.6 * math.exp(-0.3 * 0)

LANES = 128
ROUTER_PAD = LANES
NEG_BIG = -1e30
VMEM_LIMIT = 56 * 1024 * 1024

WKV_CHUNK = 64
WKV_CHUNKS_PER_STEP = 4
ATTN_Q_BLOCK = 1024
ATTN_K_BLOCK = 1024
ATTN_ROW_CHUNK = 256

_NN = (((1,), (0,)), ((), ()))
_NT = (((1,), (1,)), ((), ()))
_TN = (((0,), (0,)), ((), ()))
SEQS_PER_STEP = 4
PAGES_PER_STEP = 4
PAGE_BUFFERS = 3
MOE_ROWS = 256
MERGE_ROW_CHUNK = 128


def _cparams(sem):
    return pltpu.CompilerParams(dimension_semantics=sem, vmem_limit_bytes=VMEM_LIMIT)


def _row_tile(n, pref):
    t = min(n, pref)
    assert n % t == 0, (n, t)
    return t


def _seg_ones(width, seg, scale=1.0):
    r = lax.broadcasted_iota(jnp.int32, (width, width), 0) // seg
    c = lax.broadcasted_iota(jnp.int32, (width, width), 1) // seg
    return jnp.where(r == c, scale, 0.0).astype(F32)


def _seg_reduce(x, seg):
    hi, lo = _split(x)
    sb = seg.astype(BF16)
    return (jnp.dot(hi, sb, preferred_element_type=F32)
            + jnp.dot(lo, sb, preferred_element_type=F32))


def _sigmoid(x):
    return 1.0 / (1.0 + jnp.exp(-x))


def _ada_kernel(c_ref, w_ref, b_ref, o_ref):
    c = c_ref[...]
    s = c * _sigmoid(c)
    o_ref[...] = jnp.dot(s, w_ref[...], precision=HI, preferred_element_type=F32) + b_ref[...]


def _ada(c, w_ada, b_ada):
    rows = c.shape[0]
    n_out = w_ada.shape[1]
    tn = 1536
    return pl.pallas_call(
        _ada_kernel,
        grid=(n_out // tn,),
        in_specs=[pl.BlockSpec((rows, D_MODEL), lambda j: (0, 0)),
                  pl.BlockSpec((D_MODEL, tn), lambda j: (0, j)),
                  pl.BlockSpec((1, tn), lambda j: (0, j))],
        out_specs=pl.BlockSpec((rows, tn), lambda j: (0, j)),
        out_shape=jax.ShapeDtypeStruct((rows, n_out), F32),
        compiler_params=_cparams(("arbitrary",)),
        name="ada",
    )(c, w_ada, b_ada.reshape(1, n_out))


def _mod_spec(rows, tm):
    if rows == 1:
        return pl.BlockSpec((1, D_MODEL), lambda i: (0, 0))
    return pl.BlockSpec((tm, D_MODEL), lambda i: (i, 0))


def _inproj_kernel(x_ref, sc_ref, sh_ref, g_ref, w_ref, qg_ref, kg_ref, cos_ref, sin_ref,
                   q_ref, k_ref, v_ref, ga_ref, gr_ref, zr_ref, kb_ref, vb_ref):
    x = x_ref[...]
    tm = x.shape[0]
    ms = jnp.mean(x * x, axis=-1, keepdims=True)
    h = x * lax.rsqrt(ms + RMS_EPS) * g_ref[...]
    h = h * (1.0 + sc_ref[...]) + sh_ref[...]
    hb = h.astype(BF16)

    def sec(a, b):
        return jnp.dot(hb, w_ref[:, a:b], preferred_element_type=F32)

    seg_mean = _seg_ones(LANES, A_DH, 1.0 / A_DH)
    cos = cos_ref[...]
    sin = sin_ref[...]
    lane = lax.broadcasted_iota(jnp.int32, cos.shape, 1)
    first_half = (lane % A_DH) < (A_DH // 2)

    def norm_rope(z, gain):
        m = _seg_reduce(z * z, seg_mean)
        zn = z * lax.rsqrt(m + RMS_EPS) * gain
        swapped = jnp.where(first_half, pltpu.roll(zn, LANES - A_DH // 2, 1),
                            pltpu.roll(zn, A_DH // 2, 1))
        return zn * cos + swapped * sin

    zq = sec(0, COL_K)
    zk = sec(COL_K, COL_V)
    for hd in range(A_HEADS):
        sl = slice(hd * LANES, (hd + 1) * LANES)
        q_ref[:, sl] = norm_rope(zq[:, sl], qg_ref[...])
        kh = norm_rope(zk[:, sl], kg_ref[...])
        k_ref[pl.ds(hd, tm, stride=A_HEADS), :] = kh
        kb_ref[:, sl] = kh.astype(BF16)
    v = sec(COL_V, COL_GA)
    for hd in range(A_HEADS):
        v_ref[pl.ds(hd, tm, stride=A_HEADS), :] = v[:, hd * LANES:(hd + 1) * LANES]
    vb_ref[...] = v.astype(BF16)
    ga_ref[...] = _sigmoid(sec(COL_GA, COL_GR)).astype(BF16)
    gr_ref[...] = _sigmoid(sec(COL_GR, COL_RW)).astype(BF16)
    zr_ref[...] = sec(COL_RW, D_IN)


def _in_proj(x, sc, sh, g_mix, w_in_bf, q_gain, k_gain, cos, sin):
    n = x.shape[0]
    tm = _row_tile(n, 256)
    row = lambda w: pl.BlockSpec((tm, w), lambda i: (i, 0))
    const = lambda r, w: pl.BlockSpec((r, w), lambda i: (0, 0))
    gain2 = lambda g: jnp.tile(g.reshape(1, A_DH), (1, 2))
    f32_out = lambda w: (row(w), jax.ShapeDtypeStruct((n, w), F32))
    by_head = (pl.BlockSpec((tm * A_HEADS, A_DV), lambda i: (i, 0)),
               jax.ShapeDtypeStruct((n * A_HEADS, A_DV), F32))
    bf_out = (row(A_WIDTH), jax.ShapeDtypeStruct((n, A_WIDTH), BF16))
    gate_out = (row(D_MODEL), jax.ShapeDtypeStruct((n, D_MODEL), BF16))
    outs = [f32_out(A_WIDTH), by_head, by_head, gate_out, gate_out, f32_out(R_IN),
            bf_out, bf_out]
    return pl.pallas_call(
        _inproj_kernel,
        grid=(n // tm,),
        in_specs=[row(D_MODEL), _mod_spec(sc.shape[0], tm), _mod_spec(sh.shape[0], tm),
                  const(1, D_MODEL), const(D_MODEL, D_IN), const(1, LANES), const(1, LANES),
                  row(LANES), row(LANES)],
        out_specs=[o[0] for o in outs],
        out_shape=[o[1] for o in outs],
        compiler_params=_cparams(("arbitrary",)),
        name="in_proj",
    )(x, sc, sh, g_mix.reshape(1, D_MODEL), w_in_bf, gain2(q_gain), gain2(k_gain), cos, sin)


def _rope_tables(pos):
    half = A_DH // 2
    inv = ROPE_THETA ** (-jnp.arange(half, dtype=F32) / half)
    lane = jnp.arange(LANES)
    inv_l = inv[lane % half]
    sign = jnp.where((lane % A_DH) < half, -1.0, 1.0).astype(F32)
    ang = pos.astype(F32)[:, None] * inv_l[None, :]
    return jnp.cos(ang), jnp.sin(ang) * sign[None, :]


def _lambda(lq1, lk1, lq2, lk2):
    s1 = jnp.sum(lq1 * lk1, axis=-1, keepdims=True)
    s2 = jnp.sum(lq2 * lk2, axis=-1, keepdims=True)
    return jnp.exp(s1) - jnp.exp(s2) + LAM_INIT


def _subln(o, gain):
    ms = jnp.mean(o * o, axis=-1, keepdims=True)
    return o * lax.rsqrt(ms + RMS_EPS) * gain * (1.0 - LAM_INIT)


def _attn_prompt_kernel(bk, q_ref, k_ref, v_ref, lq1_ref, lk1_ref, lq2_ref, lk2_ref, gain_ref,
                        o_ref, qs_scr, m_scr, acc_scr):
    i = pl.program_id(1)
    bq = q_ref.shape[0]
    rc = min(bq, ATTN_ROW_CHUNK)
    q = q_ref[...] * (A_DH ** -0.5 * math.log2(math.e))
    lane = lax.broadcasted_iota(jnp.int32, q.shape, 1)
    qs_scr[0:bq, :] = jnp.where(lane < A_DH, q, 0.0).astype(BF16)
    qs_scr[bq:2 * bq, :] = jnp.where(lane >= A_DH, q, 0.0).astype(BF16)
    m_scr[...] = jnp.full(m_scr.shape, NEG_BIG, F32)
    acc_scr[...] = jnp.zeros(acc_scr.shape, F32)
    ones = jnp.ones((bk, LANES), BF16)

    def update(start, mask_offset):
        kb = k_ref[pl.ds(start, bk), :]
        vx = jnp.concatenate([v_ref[pl.ds(start, bk), :], ones], axis=1)
        for c in range(2 * bq // rc):
            rows = slice(c * rc, (c + 1) * rc)
            s = lax.dot_general(qs_scr[rows, :], kb, _NT, preferred_element_type=F32)
            if mask_offset is not None:
                row = lax.broadcasted_iota(jnp.int32, (rc, bk), 0) + (c * rc) % bq
                col = lax.broadcasted_iota(jnp.int32, (rc, bk), 1) + mask_offset
                s = jnp.where(col <= row, s, NEG_BIG)
            m_prev = m_scr[rows, :]
            m_new = jnp.maximum(m_prev, jnp.max(s, axis=-1, keepdims=True))
            pr = jnp.exp2((s - jnp.tile(m_new, (1, bk // LANES))).astype(BF16))
            alpha = jnp.exp2(m_prev - m_new)
            acc_scr[rows, :] = jnp.tile(alpha, (1, 2)) * acc_scr[rows, :] + jnp.dot(
                pr, vx, preferred_element_type=F32)
            m_scr[rows, :] = m_new

    n_below = i * (bq // bk)

    def below_diagonal(j, carry):
        update(pl.multiple_of(2 * j * bk, bk), None)
        update(pl.multiple_of((2 * j + 1) * bk, bk), None)
        return carry

    lax.fori_loop(0, n_below // 2, below_diagonal, 0)

    @pl.when(n_below % 2 == 1)
    def _():
        update(pl.multiple_of((n_below - 1) * bk, bk), None)

    for jj in range(bq // bk):
        update(pl.multiple_of(i * bq + jj * bk, bk), jj * bk)
    acc = acc_scr[...]
    d = acc[:, 0:LANES] / acc[:, LANES:2 * LANES]
    lam = _lambda(lq1_ref[...], lk1_ref[...], lq2_ref[...], lk2_ref[...])
    o_ref[...] = _subln(d[0:bq, :] - lam * d[bq:2 * bq, :], gain_ref[...])


def _attn_prompt(q, kb, vb, lam_rows, subln_gain):
    n = q.shape[0]
    bq = _row_tile(n, ATTN_Q_BLOCK)
    bk = _row_tile(bq, ATTN_K_BLOCK)
    const = lambda w: pl.BlockSpec((1, w), lambda h, i: (0, 0))
    head = pl.BlockSpec((n, LANES), lambda h, i: (0, h))
    return pl.pallas_call(
        functools.partial(_attn_prompt_kernel, bk),
        grid=(A_HEADS, n // bq),
        in_specs=[pl.BlockSpec((bq, LANES), lambda h, i: (i, h)), head, head,
                  const(A_DH), const(A_DH), const(A_DH), const(A_DH), const(A_DV)],
        out_specs=pl.BlockSpec((bq, LANES), lambda h, i: (i, h)),
        out_shape=jax.ShapeDtypeStruct((n, A_WIDTH), F32),
        scratch_shapes=[pltpu.VMEM((2 * bq, LANES), BF16), pltpu.VMEM((2 * bq, LANES), F32),
                        pltpu.VMEM((2 * bq, 2 * LANES), F32)],
        compiler_params=_cparams(("arbitrary", "arbitrary")),
        name="attn_prompt",
    )(q, kb, vb, *lam_rows, subln_gain.reshape(1, A_DV))


def _attn_sample_kernel(*refs):
    ns, pps, nbuf = SEQS_PER_STEP, PAGES_PER_STEP, PAGE_BUFFERS
    (pt_ref, q_ref, kn_ref, vn_ref, lq1_ref, lk1_ref, lq2_ref, lk2_ref, gain_ref, ck_ref, cv_ref,
     o_ref, kbuf, vbuf, sems, m_scr, l_scr, acc_scr) = refs
    g = pl.program_id(1)
    ng = pl.num_programs(1)
    step = pl.program_id(0) * ng + g
    total = pl.num_programs(0) * ng

    def page_copies(step_idx, slot):
        bb = step_idx // ng
        gg = step_idx - bb * ng
        copies = []
        for u in range(ns):
            for t in range(pps):
                page = pt_ref[(bb * ns + u) * (ng * pps) + gg * pps + t]
                j = u * pps + t
                copies.append(pltpu.make_async_copy(ck_ref.at[page], kbuf.at[slot, j], sems.at[0, slot]))
                copies.append(pltpu.make_async_copy(cv_ref.at[page], vbuf.at[slot, j], sems.at[1, slot]))
        return copies

    @pl.when(step == 0)
    def _():
        for d in range(nbuf - 1):
            for c in page_copies(d, d):
                c.start()

    ahead = step + (nbuf - 1)

    @pl.when(ahead < total)
    def _():
        for c in page_copies(ahead, ahead % nbuf):
            c.start()

    slot = step % nbuf
    for c in page_copies(step, slot):
        c.wait()
    k_refs = [kbuf.at[slot, j] for j in range(ns * pps)]
    v_refs = [vbuf.at[slot, j] for j in range(ns * pps)]
    rows = 2 * A_HEADS
    page_rows = PAGE_SIZE * A_HEADS
    by_head = lambda x: jnp.concatenate(
        [jnp.broadcast_to(x[:, hd * A_DV:(hd + 1) * A_DV], (2, A_DV)) for hd in range(A_HEADS)],
        axis=0)
    rid = lax.broadcasted_iota(jnp.int32, (rows, A_DV), 0)
    comp = lax.broadcasted_iota(jnp.int32, (rows, A_DV), 1) // A_DH
    srow = lax.broadcasted_iota(jnp.int32, (rows, pps * page_rows), 0) // 2
    scol = lax.broadcasted_iota(jnp.int32, (rows, pps * page_rows), 1) % A_HEADS
    lam = _lambda(lq1_ref[...], lk1_ref[...], lq2_ref[...], lk2_ref[...])

    qms = [jnp.where(rid % 2 == comp, by_head(q_ref[u] * (A_DH ** -0.5)), 0.0) for u in range(ns)]

    @pl.when(g == 0)
    def _():
        for u in range(ns):
            m_scr[u] = jnp.sum(qms[u] * by_head(kn_ref[u]), axis=-1, keepdims=True)
            l_scr[u] = jnp.ones((rows, 1), F32)
            acc_scr[u] = by_head(vn_ref[u])

    for u in range(ns):
        qb = qms[u].astype(BF16)
        s = jnp.concatenate(
            [lax.dot_general(qb, k_refs[u * pps + t][...].astype(BF16), _NT,
                             preferred_element_type=F32) for t in range(pps)], axis=1)
        s = jnp.where(srow == scol, s, NEG_BIG)
        m_prev = m_scr[u]
        m_new = jnp.maximum(m_prev, jnp.max(s, axis=-1, keepdims=True))
        alpha = jnp.exp(m_prev - m_new)
        pr = jnp.exp(s - m_new)
        l_scr[u] = alpha * l_scr[u] + jnp.sum(pr, axis=-1, keepdims=True)
        prb = pr.astype(BF16)
        pv = jnp.dot(prb[:, 0:page_rows], v_refs[u * pps][...].astype(BF16),
                     preferred_element_type=F32)
        for t in range(1, pps):
            pv = pv + jnp.dot(prb[:, t * page_rows:(t + 1) * page_rows],
                              v_refs[u * pps + t][...].astype(BF16), preferred_element_type=F32)
        acc_scr[u] = alpha * acc_scr[u] + pv
        m_scr[u] = m_new

    @pl.when(g == pl.num_programs(1) - 1)
    def _():
        for u in range(ns):
            d = acc_scr[u] / l_scr[u]
            for hd in range(A_HEADS):
                o = d[2 * hd:2 * hd + 1, :] - lam * d[2 * hd + 1:2 * hd + 2, :]
                o_ref[u, :, hd * A_DV:(hd + 1) * A_DV] = _subln(o, gain_ref[...])


def _attn_sample(q, k_new, v_new, cache_k, cache_v, page_table, lam_rows, subln_gain):
    nb, n_pages = page_table.shape
    ns, pps, nbuf = SEQS_PER_STEP, PAGES_PER_STEP, PAGE_BUFFERS
    assert n_pages % pps == 0 and nb % ns == 0
    assert (nb // ns) * (n_pages // pps) >= nbuf - 1
    page_rows = PAGE_SIZE * A_HEADS
    ck = cache_k.reshape(cache_k.shape[0], page_rows, A_DV)
    cv = cache_v.reshape(cache_v.shape[0], page_rows, A_DV)
    pt = page_table.reshape(-1)
    tok = pl.BlockSpec((ns, 1, A_WIDTH), lambda b, g, pt: (b, 0, 0))
    const = lambda w: pl.BlockSpec((1, w), lambda b, g, pt: (0, 0))
    hbm = pl.BlockSpec(memory_space=pl.ANY)
    grid_spec = pltpu.PrefetchScalarGridSpec(
        num_scalar_prefetch=1,
        grid=(nb // ns, n_pages // pps),
        in_specs=[tok, tok, tok, const(A_DH), const(A_DH), const(A_DH), const(A_DH), const(A_DV),
                  hbm, hbm],
        out_specs=tok,
        scratch_shapes=[pltpu.VMEM((nbuf, ns * pps, page_rows, A_DV), F32),
                        pltpu.VMEM((nbuf, ns * pps, page_rows, A_DV), F32),
                        pltpu.SemaphoreType.DMA((2, nbuf)),
                        pltpu.VMEM((ns, 2 * A_HEADS, 1), F32), pltpu.VMEM((ns, 2 * A_HEADS, 1), F32),
                        pltpu.VMEM((ns, 2 * A_HEADS, A_DV), F32)],
    )
    tok3 = lambda a: a.reshape(nb, 1, A_WIDTH)
    out = pl.pallas_call(
        _attn_sample_kernel,
        grid_spec=grid_spec,
        out_shape=jax.ShapeDtypeStruct((nb, 1, A_WIDTH), F32),
        compiler_params=_cparams(("arbitrary", "arbitrary")),
        name="attn_sample",
    )(pt, tok3(q), tok3(k_new), tok3(v_new), *lam_rows, subln_gain.reshape(1, A_DV), ck, cv)
    return out.reshape(nb, A_WIDTH)


def _rwkv_prep_kernel(seq_mode, zr_ref, prev_ref, mu_ref, w0_ref, w2_ref, a0_ref, a2_ref, g2_ref,
                      kkp_ref, ka_ref, r_ref, k_ref, v_ref, kk_ref, a_ref, lw_ref, g_ref, *scr):
    z = zr_ref[...]
    if seq_mode:
        (carry,) = scr

        @pl.when(pl.program_id(0) == 0)
        def _():
            carry[...] = prev_ref[...]

        row = lax.broadcasted_iota(jnp.int32, z.shape, 0)
        zp = jnp.where(row == 0, carry[...], pltpu.roll(z, 1, 0))
        carry[...] = z[z.shape[0] - 1:z.shape[0], :]
    else:
        zp = prev_ref[...]
    zs = z + (zp - z) * mu_ref[...]
    r = zs[:, 0:R_OFF_K]
    k = zs[:, R_OFF_K:R_OFF_V]
    v = zs[:, R_OFF_V:R_OFF_W]
    zw = zs[:, R_OFF_W:R_OFF_A]
    za = zs[:, R_OFF_A:R_OFF_G]
    zg = zs[:, R_OFF_G:R_IN]
    w_pre = w0_ref[...] + _dot3(jnp.tanh(zw), w2_ref[...], _NN)
    nx = -w_pre
    softplus = jnp.maximum(nx, 0.0) + jnp.log(1.0 + jnp.exp(-jnp.abs(nx)))
    lw_ref[...] = -jnp.exp(-softplus - 0.5)
    a = _sigmoid(a0_ref[...] + _dot3(za, a2_ref[...], _NN))
    g_ref[...] = _mm1(_sigmoid(zg), g2_ref[...])
    kkr = k * kkp_ref[...]
    seg_sum = _seg_ones(LANES, R_DH)
    for sb in range(R_WIDTH // LANES):
        sl = slice(sb * LANES, (sb + 1) * LANES)
        x = kkr[:, sl]
        ss = _seg_reduce(x * x, seg_sum)
        kk_ref[:, sl] = x / jnp.maximum(jnp.sqrt(ss), 1e-12)
    r_ref[...] = r
    v_ref[...] = v
    a_ref[...] = a
    k_ref[...] = k * (1.0 + (a - 1.0) * ka_ref[...])


def _rwkv_prep(zr, prev, seq_mode, p):
    n = zr.shape[0]
    tm = _row_tile(n, 256)
    row = lambda w: pl.BlockSpec((tm, w), lambda i: (i, 0))
    const = lambda r, w: pl.BlockSpec((r, w), lambda i: (0, 0))
    prev_spec = const(1, R_IN) if seq_mode else row(R_IN)
    vec = lambda a: a.reshape(1, -1)
    return pl.pallas_call(
        functools.partial(_rwkv_prep_kernel, seq_mode),
        grid=(n // tm,),
        in_specs=[row(R_IN), prev_spec, const(1, R_IN), const(1, R_WIDTH),
                  const(DECAY_LORA, R_WIDTH), const(1, R_WIDTH), const(AAA_LORA, R_WIDTH),
                  const(GATE_LORA, R_WIDTH), const(1, R_WIDTH), const(1, R_WIDTH)],
        out_specs=[row(R_WIDTH)] * 7,
        out_shape=[jax.ShapeDtypeStruct((n, R_WIDTH), F32)] * 7,
        scratch_shapes=[pltpu.VMEM((1, R_IN), F32)] if seq_mode else [],
        compiler_params=_cparams(("arbitrary",)),
        name="rwkv_prep_seq" if seq_mode else "rwkv_prep_batch",
    )(zr, prev, vec(p['rw_mu']), vec(p['rw_w0']), p['rw_w2'], vec(p['rw_a0']), p['rw_a2'],
      p['rw_g2'], vec(p['rw_kk']), vec(p['rw_ka']))


def _split(x):
    hi = x.astype(BF16)
    return hi, (x - hi.astype(F32)).astype(BF16)


def _dot3(a, b, dims):
    ah, al = _split(a)
    bh, bl = _split(b)
    d = lambda x, y: lax.dot_general(x, y, dims, preferred_element_type=F32)
    return d(ah, bh) + (d(ah, bl) + d(al, bh))


def _mm1(a, b):
    return jnp.dot(a.astype(BF16), b.astype(BF16), preferred_element_type=F32)


def _mm(a, b):
    return _dot3(a, b, _NN)


def _mm_nt(a, b):
    return _dot3(a, b, _NT)


def _mm_tn(a, b):
    return _dot3(a, b, _TN)


def _wkv_chunk_kernel(nch, r_ref, k_ref, v_ref, kk_ref, a_ref, lw_ref, s0_ref, y_ref, s_ref):
    rows = r_ref.shape[0]
    c = rows // nch

    @pl.when(pl.program_id(0) == 0)
    def _():
        s_ref[...] = s0_ref[...]

    ti = lax.broadcasted_iota(jnp.int32, (c, c), 0)
    si = lax.broadcasted_iota(jnp.int32, (c, c), 1)
    lower = si <= ti
    strict = si < ti
    lw = lw_ref[...]
    bt_i = lax.broadcasted_iota(jnp.int32, (rows, rows), 0)
    bs_i = lax.broadcasted_iota(jnp.int32, (rows, rows), 1)
    same_chunk_lower = jnp.logical_and(bs_i <= bt_i, bs_i // c == bt_i // c)
    cs = jnp.dot(jnp.where(same_chunk_lower, 1.0, 0.0).astype(F32), lw, precision=HI,
                 preferred_element_type=F32)
    chunk_rows = [slice(ci * c, (ci + 1) * c) for ci in range(nch)]
    total = jnp.concatenate(
        [jnp.broadcast_to(cs[(ci + 1) * c - 1:(ci + 1) * c, :], (c, R_WIDTH)) for ci in range(nch)],
        axis=0)
    e_pos = jnp.exp(cs)
    e_prev = jnp.exp(cs - lw)
    e_neg = jnp.exp(-cs)
    e_rem = jnp.exp(total - cs)
    e_tot = jnp.exp(total)
    kk = kk_ref[...]
    k = k_ref[...]
    b = kk * a_ref[...]
    at_all = kk * e_prev
    bt_all = b * e_neg
    kt_all = k * e_neg
    rt_all = r_ref[...] * e_pos
    bh_all = b * e_rem
    kh_all = k * e_rem
    v_all = v_ref[...]
    eye = jnp.where(si == ti, 1.0, 0.0).astype(F32)

    pairs = [(ci, h) for ci in range(nch) for h in range(R_HEADS)]
    heads = range(len(pairs))
    sub = lambda x, p: x[chunk_rows[p[0]], p[1] * R_DH:(p[1] + 1) * R_DH]
    at = [sub(at_all, p) for p in pairs]
    rt = [sub(rt_all, p) for p in pairs]
    v = [sub(v_all, p) for p in pairs]
    a4 = [_mm_nt(jnp.concatenate([at[h], rt[h]], axis=0),
                 jnp.concatenate([sub(bt_all, pairs[h]), sub(kt_all, pairs[h])], axis=0))
          for h in heads]
    aak = [jnp.where(strict, x[0:c, c:2 * c], 0.0) for x in a4]
    arb = [jnp.where(lower, x[c:2 * c, 0:c], 0.0) for x in a4]
    ark = [jnp.where(lower, x[c:2 * c, c:2 * c], 0.0) for x in a4]
    nl = [jnp.where(strict, -x[0:c, 0:c], 0.0) for x in a4]
    inv = [eye + x for x in nl]
    pw = [_mm(x, x) for x in nl]
    span = 2
    while span < c:
        if 2 * span < c:
            both = [_mm1(jnp.concatenate([inv[h], pw[h]], axis=0), pw[h]) for h in heads]
            inv = [inv[h] + both[h][0:c, :] for h in heads]
            pw = [x[c:2 * c, :] for x in both]
        else:
            inv = [inv[h] + _mm1(inv[h], pw[h]) for h in heads]
        span *= 2
    av = [_mm(jnp.concatenate([aak[h], ark[h]], axis=0), v[h]) for h in heads]
    tw = [_mm(inv[h], jnp.concatenate([at[h], av[h][0:c, :]], axis=1)) for h in heads]
    kv = [_mm_tn(v[h], sub(kh_all, pairs[h])) for h in heads]
    state = [s_ref[hd] for hd in range(R_HEADS)]
    for ci in range(nch):
        idx = [ci * R_HEADS + hd for hd in range(R_HEADS)]
        hs = [_mm_nt(jnp.concatenate([-tw[h][:, 0:R_DH], rt[h]], axis=0), state[h % R_HEADS])
              for h in idx]
        u = [hs[j][0:c, :] - tw[h][:, R_DH:2 * R_DH] for j, h in enumerate(idx)]
        au = [_mm(arb[h], u[j]) for j, h in enumerate(idx)]
        ub = [_mm_tn(u[j], sub(bh_all, pairs[h])) for j, h in enumerate(idx)]
        for j, h in enumerate(idx):
            p = pairs[h]
            lanes = slice(p[1] * R_DH, (p[1] + 1) * R_DH)
            y_ref[chunk_rows[ci], lanes] = hs[j][c:2 * c, :] + au[j] + av[h][c:2 * c, :]
            state[j] = state[j] * e_tot[ci * c:ci * c + 1, lanes] + ub[j] + kv[h]
    for hd in range(R_HEADS):
        s_ref[hd] = state[hd]


def _wkv_chunk(r, k, v, kk, a, lw, s0):
    n = r.shape[0]
    c = _row_tile(n, WKV_CHUNK)
    nch = WKV_CHUNKS_PER_STEP if n % (c * WKV_CHUNKS_PER_STEP) == 0 else 1
    row = pl.BlockSpec((c * nch, R_WIDTH), lambda i: (i, 0))
    st = pl.BlockSpec((R_HEADS, R_DH, R_DH), lambda i: (0, 0, 0))
    return pl.pallas_call(
        functools.partial(_wkv_chunk_kernel, nch),
        grid=(n // (c * nch),),
        in_specs=[row] * 6 + [st],
        out_specs=[row, st],
        out_shape=[jax.ShapeDtypeStruct((n, R_WIDTH), F32),
                   jax.ShapeDtypeStruct((R_HEADS, R_DH, R_DH), F32)],
        compiler_params=_cparams(("arbitrary",)),
        name="wkv_chunk",
    )(r, k, v, kk, a, lw, s0)


def _wkv_step_kernel(s_ref, r_ref, k_ref, v_ref, kk_ref, a_ref, lw_ref, y_ref, so_ref):
    s = s_ref[...]
    kk = kk_ref[...]
    sa = -jnp.sum(s * kk, axis=-1, keepdims=True)
    s2 = s * jnp.exp(lw_ref[...]) + sa * (kk * a_ref[...]) + v_ref[...] * k_ref[...]
    so_ref[...] = s2
    y_ref[...] = jnp.sum(s2 * r_ref[...], axis=-1, keepdims=True)


def _wkv_step(state, r, k, v, kk, a, lw):
    nb = state.shape[0]
    bs = _row_tile(nb, 8)
    rowv = lambda x: x.reshape(nb, R_HEADS, 1, R_DH)
    st = pl.BlockSpec((bs, R_HEADS, R_DH, R_DH), lambda i: (i, 0, 0, 0))
    rw = pl.BlockSpec((bs, R_HEADS, 1, R_DH), lambda i: (i, 0, 0, 0))
    col = pl.BlockSpec((bs, R_HEADS, R_DH, 1), lambda i: (i, 0, 0, 0))
    y, s_new = pl.pallas_call(
        _wkv_step_kernel,
        grid=(nb // bs,),
        in_specs=[st, rw, rw, col, rw, rw, rw],
        out_specs=[col, st],
        out_shape=[jax.ShapeDtypeStruct((nb, R_HEADS, R_DH, 1), F32),
                   jax.ShapeDtypeStruct(state.shape, F32)],
        compiler_params=_cparams(("arbitrary",)),
        name="wkv_step",
    )(state, rowv(r), rowv(k), v.reshape(nb, R_HEADS, R_DH, 1), rowv(kk), rowv(a), rowv(lw))
    return y.reshape(nb, R_WIDTH), s_new


def _rwkv_post_kernel(y_ref, r_ref, k_ref, v_ref, g_ref, lnw_ref, lnb_ref, rk_ref, o_ref):
    seg_mean = _seg_ones(LANES, R_DH, 1.0 / R_DH)
    seg_sum = _seg_ones(LANES, R_DH)
    for sb in range(R_WIDTH // LANES):
        sl = slice(sb * LANES, (sb + 1) * LANES)
        y = y_ref[:, sl]
        mean = _seg_reduce(y, seg_mean)
        d = y - mean
        var = _seg_reduce(d * d, seg_mean)
        yn = d * lax.rsqrt(var + GN_EPS) * lnw_ref[:, sl] + lnb_ref[:, sl]
        bonus = _seg_reduce(r_ref[:, sl] * k_ref[:, sl] * rk_ref[:, sl], seg_sum)
        o_ref[:, sl] = (yn + bonus * v_ref[:, sl]) * g_ref[:, sl]


def _rwkv_post(y, r, k, v, g, p):
    n = y.shape[0]
    tm = _row_tile(n, 512)
    row = pl.BlockSpec((tm, R_WIDTH), lambda i: (i, 0))
    const = pl.BlockSpec((1, R_WIDTH), lambda i: (0, 0))
    vec = lambda a: a.reshape(1, R_WIDTH)
    return pl.pallas_call(
        _rwkv_post_kernel,
        grid=(n // tm,),
        in_specs=[row] * 5 + [const] * 3,
        out_specs=row,
        out_shape=jax.ShapeDtypeStruct((n, R_WIDTH), F32),
        compiler_params=_cparams(("arbitrary",)),
        name="rwkv_post",
    )(y, r, k, v, g, vec(p['rw_ln_w']), vec(p['rw_ln_b']), vec(p['rw_rk']))


def _merge_kernel(x_ref, o_ref, ro_ref, ga_ref, gr_ref, gt_ref, sc_ref, sh_ref, g_ref,
                  wa_ref, wr_ref, wo_ref, wrt_ref, brt_ref, x1_ref, h2_ref, rt_ref):
    tm = x_ref.shape[0]
    rc = min(tm, MERGE_ROW_CHUNK)
    mod = lambda ref, sl: ref[...] if ref.shape[0] == 1 else ref[sl, :]
    for c in range(tm // rc):
        sl = slice(c * rc, (c + 1) * rc)
        ma = jnp.dot(o_ref[sl, :].astype(BF16), wa_ref[...], preferred_element_type=F32)
        mr = jnp.dot(ro_ref[sl, :].astype(BF16), wr_ref[...], preferred_element_type=F32)
        mg = ga_ref[sl, :].astype(F32) * ma + gr_ref[sl, :].astype(F32) * mr
        merged = jnp.dot(mg.astype(BF16), wo_ref[...], preferred_element_type=F32)
        x1 = x_ref[sl, :] + mod(gt_ref, sl) * merged
        x1_ref[sl, :] = x1
        ms = jnp.mean(x1 * x1, axis=-1, keepdims=True)
        h2 = x1 * lax.rsqrt(ms + RMS_EPS) * g_ref[...]
        h2 = h2 * (1.0 + mod(sc_ref, sl)) + mod(sh_ref, sl)
        h2_ref[sl, :] = h2
        logits = _dot3(h2, wrt_ref[...], _NN) + brt_ref[...]
        rt_ref[sl, :] = _route(logits)


def _route(logits):
    lane = lax.broadcasted_iota(jnp.int32, logits.shape, 1)
    lane_f = lane.astype(F32)
    first_max = lambda x, m: jnp.min(jnp.where(x == m, lane_f, float(LANES)), axis=-1, keepdims=True)
    is_g = lane < N_GROUPS
    lg = jnp.where(is_g, logits, NEG_BIG)
    gmax = jnp.max(lg, axis=-1, keepdims=True)
    g_idx = first_max(lg, gmax)
    sum_g = jnp.sum(jnp.where(is_g, jnp.exp(lg - gmax), 0.0), axis=-1, keepdims=True)
    group_of_lane = ((lane - N_GROUPS) // EXPERTS_PER_GROUP).astype(F32)
    in_group = jnp.where(lane >= N_GROUPS, group_of_lane, -1.0) == g_idx
    le = jnp.where(in_group, logits, NEG_BIG)
    m1 = jnp.max(le, axis=-1, keepdims=True)
    i1 = first_max(le, m1)
    le2 = jnp.where(lane_f == i1, NEG_BIG, le)
    m2 = jnp.max(le2, axis=-1, keepdims=True)
    i2 = first_max(le2, m2)
    t = jnp.exp(m2 - m1)
    w1 = 1.0 / (sum_g * (1.0 + t))
    out = jnp.where(lane == 0, i1 - N_GROUPS, 0.0)
    out = jnp.where(lane == 1, i2 - N_GROUPS, out)
    out = jnp.where(lane == 2, w1, out)
    return jnp.where(lane == 3, w1 * t, out)


def _merge(x, o, ro, ga, gr, gt, sc, sh, g_ffn, wa_bf, wr_bf, wo_bf, w_router, b_router):
    n = x.shape[0]
    tm = _row_tile(n, 512)
    row = lambda w: pl.BlockSpec((tm, w), lambda i: (i, 0))
    const = lambda r, w: pl.BlockSpec((r, w), lambda i: (0, 0))
    mod = lambda a: _mod_spec(a.shape[0], tm)
    return pl.pallas_call(
        _merge_kernel,
        grid=(n // tm,),
        in_specs=[row(D_MODEL), row(A_WIDTH), row(R_WIDTH), row(D_MODEL), row(D_MODEL),
                  mod(gt), mod(sc), mod(sh), const(1, D_MODEL),
                  const(A_WIDTH, D_MODEL), const(R_WIDTH, D_MODEL), const(D_MODEL, D_MODEL),
                  const(D_MODEL, ROUTER_PAD), const(1, ROUTER_PAD)],
        out_specs=[row(D_MODEL), row(D_MODEL), row(ROUTER_PAD)],
        out_shape=[jax.ShapeDtypeStruct((n, D_MODEL), F32), jax.ShapeDtypeStruct((n, D_MODEL), F32),
                   jax.ShapeDtypeStruct((n, ROUTER_PAD), F32)],
        compiler_params=_cparams(("arbitrary",)),
        name="merge",
    )(x, o, ro, ga, gr, gt, sc, sh, g_ffn.reshape(1, D_MODEL), wa_bf, wr_bf, wo_bf,
      w_router, b_router)


def _rank_kernel(rt_ref, pos_ref, cnt_ref, carry):
    tm = rt_ref.shape[0]

    @pl.when(pl.program_id(0) == 0)
    def _():
        carry[...] = jnp.zeros(carry.shape, F32)

    rt = rt_ref[...]
    lane = lax.broadcasted_iota(jnp.int32, rt.shape, 1)
    lane_f = lane.astype(F32)
    oh0 = jnp.where(lane_f == rt[:, 0:1], 1.0, 0.0)
    oh1 = jnp.where(lane_f == rt[:, 1:2], 1.0, 0.0)
    ti = lax.broadcasted_iota(jnp.int32, (tm, tm), 0)
    si = lax.broadcasted_iota(jnp.int32, (tm, tm), 1)
    earlier = jnp.where(si < ti, 1.0, 0.0).astype(BF16)
    pre = jnp.dot(earlier, jnp.concatenate([oh0, oh1], axis=1).astype(BF16),
                  preferred_element_type=F32)
    c = carry[...]
    rank0 = jnp.sum(oh0 * (pre[:, 0:LANES] + c[0:1, :]), axis=-1, keepdims=True)
    rank1 = jnp.sum(oh1 * (pre[:, LANES:2 * LANES] + c[1:2, :]), axis=-1, keepdims=True)
    pos_ref[...] = jnp.where(lane == 0, rank0, jnp.where(lane == 1, rank1, 0.0))
    row = lax.broadcasted_iota(jnp.int32, c.shape, 0)
    c = c + jnp.where(row == 0, jnp.sum(oh0, axis=0, keepdims=True), 0.0) \
          + jnp.where(row == 1, jnp.sum(oh1, axis=0, keepdims=True), 0.0)
    carry[...] = c
    cnt_ref[...] = c


def _rank(route):
    n = route.shape[0]
    tm = _row_tile(n, 256)
    return pl.pallas_call(
        _rank_kernel,
        grid=(n // tm,),
        in_specs=[pl.BlockSpec((tm, LANES), lambda i: (i, 0))],
        out_specs=[pl.BlockSpec((tm, LANES), lambda i: (i, 0)),
                   pl.BlockSpec((8, LANES), lambda i: (0, 0))],
        out_shape=[jax.ShapeDtypeStruct((n, LANES), F32), jax.ShapeDtypeStruct((8, LANES), F32)],
        scratch_shapes=[pltpu.VMEM((8, LANES), F32)],
        compiler_params=_cparams(("arbitrary",)),
        name="moe_rank",
    )(route)


def _slots_kernel(bm, rt_ref, pos_ref, cnt_ref, dest_ref, blk_ref):
    cnt = cnt_ref[...]
    lane = lax.broadcasted_iota(jnp.int32, cnt.shape, 1)
    is_expert = lane < N_EXPERTS
    c0 = jnp.broadcast_to(cnt[0:1, :], cnt.shape)
    padded = jnp.floor((c0 + cnt[1:2, :] + (bm - 1)) * (1.0 / bm)) * bm
    src = lax.broadcasted_iota(jnp.int32, (LANES, LANES), 0)
    dst = lax.broadcasted_iota(jnp.int32, (LANES, LANES), 1)
    pad_end = jnp.dot(padded, jnp.where(src <= dst, 1.0, 0.0).astype(F32), precision=HI,
                      preferred_element_type=F32)
    pad_start = pad_end - padded
    rt = rt_ref[...]
    pos = pos_ref[...]
    tlane = lax.broadcasted_iota(jnp.int32, rt.shape, 1)
    tlane_f = tlane.astype(F32)
    pick = lambda e, table: jnp.sum(jnp.where(tlane_f == e, table[0:1, :], 0.0), axis=-1, keepdims=True)
    d0 = pick(rt[:, 0:1], pad_start) + pos[:, 0:1]
    d1 = pick(rt[:, 1:2], pad_start + c0) + pos[:, 1:2]
    dest_ref[...] = jnp.where(tlane == 0, d0, jnp.where(tlane == 1, d1, 0.0)).astype(jnp.int32)
    nb = blk_ref.shape[0]
    blane = lax.broadcasted_iota(jnp.int32, (nb, LANES), 1)
    start = (lax.broadcasted_iota(jnp.int32, (nb, LANES), 0) * bm).astype(F32)
    ends = jnp.where(blane < N_EXPERTS, pad_end[0:1, :], 3e38)
    expert = jnp.minimum(jnp.sum(jnp.where(ends <= start, 1.0, 0.0), axis=-1, keepdims=True),
                         N_EXPERTS - 1.0)
    total = jnp.max(jnp.where(is_expert, pad_end, 0.0), axis=-1, keepdims=True)[0:1, :]
    used = jnp.where(start < total, 1.0, 0.0)
    blk_ref[...] = jnp.where(blane == 0, expert, jnp.where(blane == 1, used, 0.0)).astype(jnp.int32)


def _slots(route, pos, cnt, n_blocks):
    n = route.shape[0]
    tm = _row_tile(n, 512)
    nbp = -(-n_blocks // 8) * 8
    row = pl.BlockSpec((tm, LANES), lambda i: (i, 0))
    return pl.pallas_call(
        functools.partial(_slots_kernel, MOE_ROWS),
        grid=(n // tm,),
        in_specs=[row, row, pl.BlockSpec((8, LANES), lambda i: (0, 0))],
        out_specs=[row, pl.BlockSpec((nbp, LANES), lambda i: (0, 0))],
        out_shape=[jax.ShapeDtypeStruct((n, LANES), jnp.int32),
                   jax.ShapeDtypeStruct((nbp, LANES), jnp.int32)],
        compiler_params=_cparams(("arbitrary",)),
        name="moe_slots",
    )(route, pos, cnt)


def _row_copy(src, src_row, dst, dst_row, sem):
    return pltpu.make_async_copy(src.at[pl.ds(src_row, 1), :], dst.at[pl.ds(dst_row, 1), :], sem)


def _dispatch_kernel(d0_ref, d1_ref, x_ref, xb_in, xb_ref, sem):
    del xb_in
    tm = x_ref.shape[0]
    base = pl.program_id(0) * tm

    def issue(t, carry):
        _row_copy(x_ref, t, xb_ref, d0_ref[base + t], sem).start()
        _row_copy(x_ref, t, xb_ref, d1_ref[base + t], sem).start()
        return carry

    lax.fori_loop(0, tm, issue, 0, unroll=8)
    for _ in range(TOP_K):
        pltpu.make_async_copy(x_ref, xb_ref.at[pl.ds(0, tm), :], sem).wait()


def _dispatch(h2, dest0, dest1, rows):
    n = h2.shape[0]
    tm = _row_tile(n, 256)
    grid_spec = pltpu.PrefetchScalarGridSpec(
        num_scalar_prefetch=2,
        grid=(n // tm,),
        in_specs=[pl.BlockSpec((tm, D_MODEL), lambda i, d0, d1: (i, 0)),
                  pl.BlockSpec(memory_space=pl.ANY)],
        out_specs=pl.BlockSpec(memory_space=pl.ANY),
        scratch_shapes=[pltpu.SemaphoreType.DMA(())],
    )
    return pl.pallas_call(
        _dispatch_kernel,
        grid_spec=grid_spec,
        out_shape=jax.ShapeDtypeStruct((rows, D_MODEL), F32),
        input_output_aliases={3: 0},
        compiler_params=_cparams(("arbitrary",)),
        name="moe_dispatch",
    )(dest0, dest1, h2, jnp.zeros((rows, D_MODEL), F32))


def _expert_kernel(be_ref, nv_ref, x_ref, wg_ref, wu_ref, wd_ref, y_ref):
    i = pl.program_id(0)
    del be_ref

    @pl.when(nv_ref[i] > 0)
    def _():
        xb = x_ref[...].astype(BF16)
        gate = jnp.dot(xb, wg_ref[...].astype(BF16), preferred_element_type=F32)
        up = jnp.dot(xb, wu_ref[...].astype(BF16), preferred_element_type=F32)
        hdn = gate * _sigmoid(gate) * up
        y_ref[...] = jnp.dot(hdn.astype(BF16), wd_ref[...].astype(BF16),
                             preferred_element_type=F32)

    @pl.when(nv_ref[i] == 0)
    def _():
        y_ref[...] = jnp.zeros(y_ref.shape, F32)


def _experts(xb, blk_e, blk_used, w_gate, w_up, w_down):
    rows = xb.shape[0]
    bm = MOE_ROWS
    wspec = lambda a, b: pl.BlockSpec((None, a, b), lambda i, be, nv: (be[i], 0, 0))
    grid_spec = pltpu.PrefetchScalarGridSpec(
        num_scalar_prefetch=2,
        grid=(rows // bm,),
        in_specs=[pl.BlockSpec((bm, D_MODEL), lambda i, be, nv: (i, 0)),
                  wspec(D_MODEL, D_EXPERT), wspec(D_MODEL, D_EXPERT), wspec(D_EXPERT, D_MODEL)],
        out_specs=pl.BlockSpec((bm, D_MODEL), lambda i, be, nv: (i, 0)),
    )
    return pl.pallas_call(
        _expert_kernel,
        grid_spec=grid_spec,
        out_shape=jax.ShapeDtypeStruct((rows, D_MODEL), F32),
        compiler_params=_cparams(("arbitrary",)),
        name="experts",
    )(blk_e, blk_used, xb, w_gate, w_up, w_down)


def _combine_kernel(d0_ref, d1_ref, x1_ref, rt_ref, gt_ref, yb_ref, o_ref, ya_scr, yb_scr, sem):
    tm = x1_ref.shape[0]
    base = pl.program_id(0) * tm

    def issue(t, carry):
        _row_copy(yb_ref, d0_ref[base + t], ya_scr, t, sem).start()
        _row_copy(yb_ref, d1_ref[base + t], yb_scr, t, sem).start()
        return carry

    lax.fori_loop(0, tm, issue, 0, unroll=8)
    pltpu.make_async_copy(yb_ref.at[pl.ds(0, tm), :], ya_scr, sem).wait()
    pltpu.make_async_copy(yb_ref.at[pl.ds(0, tm), :], yb_scr, sem).wait()
    rt = rt_ref[...]
    moe = rt[:, 2:3] * ya_scr[...] + rt[:, 3:4] * yb_scr[...]
    o_ref[...] = x1_ref[...] + gt_ref[...] * moe


def _combine(x1, route, gt, yb, dest0, dest1):
    n = x1.shape[0]
    tm = _row_tile(n, 256)
    gt_spec = (pl.BlockSpec((1, D_MODEL), lambda i, d0, d1: (0, 0)) if gt.shape[0] == 1
               else pl.BlockSpec((tm, D_MODEL), lambda i, d0, d1: (i, 0)))
    grid_spec = pltpu.PrefetchScalarGridSpec(
        num_scalar_prefetch=2,
        grid=(n // tm,),
        in_specs=[pl.BlockSpec((tm, D_MODEL), lambda i, d0, d1: (i, 0)),
                  pl.BlockSpec((tm, LANES), lambda i, d0, d1: (i, 0)),
                  gt_spec, pl.BlockSpec(memory_space=pl.ANY)],
        out_specs=pl.BlockSpec((tm, D_MODEL), lambda i, d0, d1: (i, 0)),
        scratch_shapes=[pltpu.VMEM((tm, D_MODEL), F32), pltpu.VMEM((tm, D_MODEL), F32),
                        pltpu.SemaphoreType.DMA(())],
    )
    return pl.pallas_call(
        _combine_kernel,
        grid_spec=grid_spec,
        out_shape=jax.ShapeDtypeStruct((n, D_MODEL), F32),
        compiler_params=_cparams(("arbitrary",)),
        name="moe_combine",
    )(dest0, dest1, x1, route, gt, yb)


def _moe(h2, route, x1, gt, w_gate, w_up, w_down):
    n = h2.shape[0]
    bm = MOE_ROWS
    pos, cnt = _rank(route)
    n_blocks = -(-(n * TOP_K) // bm) + N_EXPERTS
    dest, blk = _slots(route, pos, cnt, n_blocks)
    dest0, dest1 = dest[:, 0], dest[:, 1]
    xb = _dispatch(h2, dest0, dest1, n_blocks * bm)
    yb = _experts(xb, blk[:n_blocks, 0], blk[:n_blocks, 1], w_gate, w_up, w_down)
    return _combine(x1, route, gt, yb, dest0, dest1)


def _layer(x, mod, pos, p, w, attend, rwkv):
    sh1, sc1, gt1, sh2, sc2, gt2 = [mod[:, i * D_MODEL:(i + 1) * D_MODEL] for i in range(6)]
    cos, sin = _rope_tables(pos)
    q, k, v, ga, gr, zr, kb, vb = _in_proj(x, sc1, sh1, p['g_mix'], w['w_in'], p['q_gain'],
                                           p['k_gain'], cos, sin)
    o = attend(q, k, v, kb, vb)
    ro, wkv1 = rwkv(zr)
    x1, h2, logits = _merge(x, o, ro, ga, gr, gt1, sc2, sh2, p['g_ffn'], w['w_br_a'], w['w_br_r'],
                            w['w_o'], w['w_router'], w['b_router'])
    y = _moe(h2, logits, x1, gt2, p['w_e_gate'], p['w_e_up'], p['w_e_down'])
    return y, k, v, wkv1, zr


def kernel(x_prompt, x_sample, cache_k, cache_v, state_wkv, state_shift, page_table, c_prompt, c_sample, w_ada, b_ada, g_mix, g_ffn, w_in, q_gain, k_gain, lam_q1, lam_k1, lam_q2, lam_k2, subln_gain, rw_mu, rw_w0, rw_w2, rw_a0, rw_a2, rw_g2, rw_kk, rw_ka, rw_rk, rw_ln_w, rw_ln_b, w_br_a, w_br_r, w_o, w_rg, b_rg, w_re, b_re, w_e_gate, w_e_up, w_e_down):
    assert w_ada.shape[0] == 1, "single-layer kernel"
    B, S, _ = x_prompt.shape
    DB, T, _ = x_sample.shape
    assert B == 1 and T == 1
    past = page_table.shape[1] * PAGE_SIZE
    p = dict(g_mix=g_mix[0], g_ffn=g_ffn[0], q_gain=q_gain[0], k_gain=k_gain[0],
             rw_mu=rw_mu[0], rw_w0=rw_w0[0], rw_w2=rw_w2[0], rw_a0=rw_a0[0], rw_a2=rw_a2[0],
             rw_g2=rw_g2[0], rw_kk=rw_kk[0], rw_ka=rw_ka[0], rw_rk=rw_rk[0],
             rw_ln_w=rw_ln_w[0], rw_ln_b=rw_ln_b[0],
             w_e_gate=w_e_gate[0], w_e_up=w_e_up[0], w_e_down=w_e_down[0])
    pad = ROUTER_PAD - N_GROUPS - N_EXPERTS
    w = dict(w_in=w_in[0].astype(BF16), w_br_a=w_br_a[0].astype(BF16),
             w_br_r=w_br_r[0].astype(BF16), w_o=w_o[0].astype(BF16),
             w_router=jnp.concatenate([w_rg[0], w_re[0], jnp.zeros((D_MODEL, pad), F32)], axis=1),
             b_router=jnp.concatenate([b_rg[0], b_re[0], jnp.zeros((pad,), F32)]).reshape(1, -1))
    lam_rows = [a.reshape(1, A_DH) for a in (lam_q1[0], lam_k1[0], lam_q2[0], lam_k2[0])]

    c_all = jnp.concatenate([c_prompt, jnp.zeros((7, D_MODEL), F32), c_sample], axis=0)
    mod = _ada(c_all, w_ada[0], b_ada[0])
    mod_p, mod_s = mod[0:1], mod[8:8 + DB]

    def rwkv_prompt(zr):
        r, k, v, kk, a, lw, g = _rwkv_prep(zr, jnp.zeros((1, R_IN), F32), True, p)
        y, s1 = _wkv_chunk(r, k, v, kk, a, lw, jnp.zeros((R_HEADS, R_DH, R_DH), F32))
        return _rwkv_post(y, r, k, v, g, p), s1

    def rwkv_sample(zr):
        r, k, v, kk, a, lw, g = _rwkv_prep(zr, state_shift[0], False, p)
        y, s1 = _wkv_step(state_wkv[0], r, k, v, kk, a, lw)
        return _rwkv_post(y, r, k, v, g, p), s1

    attend_p = lambda q, k, v, kb, vb: _attn_prompt(q, kb, vb, lam_rows, subln_gain[0])
    attend_s = lambda q, k, v, kb, vb: _attn_sample(q, k, v, cache_k[0], cache_v[0], page_table,
                                                    lam_rows, subln_gain[0])

    yp, kp, vp, wp, zrp = _layer(x_prompt[0], mod_p, jnp.arange(S), p, w, attend_p, rwkv_prompt)
    ys, ks_, vs_, ws_, zrs = _layer(x_sample[:, 0], mod_s, jnp.full((DB,), past), p, w, attend_s,
                                    rwkv_sample)
    return (yp.reshape(1, S, D_MODEL), ys.reshape(DB, 1, D_MODEL),
            kp.reshape(1, 1, S, A_HEADS, 2 * A_DH), vp.reshape(1, 1, S, A_HEADS, A_DV),
            wp.reshape(1, 1, R_HEADS, R_DH, R_DH), zrp[S - 1:S].reshape(1, 1, R_IN),
            ks_.reshape(1, DB, 1, A_HEADS, 2 * A_DH), vs_.reshape(1, DB, 1, A_HEADS, A_DV),
            ws_.reshape(1, DB, R_HEADS, R_DH, R_DH), zrs.reshape(1, DB, R_IN))
```
